```python
import math
import jax, jax.numpy as jnp
from jax import lax
import numpy as np

D_MODEL = 2048
BATCH = 8
SEQ = 2048
DEPTH = 1

HEAD_DIM = 128
NAT_WIDTH = D_MODEL // 2
DIFF_WIDTH = D_MODEL - NAT_WIDTH
N_NAT_HEADS = NAT_WIDTH // HEAD_DIM
N_DIFF_HEADS = DIFF_WIDTH // (2 * HEAD_DIM)
MIX_WIDTH = NAT_WIDTH + DIFF_WIDTH
IN_WIDTH = 3 * NAT_WIDTH + 3 * DIFF_WIDTH
GRID_W = 64
NAT_KR_MAX = 8
NAT_KC = 16
N_BUCKETS = 32
MAX_DISTANCE = 128
Q_BLOCK = 128
N_EXPERTS = 32
TOP_K = 4
D_FF = D_MODEL
SWIGLU_LIMIT = 7.0
SWIGLU_ALPHA = 1.702
RMS_EPS = 1e-6
NEG_INF = -1e30

kernel_name = 'hybrid_natten_diffattn_moe_block'


def rms_norm(x, g):
    xf = x.astype(jnp.float32)
    y = xf * lax.rsqrt(jnp.mean(xf * xf, axis=-1, keepdims=True) + RMS_EPS)
    return (y * g.astype(jnp.float32)).astype(x.dtype)


def t5_bucket(rel):
    nb = N_BUCKETS // 2
    max_exact = nb // 2
    ret = jnp.where(rel > 0, nb, 0)
    n = jnp.abs(rel)
    nf = jnp.maximum(n, 1).astype(jnp.float32)
    large = max_exact + (jnp.log(nf / max_exact) / math.log(MAX_DISTANCE / max_exact)
                         * (nb - max_exact)).astype(jnp.int32)
    large = jnp.minimum(large, nb - 1)
    return ret + jnp.where(n < max_exact, n, large)


def neighbourhood_attention(q, k, v, rpb):
    B, S, H, Dh = q.shape
    rows = S // GRID_W
    kr = min(NAT_KR_MAX, rows)
    scale = Dh ** -0.5

    def grid(t):
        return t.reshape(B, rows, GRID_W, H, Dh).transpose(0, 3, 1, 2, 4)

    qg, kg, vg = grid(q), grid(k), grid(v)
    r = jnp.arange(rows)
    row_start = jnp.clip(r - kr // 2, 0, rows - kr)
    row_idx = row_start[:, None] + jnp.arange(kr)[None, :]
    k_band = kg[:, :, row_idx]
    v_band = vg[:, :, row_idx]
    col = jnp.arange(GRID_W)
    col_start = jnp.clip(col - NAT_KC // 2, 0, GRID_W - NAT_KC)
    col_mask = (col[None, :] >= col_start[:, None]) & (col[None, :] < col_start[:, None] + NAT_KC)
    row_off = row_idx - r[:, None] + (NAT_KR_MAX - 1)
    col_off = jnp.clip(col[None, :] - col[:, None], -(NAT_KC - 1), NAT_KC - 1) + (NAT_KC - 1)
    bias = jnp.take(rpb[:, row_off], col_off, axis=-1)
    bias = bias.transpose(0, 1, 3, 2, 4).astype(jnp.float32)
    logits = jnp.einsum('bhrqd,bhrkcd->bhrqkc', qg, k_band).astype(jnp.float32) * scale + bias[None]
    logits = jnp.where(col_mask[:, None, :], logits, NEG_INF)
    p = jax.nn.softmax(logits, axis=(-2, -1))
    out = jnp.einsum('bhrqkc,bhrkcd->bhrqd', p.astype(v.dtype), v_band)
    return out.transpose(0, 2, 3, 1, 4).reshape(B, S, H * Dh)


def diff_attention(q, k, v, rel_table, lambda_full, sub_g, lambda_init):
    B, S, H, _, Dh = q.shape
    nblk = S // Q_BLOCK
    scale = Dh ** -0.5
    qb = q.reshape(B, nblk, Q_BLOCK, H, 2, Dh).transpose(1, 0, 3, 4, 2, 5)
    kt = k.transpose(0, 2, 3, 1, 4)
    vt = v.transpose(0, 2, 1, 3)
    kpos = jnp.arange(S, dtype=jnp.int32)

    def block(args):
        q_blk, start = args
        qpos = start + jnp.arange(Q_BLOCK, dtype=jnp.int32)
        bias = rel_table[t5_bucket(kpos[None, :] - qpos[:, None])]
        bias = bias.transpose(2, 0, 1).astype(jnp.float32)
        logits = jnp.einsum('bhpqd,bhpkd->bhpqk', q_blk, kt).astype(jnp.float32) * scale
        p = jax.nn.softmax(logits + bias[None, :, None], axis=-1)
        a = p[:, :, 0] - lambda_full * p[:, :, 1]
        return jnp.einsum('bhqk,bhkd->bhqd', a.astype(vt.dtype), vt)

    starts = jnp.arange(nblk, dtype=jnp.int32) * Q_BLOCK
    o = lax.map(block, (qb, starts))
    o = rms_norm(o, sub_g) * (1.0 - lambda_init)
    return o.transpose(1, 0, 3, 2, 4).reshape(B, S, H * 2 * Dh)


def moe(h, router_w, router_b, w_gu, b_gu, w_dn, b_dn):
    B, S, D = h.shape
    t = h.reshape(-1, D)
    logits = (t @ router_w).astype(jnp.float32) + router_b.astype(jnp.float32)
    top_vals, top_idx = lax.top_k(logits, TOP_K)
    gates = jax.nn.softmax(top_vals, axis=-1)
    flat_e = top_idx.reshape(-1)
    order = jnp.argsort(flat_e)
    tok = order // TOP_K
    e_sorted = flat_e[order]
    g_sorted = gates.reshape(-1)[order]
    group_sizes = jnp.bincount(flat_e, length=N_EXPERTS).astype(jnp.int32)
    xs = t[tok]
    gu = lax.ragged_dot(xs, w_gu, group_sizes) + b_gu[e_sorted]
    gate = jnp.minimum(gu[:, :D_FF], SWIGLU_LIMIT)
    up = jnp.clip(gu[:, D_FF:], -SWIGLU_LIMIT, SWIGLU_LIMIT)
    act = (up + 1.0) * gate * jax.nn.sigmoid(gate * SWIGLU_ALPHA)
    y = lax.ragged_dot(act, w_dn, group_sizes) + b_dn[e_sorted]
    y = y * g_sorted[:, None].astype(y.dtype)
    out = jnp.zeros_like(t).at[tok].add(y)
    return out.reshape(B, S, D)


def setup_inputs(seed: int = 0) -> dict:
    key = jax.random.key(seed)
    ks = jax.random.split(key, 24)

    def nrm(k, shape, s):
        return jax.random.normal(k, shape, dtype=jnp.float32) * s

    def gain(k, shape):
        return 1.0 + nrm(k, shape, 0.01)

    return {
        'x': nrm(ks[0], (BATCH, SEQ, D_MODEL), 1.0),
        'c': nrm(ks[1], (BATCH, D_MODEL), 1.0),
        'w_ada': nrm(ks[2], (DEPTH, D_MODEL, 6 * D_MODEL), 0.5 * D_MODEL ** -0.5),
        'b_ada': nrm(ks[3], (DEPTH, 6 * D_MODEL), 0.01),
        'norm1_g': gain(ks[4], (DEPTH, D_MODEL)),
        'w_in': nrm(ks[5], (DEPTH, D_MODEL, IN_WIDTH), D_MODEL ** -0.5),
        'nat_q_g': gain(ks[6], (DEPTH, HEAD_DIM)),
        'nat_k_g': gain(ks[7], (DEPTH, HEAD_DIM)),
        'nat_rpb': nrm(ks[8], (DEPTH, N_NAT_HEADS, 2 * NAT_KR_MAX - 1, 2 * NAT_KC - 1), 0.1),
        'diff_q_g': gain(ks[9], (DEPTH, HEAD_DIM)),
        'diff_k_g': gain(ks[10], (DEPTH, HEAD_DIM)),
        'diff_lambda': nrm(ks[11], (DEPTH, 4, HEAD_DIM), 0.1),
        'diff_sub_g': gain(ks[12], (DEPTH, 2 * HEAD_DIM)),
        'rel_bias_table': nrm(ks[13], (N_BUCKETS, N_DIFF_HEADS), 0.1),
        'w_out': nrm(ks[14], (DEPTH, MIX_WIDTH, D_MODEL), MIX_WIDTH ** -0.5),
        'norm2_g': gain(ks[15], (DEPTH, D_MODEL)),
        'router_w': nrm(ks[16], (DEPTH, D_MODEL, N_EXPERTS), D_MODEL ** -0.5),
        'router_b': nrm(ks[17], (DEPTH, N_EXPERTS), 0.01),
        'w_gate_up': nrm(ks[18], (DEPTH, N_EXPERTS, D_MODEL, 2 * D_FF), D_MODEL ** -0.5),
        'b_gate_up': nrm(ks[19], (DEPTH, N_EXPERTS, 2 * D_FF), 0.01),
        'w_down': nrm(ks[20], (DEPTH, N_EXPERTS, D_FF, D_MODEL), D_FF ** -0.5),
        'b_down': nrm(ks[21], (DEPTH, N_EXPERTS, D_MODEL), 0.01),
    }


def reference(x, c, w_ada, b_ada, norm1_g, w_in, nat_q_g, nat_k_g, nat_rpb, diff_q_g, diff_k_g,
              diff_lambda, diff_sub_g, rel_bias_table, w_out, norm2_g, router_w, router_b,
              w_gate_up, b_gate_up, w_down, b_down):
    B, S, D = x.shape
    for l in range(DEPTH):
        lambda_init = 0.8 - 0.6 * math.exp(-0.3 * l)
        mod = jax.nn.silu(c) @ w_ada[l] + b_ada[l]
        shift1, scale1, gate1, shift2, scale2, gate2 = jnp.split(mod[:, None, :], 6, axis=-1)

        h = rms_norm(x, norm1_g[l]) * (1.0 + scale1) + shift1
        proj = h @ w_in[l]
        nq, nk, nv, dq, dk, dv = jnp.split(
            proj, [NAT_WIDTH, 2 * NAT_WIDTH, 3 * NAT_WIDTH, 3 * NAT_WIDTH + DIFF_WIDTH,
                   3 * NAT_WIDTH + 2 * DIFF_WIDTH], axis=-1)
        nq = rms_norm(nq.reshape(B, S, N_NAT_HEADS, HEAD_DIM), nat_q_g[l])
        nk = rms_norm(nk.reshape(B, S, N_NAT_HEADS, HEAD_DIM), nat_k_g[l])
        nv = nv.reshape(B, S, N_NAT_HEADS, HEAD_DIM)
        nat_out = neighbourhood_attention(nq, nk, nv, nat_rpb[l])

        dq = rms_norm(dq.reshape(B, S, N_DIFF_HEADS, 2, HEAD_DIM), diff_q_g[l])
        dk = rms_norm(dk.reshape(B, S, N_DIFF_HEADS, 2, HEAD_DIM), diff_k_g[l])
        dv = dv.reshape(B, S, N_DIFF_HEADS, 2 * HEAD_DIM)
        lam = diff_lambda[l].astype(jnp.float32)
        lambda_full = (jnp.exp(jnp.sum(lam[0] * lam[1])) - jnp.exp(jnp.sum(lam[2] * lam[3]))
                       + lambda_init)
        diff_out = diff_attention(dq, dk, dv, rel_bias_table, lambda_full, diff_sub_g[l],
                                  lambda_init)

        mix = jnp.concatenate([nat_out, diff_out], axis=-1) @ w_out[l]
        x = x + gate1 * mix

        h2 = rms_norm(x, norm2_g[l]) * (1.0 + scale2) + shift2
        x = x + gate2 * moe(h2, router_w[l], router_b[l], w_gate_up[l], b_gate_up[l],
                            w_down[l], b_down[l])
    return x
```

```python
import functools
import math

import jax
import jax.numpy as jnp
from jax import lax
from jax.experimental import pallas as pl
from jax.experimental.pallas import tpu as pltpu

HEAD_DIM = 128
GRID_W = 64
NAT_KR_MAX = 8
NAT_KC = 16
N_BUCKETS = 32
MAX_DISTANCE = 128
TOP_K = 4
SWIGLU_LIMIT = 7.0
SWIGLU_ALPHA = 1.702
RMS_EPS = 1e-6
NEG_INF = -1e30

LANES = 128
V7X_VMEM_LIMIT_BYTES = 56 * 1024 * 1024

F32 = jnp.float32
BF16 = jnp.bfloat16
I32 = jnp.int32


def _tile(n, pref):
    t = min(n, pref)
    assert n % t == 0, (n, pref)
    return t


def _params(n_axes):
    return pltpu.CompilerParams(dimension_semantics=("arbitrary",) * n_axes,
                                vmem_limit_bytes=V7X_VMEM_LIMIT_BYTES)


def _adaln_kernel(c_ref, w_ref, b_ref, o_ref):
    c = c_ref[...]
    a = (c * jax.nn.sigmoid(c)).astype(BF16)
    o_ref[...] = jnp.dot(a, w_ref[...].astype(BF16), preferred_element_type=F32) + b_ref[...]


def _adaln(c, w, b):
    bsz, d = c.shape
    n = w.shape[1]
    tn = _tile(n, 1024)
    return pl.pallas_call(
        _adaln_kernel,
        grid=(n // tn,),
        in_specs=[pl.BlockSpec((bsz, d), lambda j: (0, 0)),
                  pl.BlockSpec((d, tn), lambda j: (0, j)),
                  pl.BlockSpec((1, tn), lambda j: (0, j))],
        out_specs=pl.BlockSpec((bsz, tn), lambda j: (0, j)),
        out_shape=jax.ShapeDtypeStruct((bsz, n), F32),
        compiler_params=_params(1),
        name="adaln",
    )(c, w, b.reshape(1, n))


def _inproj_kernel(x_ref, mod_ref, g_ref, w_ref, gain_ref, o_ref, h_scr):
    j = pl.program_id(1)

    @pl.when(j == 0)
    def _():
        x = x_ref[...]
        ms = jnp.mean(x * x, axis=-1, keepdims=True)
        y = x * lax.rsqrt(ms + RMS_EPS) * g_ref[...]
        m = mod_ref[...]
        h_scr[...] = (y * (1.0 + m[1:2]) + m[0:1]).astype(BF16)

    y = jnp.dot(h_scr[...], w_ref[...], preferred_element_type=F32)
    heads = o_ref.shape[0]
    is_value = (j == 2) | (j == 5)

    @pl.when(is_value)
    def _():
        for hh in range(heads):
            o_ref[hh] = y[:, hh * HEAD_DIM:(hh + 1) * HEAD_DIM].astype(BF16)

    @pl.when(jnp.logical_not(is_value))
    def _():
        g = gain_ref[pl.ds(j, 1), :]
        for hh in range(heads):
            yh = y[:, hh * HEAD_DIM:(hh + 1) * HEAD_DIM]
            ms = jnp.mean(yh * yh, axis=-1, keepdims=True)
            o_ref[hh] = (yh * lax.rsqrt(ms + RMS_EPS) * g).astype(BF16)


def _inproj(xf, mod3, g1, w_bf16, gains, seq):
    t, d = xf.shape
    seg = d // 2
    ns = seg // HEAD_DIM
    tm = _tile(seq, 1024)
    per_b = seq // tm
    return pl.pallas_call(
        _inproj_kernel,
        grid=(t // tm, 6),
        in_specs=[pl.BlockSpec((tm, d), lambda i, j: (i, 0)),
                  pl.BlockSpec((None, 6, d), lambda i, j: (i // per_b, 0, 0)),
                  pl.BlockSpec((1, d), lambda i, j: (0, 0)),
                  pl.BlockSpec((d, seg), lambda i, j: (0, j)),
                  pl.BlockSpec((8, HEAD_DIM), lambda i, j: (0, 0))],
        out_specs=pl.BlockSpec((ns, tm, HEAD_DIM), lambda i, j: (j, i, 0)),
        out_shape=jax.ShapeDtypeStruct((6 * ns, t, HEAD_DIM), BF16),
        scratch_shapes=[pltpu.VMEM((tm, d), BF16)],
        compiler_params=_params(2),
        name="inproj",
    )(xf, mod3, g1.reshape(1, d), w_bf16, gains)


def _nat_kernel(q_ref, k_ref, v_ref, b_ref, o_ref, *, rows, kr):
    band = kr * GRID_W

    def body(r, carry):
        rs = jnp.clip(r - kr // 2, 0, rows - kr)
        q0 = pl.multiple_of(r * GRID_W, GRID_W)
        k0 = pl.multiple_of(rs * GRID_W, GRID_W)
        q = q_ref[pl.ds(q0, GRID_W), :]
        kb = k_ref[pl.ds(k0, band), :]
        vb = v_ref[pl.ds(k0, band), :]
        s = lax.dot_general(q, kb, (((1,), (1,)), ((), ())), preferred_element_type=F32)
        s = s + b_ref[r - rs]
        m = jnp.max(s, axis=-1, keepdims=True)
        p = jnp.exp(s - m)
        den = jnp.sum(p, axis=-1, keepdims=True)
        o = jnp.dot(p.astype(BF16), vb, preferred_element_type=F32) / den
        o_ref[pl.ds(q0, GRID_W), :] = o.astype(BF16)
        return carry

    lax.fori_loop(0, rows, body, 0)


def _nat_bias_table(rpb, kr):
    j = jnp.arange(kr)
    row_off = j[None, :] - j[:, None] + (NAT_KR_MAX - 1)
    col = jnp.arange(GRID_W)
    col_off = jnp.clip(col[None, :] - col[:, None], -(NAT_KC - 1), NAT_KC - 1) + (NAT_KC - 1)
    col_start = jnp.clip(col - NAT_KC // 2, 0, GRID_W - NAT_KC)
    col_mask = (col[None, :] >= col_start[:, None]) & (col[None, :] < col_start[:, None] + NAT_KC)
    bias = jnp.take(rpb[:, row_off], col_off, axis=-1).astype(F32)
    bias = jnp.where(col_mask[None, None, None], bias, NEG_INF)
    bias = bias.transpose(0, 1, 3, 2, 4)
    return bias.reshape(rpb.shape[0], kr, GRID_W, kr * GRID_W)


def _nat_attention(qkv, bias_tab, bsz, seq, ns):
    t = qkv.shape[1]
    rows = seq // GRID_W
    kr = min(NAT_KR_MAX, rows)
    blk = (None, seq, HEAD_DIM)
    return pl.pallas_call(
        functools.partial(_nat_kernel, rows=rows, kr=kr),
        grid=(ns, bsz),
        in_specs=[pl.BlockSpec(blk, lambda h, b: (h, b, 0)),
                  pl.BlockSpec(blk, lambda h, b: (ns + h, b, 0)),
                  pl.BlockSpec(blk, lambda h, b: (2 * ns + h, b, 0)),
                  pl.BlockSpec((None, kr, GRID_W, kr * GRID_W), lambda h, b: (h, 0, 0, 0))],
        out_specs=pl.BlockSpec(blk, lambda h, b: (h, b, 0)),
        out_shape=jax.ShapeDtypeStruct((ns, t, HEAD_DIM), BF16),
        compiler_params=_params(2),
        name="nat_attn",
    )(qkv, qkv, qkv, bias_tab)


def _t5_bucket(rel):
    nb = N_BUCKETS // 2
    max_exact = nb // 2
    ret = jnp.where(rel > 0, nb, 0)
    n = jnp.abs(rel)
    nf = jnp.maximum(n, 1).astype(F32)
    large = max_exact + (jnp.log(nf / max_exact) / math.log(MAX_DISTANCE / max_exact)
                         * (nb - max_exact)).astype(I32)
    large = jnp.minimum(large, nb - 1)
    return ret + jnp.where(n < max_exact, n, large)


def _t5_bias_strip(rel_table, seq, tq):
    rel = jnp.arange(2 * seq - tq, dtype=I32)[None, :] - jnp.arange(tq, dtype=I32)[:, None] - (seq - tq)
    return rel_table[_t5_bucket(rel)].transpose(2, 0, 1).astype(F32)


def _diff_kernel(q_ref, k_ref, v_ref, e_ref, lam_ref, sg_ref, o_ref, *, seq, tq, nq, lambda_init):
    qi = pl.program_id(2)
    m0 = pl.multiple_of((nq - 1 - qi) * tq, tq)
    bias = e_ref[:, pl.ds(m0, seq)]
    lam = lam_ref[...]
    lam_full = (jnp.exp(jnp.sum(lam[0:1] * lam[1:2], axis=-1, keepdims=True))
                - jnp.exp(jnp.sum(lam[2:3] * lam[3:4], axis=-1, keepdims=True)) + lambda_init)

    def softmax_parts(p):
        s = lax.dot_general(q_ref[p], k_ref[p], (((1,), (1,)), ((), ())), preferred_element_type=F32)
        s = s + bias
        e = jnp.exp(s - jnp.max(s, axis=-1, keepdims=True))
        return e, 1.0 / jnp.sum(e, axis=-1, keepdims=True)

    e1, r1 = softmax_parts(0)
    e2, r2 = softmax_parts(1)
    a = e1 * r1 - e2 * (r2 * lam_full)
    v = jnp.concatenate([v_ref[0], v_ref[1]], axis=-1)
    o = jnp.dot(a.astype(BF16), v, preferred_element_type=F32)
    ms = jnp.mean(o * o, axis=-1, keepdims=True)
    o = o * lax.rsqrt(ms + RMS_EPS) * sg_ref[...] * (1.0 - lambda_init)
    o_ref[0] = o[:, :HEAD_DIM].astype(BF16)
    o_ref[1] = o[:, HEAD_DIM:].astype(BF16)


def _diff_attention(qkv, strip, lam, sub_g, lambda_init, bsz, seq, ns):
    t = qkv.shape[1]
    hd = ns // 2
    tq = strip.shape[1]
    nq = seq // tq
    qb, kb, vb = 3 * ns // 2, 4 * ns // 2, 5 * ns // 2
    return pl.pallas_call(
        functools.partial(_diff_kernel, seq=seq, tq=tq, nq=nq, lambda_init=lambda_init),
        grid=(hd, bsz, nq),
        in_specs=[pl.BlockSpec((2, tq, HEAD_DIM), lambda h, b, i: (qb + h, b * nq + i, 0)),
                  pl.BlockSpec((2, seq, HEAD_DIM), lambda h, b, i: (kb + h, b, 0)),
                  pl.BlockSpec((2, seq, HEAD_DIM), lambda h, b, i: (vb + h, b, 0)),
                  pl.BlockSpec((None, tq, 2 * seq - tq), lambda h, b, i: (h, 0, 0)),
                  pl.BlockSpec((4, HEAD_DIM), lambda h, b, i: (0, 0)),
                  pl.BlockSpec((1, 2 * HEAD_DIM), lambda h, b, i: (0, 0))],
        out_specs=pl.BlockSpec((2, tq, HEAD_DIM), lambda h, b, i: (h, b * nq + i, 0)),
        out_shape=jax.ShapeDtypeStruct((ns, t, HEAD_DIM), BF16),
        compiler_params=_params(3),
        name="diff_attn",
    )(qkv, qkv, qkv, strip, lam, sub_g.reshape(1, 2 * HEAD_DIM))


def _rows_shape(rows, d):
    return (rows * (d // LANES), LANES)


def _split_bf16(v):
    hi = v.astype(BF16)
    lo = (v - hi.astype(F32)).astype(BF16)
    return hi, lo


def _outproj_kernel(nat_ref, dif_ref, x_ref, mod_ref, w_ref, g_ref, rw_ref, rb_ref,
                    x1_ref, h2_ref, lt_ref):
    ns = nat_ref.shape[0]
    a = jnp.concatenate([nat_ref[hh] for hh in range(ns)] + [dif_ref[hh] for hh in range(ns)], axis=-1)
    mix = jnp.dot(a, w_ref[...], preferred_element_type=F32)
    m = mod_ref[...]
    x1 = x_ref[...] + m[2:3] * mix
    x1_ref[...] = x1
    ms = jnp.mean(x1 * x1, axis=-1, keepdims=True)
    h2 = x1 * lax.rsqrt(ms + RMS_EPS) * g_ref[...] * (1.0 + m[4:5]) + m[3:4]
    n_chunks = h2.shape[1] // LANES
    for ch in range(n_chunks):
        h2_ref[pl.ds(ch, h2.shape[0], stride=n_chunks), :] = h2[:, ch * LANES:(ch + 1) * LANES]
    h_hi, h_lo = _split_bf16(h2)
    w_hi, w_lo = _split_bf16(rw_ref[...])
    lg = (jnp.dot(h_hi, w_hi, preferred_element_type=F32)
          + jnp.dot(h_lo, w_hi, preferred_element_type=F32)
          + jnp.dot(h_hi, w_lo, preferred_element_type=F32))
    n_exp = lt_ref.shape[0]
    lt_ref[...] = lg.T[:n_exp] + rb_ref[...]


def _outproj(nat, dif, xf, mod3, w_bf16, g2, rw_pad, rb, seq):
    t, d = xf.shape
    ns = nat.shape[0]
    n_exp = rb.shape[0]
    tm = _tile(seq, 512)
    per_b = seq // tm
    return pl.pallas_call(
        _outproj_kernel,
        grid=(t // tm,),
        in_specs=[pl.BlockSpec((ns, tm, HEAD_DIM), lambda i: (0, i, 0)),
                  pl.BlockSpec((ns, tm, HEAD_DIM), lambda i: (0, i, 0)),
                  pl.BlockSpec((tm, d), lambda i: (i, 0)),
                  pl.BlockSpec((None, 6, d), lambda i: (i // per_b, 0, 0)),
                  pl.BlockSpec((d, d), lambda i: (0, 0)),
                  pl.BlockSpec((1, d), lambda i: (0, 0)),
                  pl.BlockSpec((d, LANES), lambda i: (0, 0)),
                  pl.BlockSpec((n_exp, 1), lambda i: (0, 0))],
        out_specs=[pl.BlockSpec((tm, d), lambda i: (i, 0)),
                   pl.BlockSpec((tm * (d // LANES), LANES), lambda i: (i, 0)),
                   pl.BlockSpec((n_exp, tm), lambda i: (0, i))],
        out_shape=[jax.ShapeDtypeStruct((t, d), F32),
                   jax.ShapeDtypeStruct(_rows_shape(t, d), F32),
                   jax.ShapeDtypeStruct((n_exp, t), F32)],
        compiler_params=_params(1),
        name="outproj",
    )(nat, dif, xf, mod3, w_bf16, g2.reshape(1, d), rw_pad, rb.reshape(n_exp, 1))


def _route_kernel(lt_ref, idx_ref, gate_ref, pos_ref, tmeta_ref, emeta_ref, rank_scr, *, tm, tb):
    n_exp, t = lt_ref.shape
    ntp = tmeta_ref.shape[1]
    eidx = lax.broadcasted_iota(I32, (n_exp, tb), 0)
    tri = (lax.broadcasted_iota(I32, (tb, tb), 0) < lax.broadcasted_iota(I32, (tb, tb), 1)).astype(BF16)

    def pass1(jb, counts):
        off = pl.multiple_of(jb * tb, tb)
        l = lt_ref[:, pl.ds(off, tb)]
        vals, sels, hots = [], [], []
        for _ in range(TOP_K):
            m = jnp.max(l, axis=0, keepdims=True)
            sel = jnp.min(jnp.where(l == m, eidx, n_exp), axis=0, keepdims=True)
            hot = eidx == sel
            vals.append(m)
            sels.append(sel)
            hots.append(hot)
            l = jnp.where(hot, -jnp.inf, l)
        exps = [jnp.exp(v - vals[0]) for v in vals]
        den = exps[0]
        for e in exps[1:]:
            den = den + e
        member = hots[0].astype(F32)
        for hot in hots[1:]:
            member = member + hot.astype(F32)
        before = jnp.dot(member.astype(BF16), tri, preferred_element_type=F32) + counts
        for k in range(TOP_K):
            rank = jnp.sum(jnp.where(hots[k], before, 0.0), axis=0, keepdims=True)
            idx_ref[pl.ds(k, 1), pl.ds(off, tb)] = sels[k]
            gate_ref[pl.ds(k, 1), pl.ds(off, tb)] = exps[k] / den
            rank_scr[pl.ds(k, 1), pl.ds(off, tb)] = rank.astype(I32)
        return counts + jnp.sum(member, axis=1, keepdims=True)

    counts = lax.fori_loop(0, t // tb, pass1, jnp.zeros((n_exp, 1), F32))

    padded = jnp.ceil(counts * (1.0 / tm)) * tm
    er = lax.broadcasted_iota(I32, (n_exp, LANES), 0)
    ec = lax.broadcasted_iota(I32, (n_exp, LANES), 1)

    def to_lanes(col):
        return jnp.sum(jnp.where(er == ec, col, 0.0), axis=0, keepdims=True)

    start = jnp.sum(jnp.where(ec < er, to_lanes(padded), 0.0), axis=1, keepdims=True)
    end = start + padded

    def pass2(jb, carry):
        off = pl.multiple_of(jb * tb, tb)
        for k in range(TOP_K):
            sel = idx_ref[pl.ds(k, 1), pl.ds(off, tb)]
            st = jnp.sum(jnp.where(eidx == sel, start, 0.0), axis=0, keepdims=True)
            pos_ref[pl.ds(k, 1), pl.ds(off, tb)] = st.astype(I32) + rank_scr[pl.ds(k, 1), pl.ds(off, tb)]
        return carry

    lax.fori_loop(0, t // tb, pass2, 0)

    tile_row = (lax.broadcasted_iota(I32, (n_exp, ntp), 1) * tm).astype(F32)
    tile_exp = jnp.sum((end <= tile_row).astype(F32), axis=0, keepdims=True)
    tile_exp = jnp.minimum(tile_exp, n_exp - 1.0)
    n_used = jnp.sum(padded, axis=0, keepdims=True) * (1.0 / tm)
    tmeta_ref[...] = jnp.zeros(tmeta_ref.shape, I32)
    tmeta_ref[0:1, :] = tile_exp.astype(I32)
    tmeta_ref[1:2, :] = jnp.broadcast_to(n_used, (1, ntp)).astype(I32)
    emeta_ref[...] = jnp.zeros(emeta_ref.shape, I32)
    emeta_ref[0:1, :] = to_lanes(start).astype(I32)
    emeta_ref[1:2, :] = to_lanes(counts).astype(I32)


def _route(logits_t, tm, n_tiles):
    n_exp, t = logits_t.shape
    tb = _tile(t, 512)
    ntp = pl.cdiv(n_tiles, LANES) * LANES
    full = lambda shape: pl.BlockSpec(shape, lambda: (0,) * len(shape))
    return pl.pallas_call(
        functools.partial(_route_kernel, tm=tm, tb=tb),
        in_specs=[full((n_exp, t))],
        out_specs=[full((TOP_K, t)), full((TOP_K, t)), full((TOP_K, t)), full((8, ntp)), full((8, LANES))],
        out_shape=[jax.ShapeDtypeStruct((TOP_K, t), I32),
                   jax.ShapeDtypeStruct((TOP_K, t), F32),
                   jax.ShapeDtypeStruct((TOP_K, t), I32),
                   jax.ShapeDtypeStruct((8, ntp), I32),
                   jax.ShapeDtypeStruct((8, LANES), I32)],
        scratch_shapes=[pltpu.VMEM((TOP_K, t), I32)],
        compiler_params=pltpu.CompilerParams(vmem_limit_bytes=V7X_VMEM_LIMIT_BYTES),
        name="route",
    )(logits_t)


def _dispatch_kernel(estart_ref, ecnt_ref, nused_ref, pos_hbm, h_ref, xs_hbm, pos_smem, zero_scr,
                     sem_idx, sem_row, sem_zero, *, tm, nch):
    i = pl.program_id(0)
    tmd = h_ref.shape[0] // nch
    n_exp = estart_ref.shape[0]
    n_tiles = xs_hbm.shape[0] // (tm * nch)

    idx_copy = pltpu.make_async_copy(pos_hbm.at[i], pos_smem, sem_idx)
    idx_copy.start()

    @pl.when(i == 0)
    def _():
        zero_scr[...] = jnp.zeros(zero_scr.shape, F32)

        def fill(e, carry):
            cnt = ecnt_ref[e]
            pad = (tm - cnt % tm) % tm
            base = estart_ref[e] + cnt
            size = tm // 2
            while size >= 1:
                off = pad & ~(2 * size - 1)

                @pl.when((pad & size) != 0)
                def _(size=size, off=off):
                    dst0 = pl.multiple_of((base + off) * nch, nch)
                    cp = pltpu.make_async_copy(zero_scr.at[pl.ds(0, size * nch)],
                                               xs_hbm.at[pl.ds(dst0, size * nch)], sem_zero)
                    cp.start()
                    cp.wait()
                size //= 2
            return carry

        lax.fori_loop(0, n_exp, fill, 0)

        def tail_copy(j, part):
            dst0 = pl.multiple_of((j * tm + part * (tm // 2)) * nch, nch)
            return pltpu.make_async_copy(zero_scr, xs_hbm.at[pl.ds(dst0, tm // 2 * nch)], sem_zero)

        def tail_start(j, carry):
            tail_copy(j, 0).start()
            tail_copy(j, 1).start()
            return carry

        def tail_wait(j, carry):
            tail_copy(j, 0).wait()
            tail_copy(j, 1).wait()
            return carry

        lax.fori_loop(nused_ref[0], n_tiles, tail_start, 0)
        lax.fori_loop(nused_ref[0], n_tiles, tail_wait, 0)

    idx_copy.wait()

    def issue(tok, carry):
        src0 = pl.multiple_of(tok * nch, nch)
        for k in range(TOP_K):
            dst0 = pl.multiple_of(pos_smem[k * tmd + tok] * nch, nch)
            pltpu.make_async_copy(h_ref.at[pl.ds(src0, nch)], xs_hbm.at[pl.ds(dst0, nch)], sem_row).start()
        return carry

    lax.fori_loop(0, tmd, issue, 0)
    for _ in range(TOP_K):
        pltpu.make_async_copy(h_ref, xs_hbm.at[pl.ds(0, tmd * nch)], sem_row).wait()


def _dispatch(h2, pos_tiles, estart, ecnt, n_used, tm, n_tiles, nch):
    ntt, per_tile = pos_tiles.shape
    tmd = per_tile // TOP_K
    grid_spec = pltpu.PrefetchScalarGridSpec(
        num_scalar_prefetch=3,
        grid=(ntt,),
        in_specs=[pl.BlockSpec(memory_space=pl.ANY),
                  pl.BlockSpec((tmd * nch, LANES), lambda i, es, ec, nu: (i, 0))],
        out_specs=pl.BlockSpec(memory_space=pl.ANY),
        scratch_shapes=[pltpu.SMEM((per_tile,), I32),
                        pltpu.VMEM((tm // 2 * nch, LANES), F32),
                        pltpu.SemaphoreType.DMA, pltpu.SemaphoreType.DMA, pltpu.SemaphoreType.DMA],
    )
    return pl.pallas_call(
        functools.partial(_dispatch_kernel, tm=tm, nch=nch),
        grid_spec=grid_spec,
        out_shape=jax.ShapeDtypeStruct((n_tiles * tm * nch, LANES), F32),
        compiler_params=_params(1),
        name="dispatch",
    )(estart, ecnt, n_used, pos_tiles, h2)


def _gate_up_kernel(texp_ref, nused_ref, x_ref, w_ref, b_ref, act_ref, *, chunk, nch):
    i = pl.program_id(0)

    @pl.when(i < nused_ref[0])
    def _():
        tm, ff = act_ref.shape
        x = jnp.concatenate([x_ref[pl.ds(ch, tm, stride=nch), :] for ch in range(nch)], axis=-1).astype(BF16)
        for c0 in range(0, ff, chunk):
            gate = jnp.dot(x, w_ref[:, c0:c0 + chunk], preferred_element_type=F32) + b_ref[:, c0:c0 + chunk]
            up = (jnp.dot(x, w_ref[:, ff + c0:ff + c0 + chunk], preferred_element_type=F32)
                  + b_ref[:, ff + c0:ff + c0 + chunk])
            gate = jnp.minimum(gate, SWIGLU_LIMIT)
            up = jnp.clip(up, -SWIGLU_LIMIT, SWIGLU_LIMIT)
            act = (up + 1.0) * gate * jax.nn.sigmoid(gate * SWIGLU_ALPHA)
            act_ref[:, c0:c0 + chunk] = act.astype(BF16)

    @pl.when(i >= nused_ref[0])
    def _():
        act_ref[...] = jnp.zeros(act_ref.shape, BF16)


def _gate_up(xs, w_bf16, b, tile_exp, n_used, tm):
    n_exp, d, ff2 = w_bf16.shape
    nch = d // LANES
    rows = xs.shape[0] // nch
    ff = ff2 // 2
    n_tiles = rows // tm
    live = lambda i, te, nu: jnp.minimum(i, nu[0] - 1)
    grid_spec = pltpu.PrefetchScalarGridSpec(
        num_scalar_prefetch=2,
        grid=(n_tiles,),
        in_specs=[pl.BlockSpec((tm * nch, LANES), lambda i, te, nu: (live(i, te, nu), 0)),
                  pl.BlockSpec((None, d, ff2), lambda i, te, nu: (te[i], 0, 0)),
                  pl.BlockSpec((None, 1, ff2), lambda i, te, nu: (te[i], 0, 0))],
        out_specs=pl.BlockSpec((tm, ff), lambda i, te, nu: (i, 0)),
    )
    return pl.pallas_call(
        functools.partial(_gate_up_kernel, chunk=_tile(ff, 512), nch=nch),
        grid_spec=grid_spec,
        out_shape=jax.ShapeDtypeStruct((rows, ff), BF16),
        compiler_params=_params(1),
        name="gate_up",
    )(tile_exp, n_used, xs, w_bf16, b.reshape(n_exp, 1, ff2))


def _down_kernel(texp_ref, nused_ref, a_ref, w_ref, b_ref, y_ref):
    i = pl.program_id(0)

    @pl.when(i < nused_ref[0])
    def _():
        y = jnp.dot(a_ref[...], w_ref[...], preferred_element_type=F32) + b_ref[...]
        tm, d = y.shape
        nch = d // LANES
        for ch in range(nch):
            y_ref[pl.ds(ch, tm, stride=nch), :] = y[:, ch * LANES:(ch + 1) * LANES]

    @pl.when(i >= nused_ref[0])
    def _():
        y_ref[...] = jnp.zeros(y_ref.shape, F32)


def _down(act, w_bf16, b, tile_exp, n_used, tm):
    rows, ff = act.shape
    n_exp, _, d = w_bf16.shape
    nch = d // LANES
    n_tiles = rows // tm
    live = lambda i, te, nu: jnp.minimum(i, nu[0] - 1)
    grid_spec = pltpu.PrefetchScalarGridSpec(
        num_scalar_prefetch=2,
        grid=(n_tiles,),
        in_specs=[pl.BlockSpec((tm, ff), lambda i, te, nu: (live(i, te, nu), 0)),
                  pl.BlockSpec((None, ff, d), lambda i, te, nu: (te[i], 0, 0)),
                  pl.BlockSpec((None, 1, d), lambda i, te, nu: (te[i], 0, 0))],
        out_specs=pl.BlockSpec((tm * nch, LANES), lambda i, te, nu: (i, 0)),
    )
    return pl.pallas_call(
        _down_kernel,
        grid_spec=grid_spec,
        out_shape=jax.ShapeDtypeStruct(_rows_shape(rows, d), F32),
        compiler_params=_params(1),
        name="down",
    )(tile_exp, n_used, act, w_bf16, b.reshape(n_exp, 1, d))


def _combine_kernel(pos_hbm, y_hbm, x1_ref, mod_ref, g_ref, o_ref, pos_smem, ybuf, sem_idx, sem_row):
    i = pl.program_id(0)
    tmc, d = x1_ref.shape

    idx_copy = pltpu.make_async_copy(pos_hbm.at[i], pos_smem, sem_idx)
    idx_copy.start()
    idx_copy.wait()

    nch = d // LANES

    def issue(tok, carry):
        dst0 = pl.multiple_of(tok * nch, nch)
        for k in range(TOP_K):
            src0 = pl.multiple_of(pos_smem[k * tmc + tok] * nch, nch)
            pltpu.make_async_copy(y_hbm.at[pl.ds(src0, nch)], ybuf.at[k, pl.ds(dst0, nch)], sem_row).start()
        return carry

    lax.fori_loop(0, tmc, issue, 0)
    for k in range(TOP_K):
        pltpu.make_async_copy(y_hbm.at[pl.ds(0, tmc * nch)], ybuf.at[k], sem_row).wait()

    g = g_ref[...]
    gate2 = mod_ref[...][5:6]
    for ch in range(nch):
        cols = slice(ch * LANES, (ch + 1) * LANES)
        moe = ybuf[0, pl.ds(ch, tmc, stride=nch), :] * g[:, 0:1]
        for k in range(1, TOP_K):
            moe = moe + ybuf[k, pl.ds(ch, tmc, stride=nch), :] * g[:, k:k + 1]
        o_ref[:, cols] = x1_ref[:, cols] + gate2[:, cols] * moe


def _combine(y, pos_tiles, x1, mod3, gates_t, seq):
    t, d = x1.shape
    ntt, per_tile = pos_tiles.shape
    tmc = per_tile // TOP_K
    per_b = seq // tmc
    return pl.pallas_call(
        _combine_kernel,
        grid=(ntt,),
        in_specs=[pl.BlockSpec(memory_space=pl.ANY),
                  pl.BlockSpec(memory_space=pl.ANY),
                  pl.BlockSpec((tmc, d), lambda i: (i, 0)),
                  pl.BlockSpec((None, 6, d), lambda i: (i // per_b, 0, 0)),
                  pl.BlockSpec((tmc, TOP_K), lambda i: (i, 0))],
        out_specs=pl.BlockSpec((tmc, d), lambda i: (i, 0)),
        out_shape=jax.ShapeDtypeStruct((t, d), F32),
        scratch_shapes=[pltpu.SMEM((per_tile,), I32),
                        pltpu.VMEM((TOP_K, tmc * (d // LANES), LANES), F32),
                        pltpu.SemaphoreType.DMA, pltpu.SemaphoreType.DMA],
        compiler_params=_params(1),
        name="combine",
    )(pos_tiles, y, x1, mod3, gates_t)


def _moe(h2, logits_t, x1, mod3, w_gu, b_gu, w_dn, b_dn, seq):
    t, d = x1.shape
    n_exp = logits_t.shape[0]
    tm = _tile(t * TOP_K, 256)
    n_tiles = (t * TOP_K) // tm + n_exp
    idx, gates, pos, tmeta, emeta = _route(logits_t, tm, n_tiles)
    del idx
    tok_tile = _tile(seq, 256)
    pos_tiles = (pos.reshape(TOP_K, t // tok_tile, tok_tile).transpose(1, 0, 2)
                 .reshape(t // tok_tile, TOP_K * tok_tile))
    tile_exp = tmeta[0, :n_tiles]
    n_used = tmeta[1, :1]
    xs = _dispatch(h2, pos_tiles, emeta[0, :n_exp], emeta[1, :n_exp], n_used, tm, n_tiles, d // LANES)
    act = _gate_up(xs, w_gu.astype(BF16), b_gu, tile_exp, n_used, tm)
    y = _down(act, w_dn.astype(BF16), b_dn, tile_exp, n_used, tm)
    return _combine(y, pos_tiles, x1, mod3, gates.T, seq)


def kernel(x, c, w_ada, b_ada, norm1_g, w_in, nat_q_g, nat_k_g, nat_rpb, diff_q_g, diff_k_g, diff_lambda,
           diff_sub_g, rel_bias_table, w_out, norm2_g, router_w, router_b, w_gate_up, b_gate_up, w_down, b_down):
    bsz, seq, d = x.shape
    t = bsz * seq
    ns = (d // 2) // HEAD_DIM
    n_exp = router_w.shape[-1]
    scale = HEAD_DIM ** -0.5
    rows = seq // GRID_W
    kr = min(NAT_KR_MAX, rows)
    strip = _t5_bias_strip(rel_bias_table, seq, _tile(seq, 256))
    ones = jnp.ones((HEAD_DIM,), F32)
    xf = x.reshape(t, d)
    for l in range(w_ada.shape[0]):
        lambda_init = 0.8 - 0.6 * math.exp(-0.3 * l)
        mod3 = _adaln(c, w_ada[l], b_ada[l]).reshape(bsz, 6, d)
        gains = jnp.stack([nat_q_g[l] * scale, nat_k_g[l], ones, diff_q_g[l] * scale, diff_k_g[l], ones, ones, ones])
        qkv = _inproj(xf, mod3, norm1_g[l], w_in[l].astype(BF16), gains, seq)
        nat = _nat_attention(qkv, _nat_bias_table(nat_rpb[l], kr), bsz, seq, ns)
        dif = _diff_attention(qkv, strip, diff_lambda[l], diff_sub_g[l], lambda_init, bsz, seq, ns)
        rw_pad = jnp.pad(router_w[l], ((0, 0), (0, LANES - n_exp)))
        x1, h2, logits_t = _outproj(nat, dif, xf, mod3, w_out[l].astype(BF16), norm2_g[l], rw_pad, router_b[l], seq)
        xf = _moe(h2, logits_t, x1, mod3, w_gate_up[l], b_gate_up[l], w_down[l], b_down[l], seq)
    return xf.reshape(bsz, seq, d)
```

```python
import functools
import math

import jax
import jax.numpy as jnp
from jax import lax
from jax.experimental import pallas as pl
from jax.experimental.pallas import tpu as pltpu

HEAD_DIM = 128
GRID_W = 64
NAT_KR_MAX = 8
NAT_KC = 16
N_BUCKETS = 32
MAX_DISTANCE = 128
TOP_K = 4
SWIGLU_LIMIT = 7.0
SWIGLU_ALPHA = 1.702
RMS_EPS = 1e-6
NEG_INF = -1e30

LANES = 128
V7X_VMEM_LIMIT_BYTES = 56 * 1024 * 1024

F32 = jnp.float32
BF16 = jnp.bfloat16
I32 = jnp.int32


def _tile(n, pref):
    t = min(n, pref)
    assert n % t == 0, (n, pref)
    return t


def _params(n_axes):
    return pltpu.CompilerParams(dimension_semantics=("arbitrary",) * n_axes,
                                vmem_limit_bytes=V7X_VMEM_LIMIT_BYTES)


def _adaln_kernel(c_ref, w_ref, b_ref, o_ref):
    c = c_ref[...]
    a = (c * jax.nn.sigmoid(c)).astype(BF16)
    o_ref[...] = jnp.dot(a, w_ref[...].astype(BF16), preferred_element_type=F32) + b_ref[...]


def _adaln(c, w, b):
    bsz, d = c.shape
    n = w.shape[1]
    tn = _tile(n, 1024)
    return pl.pallas_call(
        _adaln_kernel,
        grid=(n // tn,),
        in_specs=[pl.BlockSpec((bsz, d), lambda j: (0, 0)),
                  pl.BlockSpec((d, tn), lambda j: (0, j)),
                  pl.BlockSpec((1, tn), lambda j: (0, j))],
        out_specs=pl.BlockSpec((bsz, tn), lambda j: (0, j)),
        out_shape=jax.ShapeDtypeStruct((bsz, n), F32),
        compiler_params=_params(1),
        name="adaln",
    )(c, w, b.reshape(1, n))


def _inproj_kernel(x_ref, mod_ref, g_ref, w_ref, gain_ref, o_ref, h_scr):
    j = pl.program_id(1)

    @pl.when(j == 0)
    def _():
        x = x_ref[...]
        ms = jnp.mean(x * x, axis=-1, keepdims=True)
        y = x * lax.rsqrt(ms + RMS_EPS) * g_ref[...]
        m = mod_ref[...]
        h_scr[...] = (y * (1.0 + m[1:2]) + m[0:1]).astype(BF16)

    y = jnp.dot(h_scr[...], w_ref[...], preferred_element_type=F32)
    heads = o_ref.shape[0]
    is_value = (j == 2) | (j == 5)

    @pl.when(is_value)
    def _():
        for hh in range(heads):
            o_ref[hh] = y[:, hh * HEAD_DIM:(hh + 1) * HEAD_DIM].astype(BF16)

    @pl.when(jnp.logical_not(is_value))
    def _():
        g = gain_ref[pl.ds(j, 1), :]
        for hh in range(heads):
            yh = y[:, hh * HEAD_DIM:(hh + 1) * HEAD_DIM]
            ms = jnp.mean(yh * yh, axis=-1, keepdims=True)
            o_ref[hh] = (yh * lax.rsqrt(ms + RMS_EPS) * g).astype(BF16)


def _inproj(xf, mod3, g1, w_bf16, gains, seq):
    t, d = xf.shape
    seg = d // 2
    ns = seg // HEAD_DIM
    tm = _tile(seq, 1024)
    per_b = seq // tm
    return pl.pallas_call(
        _inproj_kernel,
        grid=(t // tm, 6),
        in_specs=[pl.BlockSpec((tm, d), lambda i, j: (i, 0)),
                  pl.BlockSpec((None, 6, d), lambda i, j: (i // per_b, 0, 0)),
                  pl.BlockSpec((1, d), lambda i, j: (0, 0)),
                  pl.BlockSpec((d, seg), lambda i, j: (0, j)),
                  pl.BlockSpec((8, HEAD_DIM), lambda i, j: (0, 0))],
        out_specs=pl.BlockSpec((ns, tm, HEAD_DIM), lambda i, j: (j, i, 0)),
        out_shape=jax.ShapeDtypeStruct((6 * ns, t, HEAD_DIM), BF16),
        scratch_shapes=[pltpu.VMEM((tm, d), BF16)],
        compiler_params=_params(2),
        name="inproj",
    )(xf, mod3, g1.reshape(1, d), w_bf16, gains)


def _nat_kernel(q_ref, k_ref, v_ref, b_ref, o_ref, *, rows, kr, group):
    band = kr * GRID_W

    def body(g, carry):
        k0s, scores = [], []
        for i in range(group):
            r = g * group + i
            rs = jnp.clip(r - kr // 2, 0, rows - kr)
            k0 = pl.multiple_of(rs * GRID_W, GRID_W)
            q = q_ref[pl.ds(pl.multiple_of(r * GRID_W, GRID_W), GRID_W), :]
            s = lax.dot_general(q, k_ref[pl.ds(k0, band), :], (((1,), (1,)), ((), ())),
                                preferred_element_type=F32)
            k0s.append(k0)
            scores.append(s + b_ref[r - rs])
        s = jnp.concatenate(scores, axis=0)
        p = jnp.exp(s - jnp.max(s, axis=-1, keepdims=True))
        den = jnp.sum(p, axis=-1, keepdims=True)
        pb = p.astype(BF16)
        outs = [jnp.dot(pb[i * GRID_W:(i + 1) * GRID_W], v_ref[pl.ds(k0s[i], band), :],
                        preferred_element_type=F32) for i in range(group)]
        o = jnp.concatenate(outs, axis=0) / den
        o_ref[pl.ds(pl.multiple_of(g * (group * GRID_W), group * GRID_W), group * GRID_W), :] = o.astype(BF16)
        return carry

    lax.fori_loop(0, rows // group, body, 0)


def _nat_bias_table(rpb, kr):
    j = jnp.arange(kr)
    row_off = j[None, :] - j[:, None] + (NAT_KR_MAX - 1)
    col = jnp.arange(GRID_W)
    col_start = jnp.clip(col - NAT_KC // 2, 0, GRID_W - NAT_KC)
    col_mask = (col[None, :] >= col_start[:, None]) & (col[None, :] < col_start[:, None] + NAT_KC)
    edge = GRID_W - NAT_KC
    u = jnp.pad(rpb[:, row_off].astype(F32), ((0, 0), (0, 0), (0, 0), (edge, edge + 1)), mode="edge")
    bias = _toeplitz(u, GRID_W)[..., GRID_W - 1:2 * GRID_W - 1]
    bias = jnp.where(col_mask[None, None, None], bias, NEG_INF)
    bias = bias.transpose(0, 1, 3, 2, 4)
    return bias.reshape(rpb.shape[0], kr, GRID_W, kr * GRID_W)


def _nat_attention(qkv, bias_tab, bsz, seq, ns):
    t = qkv.shape[1]
    rows = seq // GRID_W
    kr = min(NAT_KR_MAX, rows)
    blk = (None, seq, HEAD_DIM)
    return pl.pallas_call(
        functools.partial(_nat_kernel, rows=rows, kr=kr, group=_tile(rows, 8)),
        grid=(ns, bsz),
        in_specs=[pl.BlockSpec(blk, lambda h, b: (h, b, 0)),
                  pl.BlockSpec(blk, lambda h, b: (ns + h, b, 0)),
                  pl.BlockSpec(blk, lambda h, b: (2 * ns + h, b, 0)),
                  pl.BlockSpec((None, kr, GRID_W, kr * GRID_W), lambda h, b: (h, 0, 0, 0))],
        out_specs=pl.BlockSpec(blk, lambda h, b: (h, b, 0)),
        out_shape=jax.ShapeDtypeStruct((ns, t, HEAD_DIM), BF16),
        compiler_params=_params(2),
        name="nat_attn",
    )(qkv, qkv, qkv, bias_tab)


def _t5_bucket(rel):
    nb = N_BUCKETS // 2
    max_exact = nb // 2
    ret = jnp.where(rel > 0, nb, 0)
    n = jnp.abs(rel)
    nf = jnp.maximum(n, 1).astype(F32)
    large = max_exact + (jnp.log(nf / max_exact) / math.log(MAX_DISTANCE / max_exact)
                         * (nb - max_exact)).astype(I32)
    large = jnp.minimum(large, nb - 1)
    return ret + jnp.where(n < max_exact, n, large)


def _t5_bias_strip(rel_table, seq, tq):
    n = 2 * seq
    rel = jnp.arange(n, dtype=I32) - (seq - 1)
    u = rel_table[_t5_bucket(rel)].T.astype(F32)
    return _toeplitz(u, tq)[:, :, tq - 1:2 * seq - 1]


def _toeplitz(u, rows):
    n = u.shape[-1]
    tiled = jnp.tile(u, (1,) * (u.ndim - 1) + (rows,))[..., :rows * (n - 1)]
    return tiled.reshape(u.shape[:-1] + (rows, n - 1))


def _diff_kernel(q_ref, k_ref, v_ref, e_ref, lam_ref, sg_ref, o_ref, *, seq, tq, nq, lambda_init):
    qi = pl.program_id(2)
    m0 = pl.multiple_of((nq - 1 - qi) * tq, tq)
    bias = e_ref[:, pl.ds(m0, seq)]
    lam = lam_ref[...]
    lam_full = (jnp.exp(jnp.sum(lam[0:1] * lam[1:2], axis=-1, keepdims=True))
                - jnp.exp(jnp.sum(lam[2:3] * lam[3:4], axis=-1, keepdims=True)) + lambda_init)

    def softmax_parts(p):
        s = lax.dot_general(q_ref[p], k_ref[p], (((1,), (1,)), ((), ())), preferred_element_type=F32)
        s = s + bias
        e = jnp.exp(s - jnp.max(s, axis=-1, keepdims=True))
        return e, 1.0 / jnp.sum(e, axis=-1, keepdims=True)

    e1, r1 = softmax_parts(0)
    e2, r2 = softmax_parts(1)
    a = e1 * r1 - e2 * (r2 * lam_full)
    v = jnp.concatenate([v_ref[0], v_ref[1]], axis=-1)
    o = jnp.dot(a.astype(BF16), v, preferred_element_type=F32)
    ms = jnp.mean(o * o, axis=-1, keepdims=True)
    o = o * lax.rsqrt(ms + RMS_EPS) * sg_ref[...] * (1.0 - lambda_init)
    o_ref[0] = o[:, :HEAD_DIM].astype(BF16)
    o_ref[1] = o[:, HEAD_DIM:].astype(BF16)


def _diff_attention(qkv, strip, lam, sub_g, lambda_init, bsz, seq, ns):
    t = qkv.shape[1]
    hd = ns // 2
    tq = strip.shape[1]
    nq = seq // tq
    qb, kb, vb = 3 * ns // 2, 4 * ns // 2, 5 * ns // 2
    return pl.pallas_call(
        functools.partial(_diff_kernel, seq=seq, tq=tq, nq=nq, lambda_init=lambda_init),
        grid=(hd, bsz, nq),
        in_specs=[pl.BlockSpec((2, tq, HEAD_DIM), lambda h, b, i: (qb + h, b * nq + i, 0)),
                  pl.BlockSpec((2, seq, HEAD_DIM), lambda h, b, i: (kb + h, b, 0)),
                  pl.BlockSpec((2, seq, HEAD_DIM), lambda h, b, i: (vb + h, b, 0)),
                  pl.BlockSpec((None, tq, 2 * seq - tq), lambda h, b, i: (h, 0, 0)),
                  pl.BlockSpec((4, HEAD_DIM), lambda h, b, i: (0, 0)),
                  pl.BlockSpec((1, 2 * HEAD_DIM), lambda h, b, i: (0, 0))],
        out_specs=pl.BlockSpec((2, tq, HEAD_DIM), lambda h, b, i: (h, b * nq + i, 0)),
        out_shape=jax.ShapeDtypeStruct((ns, t, HEAD_DIM), BF16),
        compiler_params=_params(3),
        name="diff_attn",
    )(qkv, qkv, qkv, strip, lam, sub_g.reshape(1, 2 * HEAD_DIM))


def _rows_shape(rows, d):
    return (rows * (d // LANES), LANES)


def _split_bf16(v):
    hi = v.astype(BF16)
    lo = (v - hi.astype(F32)).astype(BF16)
    return hi, lo


def _outproj_kernel(nat_ref, dif_ref, x_ref, mod_ref, w_ref, g_ref, rw_ref, rb_ref,
                    x1_ref, h2_ref, lt_ref):
    ns = nat_ref.shape[0]
    a = jnp.concatenate([nat_ref[hh] for hh in range(ns)] + [dif_ref[hh] for hh in range(ns)], axis=-1)
    mix = jnp.dot(a, w_ref[...], preferred_element_type=F32)
    m = mod_ref[...]
    x1 = x_ref[...] + m[2:3] * mix
    x1_ref[...] = x1
    ms = jnp.mean(x1 * x1, axis=-1, keepdims=True)
    h2 = x1 * lax.rsqrt(ms + RMS_EPS) * g_ref[...] * (1.0 + m[4:5]) + m[3:4]
    n_chunks = h2.shape[1] // LANES
    for ch in range(n_chunks):
        h2_ref[pl.ds(ch, h2.shape[0], stride=n_chunks), :] = h2[:, ch * LANES:(ch + 1) * LANES]
    h_hi, h_lo = _split_bf16(h2)
    w_hi, w_lo = _split_bf16(rw_ref[...])
    lg = (jnp.dot(h_hi, w_hi, preferred_element_type=F32)
          + jnp.dot(h_lo, w_hi, preferred_element_type=F32)
          + jnp.dot(h_hi, w_lo, preferred_element_type=F32))
    n_exp = lt_ref.shape[0]
    lt_ref[...] = lg.T[:n_exp] + rb_ref[...]


def _outproj(nat, dif, xf, mod3, w_bf16, g2, rw_pad, rb, seq):
    t, d = xf.shape
    ns = nat.shape[0]
    n_exp = rb.shape[0]
    tm = _tile(seq, 512)
    per_b = seq // tm
    return pl.pallas_call(
        _outproj_kernel,
        grid=(t // tm,),
        in_specs=[pl.BlockSpec((ns, tm, HEAD_DIM), lambda i: (0, i, 0)),
                  pl.BlockSpec((ns, tm, HEAD_DIM), lambda i: (0, i, 0)),
                  pl.BlockSpec((tm, d), lambda i: (i, 0)),
                  pl.BlockSpec((None, 6, d), lambda i: (i // per_b, 0, 0)),
                  pl.BlockSpec((d, d), lambda i: (0, 0)),
                  pl.BlockSpec((1, d), lambda i: (0, 0)),
                  pl.BlockSpec((d, LANES), lambda i: (0, 0)),
                  pl.BlockSpec((n_exp, 1), lambda i: (0, 0))],
        out_specs=[pl.BlockSpec((tm, d), lambda i: (i, 0)),
                   pl.BlockSpec((tm * (d // LANES), LANES), lambda i: (i, 0)),
                   pl.BlockSpec((n_exp, tm), lambda i: (0, i))],
        out_shape=[jax.ShapeDtypeStruct((t, d), F32),
                   jax.ShapeDtypeStruct(_rows_shape(t, d), F32),
                   jax.ShapeDtypeStruct((n_exp, t), F32)],
        compiler_params=_params(1),
        name="outproj",
    )(nat, dif, xf, mod3, w_bf16, g2.reshape(1, d), rw_pad, rb.reshape(n_exp, 1))


def _route_kernel(lt_ref, idx_ref, gate_ref, pos_ref, tmeta_ref, emeta_ref, rank_scr, *, tm, tb):
    n_exp, t = lt_ref.shape
    ntp = tmeta_ref.shape[1]
    eidx = lax.broadcasted_iota(I32, (n_exp, tb), 0)
    tri = (lax.broadcasted_iota(I32, (tb, tb), 0) < lax.broadcasted_iota(I32, (tb, tb), 1)).astype(BF16)

    def pass1(jb, counts):
        off = pl.multiple_of(jb * tb, tb)
        l = lt_ref[:, pl.ds(off, tb)]
        vals, sels, hots = [], [], []
        for _ in range(TOP_K):
            m = jnp.max(l, axis=0, keepdims=True)
            sel = jnp.min(jnp.where(l == m, eidx, n_exp), axis=0, keepdims=True)
            hot = eidx == sel
            vals.append(m)
            sels.append(sel)
            hots.append(hot)
            l = jnp.where(hot, -jnp.inf, l)
        exps = [jnp.exp(v - vals[0]) for v in vals]
        den = exps[0]
        for e in exps[1:]:
            den = den + e
        member = hots[0].astype(F32)
        for hot in hots[1:]:
            member = member + hot.astype(F32)
        before = jnp.dot(member.astype(BF16), tri, preferred_element_type=F32) + counts
        for k in range(TOP_K):
            rank = jnp.sum(jnp.where(hots[k], before, 0.0), axis=0, keepdims=True)
            idx_ref[pl.ds(k, 1), pl.ds(off, tb)] = sels[k]
            gate_ref[pl.ds(k, 1), pl.ds(off, tb)] = exps[k] / den
            rank_scr[pl.ds(k, 1), pl.ds(off, tb)] = rank.astype(I32)
        return counts + jnp.sum(member, axis=1, keepdims=True)

    counts = lax.fori_loop(0, t // tb, pass1, jnp.zeros((n_exp, 1), F32))

    padded = jnp.ceil(counts * (1.0 / tm)) * tm
    er = lax.broadcasted_iota(I32, (n_exp, LANES), 0)
    ec = lax.broadcasted_iota(I32, (n_exp, LANES), 1)

    def to_lanes(col):
        return jnp.sum(jnp.where(er == ec, col, 0.0), axis=0, keepdims=True)

    start = jnp.sum(jnp.where(ec < er, to_lanes(padded), 0.0), axis=1, keepdims=True)
    end = start + padded

    def pass2(jb, carry):
        off = pl.multiple_of(jb * tb, tb)
        for k in range(TOP_K):
            sel = idx_ref[pl.ds(k, 1), pl.ds(off, tb)]
            st = jnp.sum(jnp.where(eidx == sel, start, 0.0), axis=0, keepdims=True)
            pos_ref[pl.ds(k, 1), pl.ds(off, tb)] = st.astype(I32) + rank_scr[pl.ds(k, 1), pl.ds(off, tb)]
        return carry

    lax.fori_loop(0, t // tb, pass2, 0)

    tile_row = (lax.broadcasted_iota(I32, (n_exp, ntp), 1) * tm).astype(F32)
    tile_exp = jnp.sum((end <= tile_row).astype(F32), axis=0, keepdims=True)
    tile_exp = jnp.minimum(tile_exp, n_exp - 1.0)
    n_used = jnp.sum(padded, axis=0, keepdims=True) * (1.0 / tm)
    tmeta_ref[...] = jnp.zeros(tmeta_ref.shape, I32)
    tmeta_ref[0:1, :] = tile_exp.astype(I32)
    tmeta_ref[1:2, :] = jnp.broadcast_to(n_used, (1, ntp)).astype(I32)
    emeta_ref[...] = jnp.zeros(emeta_ref.shape, I32)
    emeta_ref[0:1, :] = to_lanes(start).astype(I32)
    emeta_ref[1:2, :] = to_lanes(counts).astype(I32)


def _route(logits_t, tm, n_tiles):
    n_exp, t = logits_t.shape
    tb = _tile(t, 512)
    ntp = pl.cdiv(n_tiles, LANES) * LANES
    full = lambda shape: pl.BlockSpec(shape, lambda: (0,) * len(shape))
    return pl.pallas_call(
        functools.partial(_route_kernel, tm=tm, tb=tb),
        in_specs=[full((n_exp, t))],
        out_specs=[full((TOP_K, t)), full((TOP_K, t)), full((TOP_K, t)), full((8, ntp)), full((8, LANES))],
        out_shape=[jax.ShapeDtypeStruct((TOP_K, t), I32),
                   jax.ShapeDtypeStruct((TOP_K, t), F32),
                   jax.ShapeDtypeStruct((TOP_K, t), I32),
                   jax.ShapeDtypeStruct((8, ntp), I32),
                   jax.ShapeDtypeStruct((8, LANES), I32)],
        scratch_shapes=[pltpu.VMEM((TOP_K, t), I32)],
        compiler_params=pltpu.CompilerParams(vmem_limit_bytes=V7X_VMEM_LIMIT_BYTES),
        name="route",
    )(logits_t)


def _dispatch_kernel(estart_ref, ecnt_ref, nused_ref, pos_hbm, h_ref, xs_hbm, pos_smem, zero_scr,
                     sem_idx, sem_row, sem_zero, *, tm, nch):
    i = pl.program_id(0)
    tmd = h_ref.shape[0] // nch
    n_exp = estart_ref.shape[0]
    n_tiles = xs_hbm.shape[0] // (tm * nch)

    idx_copy = pltpu.make_async_copy(pos_hbm.at[i], pos_smem, sem_idx)
    idx_copy.start()

    @pl.when(i == 0)
    def _():
        zero_scr[...] = jnp.zeros(zero_scr.shape, F32)

        def fill(e, carry):
            cnt = ecnt_ref[e]
            pad = (tm - cnt % tm) % tm
            base = estart_ref[e] + cnt
            size = tm // 2
            while size >= 1:
                off = pad & ~(2 * size - 1)

                @pl.when((pad & size) != 0)
                def _(size=size, off=off):
                    dst0 = pl.multiple_of((base + off) * nch, nch)
                    cp = pltpu.make_async_copy(zero_scr.at[pl.ds(0, size * nch)],
                                               xs_hbm.at[pl.ds(dst0, size * nch)], sem_zero)
                    cp.start()
                    cp.wait()
                size //= 2
            return carry

        lax.fori_loop(0, n_exp, fill, 0)

        def tail_copy(j, part):
            dst0 = pl.multiple_of((j * tm + part * (tm // 2)) * nch, nch)
            return pltpu.make_async_copy(zero_scr, xs_hbm.at[pl.ds(dst0, tm // 2 * nch)], sem_zero)

        def tail_start(j, carry):
            tail_copy(j, 0).start()
            tail_copy(j, 1).start()
            return carry

        def tail_wait(j, carry):
            tail_copy(j, 0).wait()
            tail_copy(j, 1).wait()
            return carry

        lax.fori_loop(nused_ref[0], n_tiles, tail_start, 0)
        lax.fori_loop(nused_ref[0], n_tiles, tail_wait, 0)

    idx_copy.wait()

    def issue(tok, carry):
        src0 = pl.multiple_of(tok * nch, nch)
        for k in range(TOP_K):
            dst0 = pl.multiple_of(pos_smem[k * tmd + tok] * nch, nch)
            pltpu.make_async_copy(h_ref.at[pl.ds(src0, nch)], xs_hbm.at[pl.ds(dst0, nch)], sem_row).start()
        return carry

    lax.fori_loop(0, tmd, issue, 0)
    for _ in range(TOP_K):
        pltpu.make_async_copy(h_ref, xs_hbm.at[pl.ds(0, tmd * nch)], sem_row).wait()


def _dispatch(h2, pos_tiles, estart, ecnt, n_used, tm, n_tiles, nch):
    ntt, per_tile = pos_tiles.shape
    tmd = per_tile // TOP_K
    grid_spec = pltpu.PrefetchScalarGridSpec(
        num_scalar_prefetch=3,
        grid=(ntt,),
        in_specs=[pl.BlockSpec(memory_space=pl.ANY),
                  pl.BlockSpec((tmd * nch, LANES), lambda i, es, ec, nu: (i, 0))],
        out_specs=pl.BlockSpec(memory_space=pl.ANY),
        scratch_shapes=[pltpu.SMEM((per_tile,), I32),
                        pltpu.VMEM((tm // 2 * nch, LANES), F32),
                        pltpu.SemaphoreType.DMA, pltpu.SemaphoreType.DMA, pltpu.SemaphoreType.DMA],
    )
    return pl.pallas_call(
        functools.partial(_dispatch_kernel, tm=tm, nch=nch),
        grid_spec=grid_spec,
        out_shape=jax.ShapeDtypeStruct((n_tiles * tm * nch, LANES), F32),
        compiler_params=_params(1),
        name="dispatch",
    )(estart, ecnt, n_used, pos_tiles, h2)


def _gate_up_kernel(texp_ref, nused_ref, x_ref, w_ref, b_ref, act_ref, *, chunk, nch):
    i = pl.program_id(0)

    @pl.when(i < nused_ref[0])
    def _():
        tm, ff = act_ref.shape
        x = jnp.concatenate([x_ref[pl.ds(ch, tm, stride=nch), :] for ch in range(nch)], axis=-1).astype(BF16)
        for c0 in range(0, ff, chunk):
            gate = jnp.dot(x, w_ref[:, c0:c0 + chunk], preferred_element_type=F32) + b_ref[:, c0:c0 + chunk]
            up = (jnp.dot(x, w_ref[:, ff + c0:ff + c0 + chunk], preferred_element_type=F32)
                  + b_ref[:, ff + c0:ff + c0 + chunk])
            gate = jnp.minimum(gate, SWIGLU_LIMIT)
            up = jnp.clip(up, -SWIGLU_LIMIT, SWIGLU_LIMIT)
            act = (up + 1.0) * gate * jax.nn.sigmoid(gate * SWIGLU_ALPHA)
            act_ref[:, c0:c0 + chunk] = act.astype(BF16)

    @pl.when(i >= nused_ref[0])
    def _():
        act_ref[...] = jnp.zeros(act_ref.shape, BF16)


def _gate_up(xs, w_bf16, b, tile_exp, n_used, tm):
    n_exp, d, ff2 = w_bf16.shape
    nch = d // LANES
    rows = xs.shape[0] // nch
    ff = ff2 // 2
    n_tiles = rows // tm
    live = lambda i, te, nu: jnp.minimum(i, nu[0] - 1)
    grid_spec = pltpu.PrefetchScalarGridSpec(
        num_scalar_prefetch=2,
        grid=(n_tiles,),
        in_specs=[pl.BlockSpec((tm * nch, LANES), lambda i, te, nu: (live(i, te, nu), 0)),
                  pl.BlockSpec((None, d, ff2), lambda i, te, nu: (te[i], 0, 0)),
                  pl.BlockSpec((None, 1, ff2), lambda i, te, nu: (te[i], 0, 0))],
        out_specs=pl.BlockSpec((tm, ff), lambda i, te, nu: (i, 0)),
    )
    return pl.pallas_call(
        functools.partial(_gate_up_kernel, chunk=_tile(ff, 512), nch=nch),
        grid_spec=grid_spec,
        out_shape=jax.ShapeDtypeStruct((rows, ff), BF16),
        compiler_params=_params(1),
        name="gate_up",
    )(tile_exp, n_used, xs, w_bf16, b.reshape(n_exp, 1, ff2))


def _down_kernel(texp_ref, nused_ref, a_ref, w_ref, b_ref, y_ref):
    i = pl.program_id(0)

    @pl.when(i < nused_ref[0])
    def _():
        y = jnp.dot(a_ref[...], w_ref[...], preferred_element_type=F32) + b_ref[...]
        tm, d = y.shape
        nch = d // LANES
        for ch in range(nch):
            y_ref[pl.ds(ch, tm, stride=nch), :] = y[:, ch * LANES:(ch + 1) * LANES]

    @pl.when(i >= nused_ref[0])
    def _():
        y_ref[...] = jnp.zeros(y_ref.shape, F32)


def _down(act, w_bf16, b, tile_exp, n_used, tm):
    rows, ff = act.shape
    n_exp, _, d = w_bf16.shape
    nch = d // LANES
    n_tiles = rows // tm
    live = lambda i, te, nu: jnp.minimum(i, nu[0] - 1)
    grid_spec = pltpu.PrefetchScalarGridSpec(
        num_scalar_prefetch=2,
        grid=(n_tiles,),
        in_specs=[pl.BlockSpec((tm, ff), lambda i, te, nu: (live(i, te, nu), 0)),
                  pl.BlockSpec((None, ff, d), lambda i, te, nu: (te[i], 0, 0)),
                  pl.BlockSpec((None, 1, d), lambda i, te, nu: (te[i], 0, 0))],
        out_specs=pl.BlockSpec((tm * nch, LANES), lambda i, te, nu: (i, 0)),
    )
    return pl.pallas_call(
        _down_kernel,
        grid_spec=grid_spec,
        out_shape=jax.ShapeDtypeStruct(_rows_shape(rows, d), F32),
        compiler_params=_params(1),
        name="down",
    )(tile_exp, n_used, act, w_bf16, b.reshape(n_exp, 1, d))


def _combine_kernel(pos_hbm, y_hbm, x1_ref, mod_ref, g_ref, o_ref, pos_smem, ybuf, sem_idx, sem_row):
    i = pl.program_id(0)
    tmc, d = x1_ref.shape

    idx_copy = pltpu.make_async_copy(pos_hbm.at[i], pos_smem, sem_idx)
    idx_copy.start()
    idx_copy.wait()

    nch = d // LANES

    def issue(tok, carry):
        dst0 = pl.multiple_of(tok * nch, nch)
        for k in range(TOP_K):
            src0 = pl.multiple_of(pos_smem[k * tmc + tok] * nch, nch)
            pltpu.make_async_copy(y_hbm.at[pl.ds(src0, nch)], ybuf.at[k, pl.ds(dst0, nch)], sem_row).start()
        return carry

    lax.fori_loop(0, tmc, issue, 0)
    for k in range(TOP_K):
        pltpu.make_async_copy(y_hbm.at[pl.ds(0, tmc * nch)], ybuf.at[k], sem_row).wait()

    g = g_ref[...]
    gate2 = mod_ref[...][5:6]
    for ch in range(nch):
        cols = slice(ch * LANES, (ch + 1) * LANES)
        moe = ybuf[0, pl.ds(ch, tmc, stride=nch), :] * g[:, 0:1]
        for k in range(1, TOP_K):
            moe = moe + ybuf[k, pl.ds(ch, tmc, stride=nch), :] * g[:, k:k + 1]
        o_ref[:, cols] = x1_ref[:, cols] + gate2[:, cols] * moe


def _combine(y, pos_tiles, x1, mod3, gates_t, seq):
    t, d = x1.shape
    ntt, per_tile = pos_tiles.shape
    tmc = per_tile // TOP_K
    per_b = seq // tmc
    return pl.pallas_call(
        _combine_kernel,
        grid=(ntt,),
        in_specs=[pl.BlockSpec(memory_space=pl.ANY),
                  pl.BlockSpec(memory_space=pl.ANY),
                  pl.BlockSpec((tmc, d), lambda i: (i, 0)),
                  pl.BlockSpec((None, 6, d), lambda i: (i // per_b, 0, 0)),
                  pl.BlockSpec((tmc, TOP_K), lambda i: (i, 0))],
        out_specs=pl.BlockSpec((tmc, d), lambda i: (i, 0)),
        out_shape=jax.ShapeDtypeStruct((t, d), F32),
        scratch_shapes=[pltpu.SMEM((per_tile,), I32),
                        pltpu.VMEM((TOP_K, tmc * (d // LANES), LANES), F32),
                        pltpu.SemaphoreType.DMA, pltpu.SemaphoreType.DMA],
        compiler_params=_params(1),
        name="combine",
    )(pos_tiles, y, x1, mod3, gates_t)


def _moe(h2, logits_t, x1, mod3, w_gu, b_gu, w_dn, b_dn, seq):
    t, d = x1.shape
    n_exp = logits_t.shape[0]
    tm = _tile(t * TOP_K, 256)
    n_tiles = (t * TOP_K) // tm + n_exp
    idx, gates, pos, tmeta, emeta = _route(logits_t, tm, n_tiles)
    del idx
    tok_tile = _tile(seq, 256)
    pos_tiles = (pos.reshape(TOP_K, t // tok_tile, tok_tile).transpose(1, 0, 2)
                 .reshape(t // tok_tile, TOP_K * tok_tile))
    tile_exp = tmeta[0, :n_tiles]
    n_used = tmeta[1, :1]
    xs = _dispatch(h2, pos_tiles, emeta[0, :n_exp], emeta[1, :n_exp], n_used, tm, n_tiles, d // LANES)
    act = _gate_up(xs, w_gu.astype(BF16), b_gu, tile_exp, n_used, tm)
    y = _down(act, w_dn.astype(BF16), b_dn, tile_exp, n_used, tm)
    return _combine(y, pos_tiles, x1, mod3, gates.T, seq)


def kernel(x, c, w_ada, b_ada, norm1_g, w_in, nat_q_g, nat_k_g, nat_rpb, diff_q_g, diff_k_g, diff_lambda,
           diff_sub_g, rel_bias_table, w_out, norm2_g, router_w, router_b, w_gate_up, b_gate_up, w_down, b_down):
    bsz, seq, d = x.shape
    t = bsz * seq
    ns = (d // 2) // HEAD_DIM
    n_exp = router_w.shape[-1]
    scale = HEAD_DIM ** -0.5
    rows = seq // GRID_W
    kr = min(NAT_KR_MAX, rows)
    strip = _t5_bias_strip(rel_bias_table, seq, _tile(seq, 256))
    ones = jnp.ones((HEAD_DIM,), F32)
    xf = x.reshape(t, d)
    for l in range(w_ada.shape[0]):
        lambda_init = 0.8 - 0.6 * math.exp(-0.3 * l)
        mod3 = _adaln(c, w_ada[l], b_ada[l]).reshape(bsz, 6, d)
        gains = jnp.stack([nat_q_g[l] * scale, nat_k_g[l], ones, diff_q_g[l] * scale, diff_k_g[l], ones, ones, ones])
        qkv = _inproj(xf, mod3, norm1_g[l], w_in[l].astype(BF16), gains, seq)
        nat = _nat_attention(qkv, _nat_bias_table(nat_rpb[l], kr), bsz, seq, ns)
        dif = _diff_attention(qkv, strip, diff_lambda[l], diff_sub_g[l], lambda_init, bsz, seq, ns)
        rw_pad = jnp.pad(router_w[l], ((0, 0), (0, LANES - n_exp)))
        x1, h2, logits_t = _outproj(nat, dif, xf, mod3, w_out[l].astype(BF16), norm2_g[l], rw_pad, router_b[l], seq)
        xf = _moe(h2, logits_t, x1, mod3, w_gate_up[l], b_gate_up[l], w_down[l], b_down[l], seq)
    return xf.reshape(bsz, seq, d)
```

```python
import functools
import math

import jax
import jax.numpy as jnp
from jax import lax
from jax.experimental import pallas as pl
from jax.experimental.pallas import tpu as pltpu

HEAD_DIM = 128
GRID_W = 64
NAT_KR_MAX = 8
NAT_KC = 16
N_BUCKETS = 32
MAX_DISTANCE = 128
TOP_K = 4
SWIGLU_LIMIT = 7.0
SWIGLU_ALPHA = 1.702
RMS_EPS = 1e-6
NEG_INF = -1e30
LOG2_E = 1.4426950408889634

LANES = 128
V7X_VMEM_LIMIT_BYTES = 56 * 1024 * 1024

F32 = jnp.float32
BF16 = jnp.bfloat16
I32 = jnp.int32


def _tile(n, pref):
    t = min(n, pref)
    assert n % t == 0, (n, pref)
    return t


def _params(n_axes):
    return pltpu.CompilerParams(dimension_semantics=("arbitrary",) * n_axes,
                                vmem_limit_bytes=V7X_VMEM_LIMIT_BYTES)


def _adaln_kernel(c_ref, w_ref, b_ref, o_ref):
    c = c_ref[...]
    a = (c * jax.nn.sigmoid(c)).astype(BF16)
    o_ref[...] = jnp.dot(a, w_ref[...].astype(BF16), preferred_element_type=F32) + b_ref[...]


def _adaln(c, w, b):
    bsz, d = c.shape
    n = w.shape[1]
    tn = _tile(n, 1024)
    return pl.pallas_call(
        _adaln_kernel,
        grid=(n // tn,),
        in_specs=[pl.BlockSpec((bsz, d), lambda j: (0, 0)),
                  pl.BlockSpec((d, tn), lambda j: (0, j)),
                  pl.BlockSpec((1, tn), lambda j: (0, j))],
        out_specs=pl.BlockSpec((bsz, tn), lambda j: (0, j)),
        out_shape=jax.ShapeDtypeStruct((bsz, n), F32),
        compiler_params=_params(1),
        name="adaln",
    )(c, w, b.reshape(1, n))


def _inproj_kernel(x_ref, mod_ref, g_ref, w_ref, gain_ref, o_ref, h_scr):
    j = pl.program_id(1)

    @pl.when(j == 0)
    def _():
        x = x_ref[...]
        ms = jnp.mean(x * x, axis=-1, keepdims=True)
        y = x * lax.rsqrt(ms + RMS_EPS) * g_ref[...]
        m = mod_ref[...]
        h_scr[...] = (y * (1.0 + m[1:2]) + m[0:1]).astype(BF16)

    y = jnp.dot(h_scr[...], w_ref[...].astype(BF16), preferred_element_type=F32)
    heads = o_ref.shape[0]
    is_value = (j == 2) | (j == 5)

    @pl.when(is_value)
    def _():
        for hh in range(heads):
            o_ref[hh] = y[:, hh * HEAD_DIM:(hh + 1) * HEAD_DIM].astype(BF16)

    @pl.when(jnp.logical_not(is_value))
    def _():
        g = gain_ref[pl.ds(j, 1), :]
        for hh in range(heads):
            yh = y[:, hh * HEAD_DIM:(hh + 1) * HEAD_DIM]
            ms = jnp.mean(yh * yh, axis=-1, keepdims=True)
            o_ref[hh] = (yh * lax.rsqrt(ms + RMS_EPS) * g).astype(BF16)


def _inproj(xf, mod3, g1, w, gains, seq):
    t, d = xf.shape
    seg = d // 2
    ns = seg // HEAD_DIM
    tm = _tile(seq, 1024)
    per_b = seq // tm
    return pl.pallas_call(
        _inproj_kernel,
        grid=(t // tm, 6),
        in_specs=[pl.BlockSpec((tm, d), lambda i, j: (i, 0)),
                  pl.BlockSpec((None, 6, d), lambda i, j: (i // per_b, 0, 0)),
                  pl.BlockSpec((1, d), lambda i, j: (0, 0)),
                  pl.BlockSpec((d, seg), lambda i, j: (0, j)),
                  pl.BlockSpec((8, HEAD_DIM), lambda i, j: (0, 0))],
        out_specs=pl.BlockSpec((ns, tm, HEAD_DIM), lambda i, j: (j, i, 0)),
        out_shape=jax.ShapeDtypeStruct((6 * ns, t, HEAD_DIM), BF16),
        scratch_shapes=[pltpu.VMEM((tm, d), BF16)],
        compiler_params=_params(2),
        name="inproj",
    )(xf, mod3, g1.reshape(1, d), w, gains)


def _nat_kernel(q_ref, k_ref, v_ref, b_ref, o_ref, *, rows, kr, group):
    band = kr * GRID_W

    def body(g, carry):
        k0s, scores = [], []
        for i in range(group):
            r = g * group + i
            rs = jnp.clip(r - kr // 2, 0, rows - kr)
            k0 = pl.multiple_of(rs * GRID_W, GRID_W)
            q = q_ref[pl.ds(pl.multiple_of(r * GRID_W, GRID_W), GRID_W), :]
            s = lax.dot_general(q, k_ref[pl.ds(k0, band), :], (((1,), (1,)), ((), ())),
                                preferred_element_type=F32)
            k0s.append(k0)
            scores.append(s + b_ref[r - rs])
        s = jnp.concatenate(scores, axis=0)
        p = jnp.exp(s - jnp.max(s, axis=-1, keepdims=True))
        den = jnp.sum(p, axis=-1, keepdims=True)
        pb = p.astype(BF16)
        outs = [jnp.dot(pb[i * GRID_W:(i + 1) * GRID_W], v_ref[pl.ds(k0s[i], band), :],
                        preferred_element_type=F32) for i in range(group)]
        o = jnp.concatenate(outs, axis=0) / den
        o_ref[pl.ds(pl.multiple_of(g * (group * GRID_W), group * GRID_W), group * GRID_W), :] = o.astype(BF16)
        return carry

    lax.fori_loop(0, rows // group, body, 0)


def _nat_bias_table(rpb, kr):
    j = jnp.arange(kr)
    row_off = j[None, :] - j[:, None] + (NAT_KR_MAX - 1)
    col = jnp.arange(GRID_W)
    col_start = jnp.clip(col - NAT_KC // 2, 0, GRID_W - NAT_KC)
    col_mask = (col[None, :] >= col_start[:, None]) & (col[None, :] < col_start[:, None] + NAT_KC)
    edge = GRID_W - NAT_KC
    u = jnp.pad(rpb[:, row_off].astype(F32), ((0, 0), (0, 0), (0, 0), (edge, edge + 1)), mode="edge")
    bias = _toeplitz(u, GRID_W)[..., GRID_W - 1:2 * GRID_W - 1]
    bias = jnp.where(col_mask[None, None, None], bias, NEG_INF)
    bias = bias.transpose(0, 1, 3, 2, 4)
    return bias.reshape(rpb.shape[0], kr, GRID_W, kr * GRID_W)


def _nat_attention(qkv, bias_tab, bsz, seq, ns):
    t = qkv.shape[1]
    rows = seq // GRID_W
    kr = min(NAT_KR_MAX, rows)
    blk = (None, seq, HEAD_DIM)
    return pl.pallas_call(
        functools.partial(_nat_kernel, rows=rows, kr=kr, group=_tile(rows, 8)),
        grid=(ns, bsz),
        in_specs=[pl.BlockSpec(blk, lambda h, b: (h, b, 0)),
                  pl.BlockSpec(blk, lambda h, b: (ns + h, b, 0)),
                  pl.BlockSpec(blk, lambda h, b: (2 * ns + h, b, 0)),
                  pl.BlockSpec((None, kr, GRID_W, kr * GRID_W), lambda h, b: (h, 0, 0, 0))],
        out_specs=pl.BlockSpec(blk, lambda h, b: (h, b, 0)),
        out_shape=jax.ShapeDtypeStruct((ns, t, HEAD_DIM), BF16),
        compiler_params=_params(2),
        name="nat_attn",
    )(qkv, qkv, qkv, bias_tab)


def _t5_bucket(rel):
    nb = N_BUCKETS // 2
    max_exact = nb // 2
    ret = jnp.where(rel > 0, nb, 0)
    n = jnp.abs(rel)
    nf = jnp.maximum(n, 1).astype(F32)
    large = max_exact + (jnp.log(nf / max_exact) / math.log(MAX_DISTANCE / max_exact)
                         * (nb - max_exact)).astype(I32)
    large = jnp.minimum(large, nb - 1)
    return ret + jnp.where(n < max_exact, n, large)


def _t5_bias_strip(rel_table, seq, tq):
    n = 2 * seq
    rel = jnp.arange(n, dtype=I32) - (seq - 1)
    u = rel_table[_t5_bucket(rel)].T.astype(F32) * LOG2_E
    return _toeplitz(u, tq)[:, :, tq - 1:2 * seq - 1]


def _toeplitz(u, rows):
    n = u.shape[-1]
    tiled = jnp.tile(u, (1,) * (u.ndim - 1) + (rows,))[..., :rows * (n - 1)]
    return tiled.reshape(u.shape[:-1] + (rows, n - 1))


def _diff_kernel(q_ref, k_ref, v_ref, e_ref, lam_ref, sg_ref, o_ref, *, seq, tq, nq, n_split, lambda_init):
    qi = pl.program_id(2)
    m0 = pl.multiple_of((nq - 1 - qi) * tq, tq)
    bias = e_ref[:, pl.ds(m0, seq)]
    lam = lam_ref[...]
    lam_full = (jnp.exp(jnp.sum(lam[0:1] * lam[1:2], axis=-1, keepdims=True))
                - jnp.exp(jnp.sum(lam[2:3] * lam[3:4], axis=-1, keepdims=True)) + lambda_init)

    v = jnp.concatenate([v_ref[0], v_ref[1]], axis=-1)
    half = tq // n_split

    def scores(p, rows):
        s = lax.dot_general(q_ref[p, rows, :], k_ref[p], (((1,), (1,)), ((), ())), preferred_element_type=F32)
        return s + bias[rows]

    def attend(s1, s2):
        e1 = jnp.exp2(s1 - jnp.max(s1, axis=-1, keepdims=True))
        e2 = jnp.exp2(s2 - jnp.max(s2, axis=-1, keepdims=True))
        r1 = 1.0 / jnp.sum(e1, axis=-1, keepdims=True)
        r2 = lam_full / jnp.sum(e2, axis=-1, keepdims=True)
        o = jnp.dot((e1 * r1 - e2 * r2).astype(BF16), v, preferred_element_type=F32)
        ms = jnp.mean(o * o, axis=-1, keepdims=True)
        return o * lax.rsqrt(ms + RMS_EPS) * sg_ref[...] * (1.0 - lambda_init)

    blocks = [slice(j * half, (j + 1) * half) for j in range(n_split)]
    all_scores = [(scores(0, rows), scores(1, rows)) for rows in blocks]
    for rows, (s1, s2) in zip(blocks, all_scores):
        o = attend(s1, s2)
        o_ref[0, rows, :] = o[:, :HEAD_DIM].astype(BF16)
        o_ref[1, rows, :] = o[:, HEAD_DIM:].astype(BF16)


def _diff_attention(qkv, strip, lam, sub_g, lambda_init, bsz, seq, ns):
    t = qkv.shape[1]
    hd = ns // 2
    tq = strip.shape[1]
    nq = seq // tq
    qb, kb, vb = 3 * ns // 2, 4 * ns // 2, 5 * ns // 2
    return pl.pallas_call(
        functools.partial(_diff_kernel, seq=seq, tq=tq, nq=nq, n_split=2, lambda_init=lambda_init),
        grid=(hd, bsz, nq),
        in_specs=[pl.BlockSpec((2, tq, HEAD_DIM), lambda h, b, i: (qb + h, b * nq + i, 0)),
                  pl.BlockSpec((2, seq, HEAD_DIM), lambda h, b, i: (kb + h, b, 0)),
                  pl.BlockSpec((2, seq, HEAD_DIM), lambda h, b, i: (vb + h, b, 0)),
                  pl.BlockSpec((None, tq, 2 * seq - tq), lambda h, b, i: (h, 0, 0)),
                  pl.BlockSpec((4, HEAD_DIM), lambda h, b, i: (0, 0)),
                  pl.BlockSpec((1, 2 * HEAD_DIM), lambda h, b, i: (0, 0))],
        out_specs=pl.BlockSpec((2, tq, HEAD_DIM), lambda h, b, i: (h, b * nq + i, 0)),
        out_shape=jax.ShapeDtypeStruct((ns, t, HEAD_DIM), BF16),
        compiler_params=_params(3),
        name="diff_attn",
    )(qkv, qkv, qkv, strip, lam, sub_g.reshape(1, 2 * HEAD_DIM))


def _rows_shape(rows, d):
    return (rows * (d // LANES), LANES)


def _split_bf16(v):
    hi = v.astype(BF16)
    lo = (v - hi.astype(F32)).astype(BF16)
    return hi, lo


def _outproj_kernel(nat_ref, dif_ref, x_ref, mod_ref, w_ref, g_ref, rw_ref, rb_ref,
                    x1_ref, h2_ref, lt_ref):
    ns = nat_ref.shape[0]
    a = jnp.concatenate([nat_ref[hh] for hh in range(ns)] + [dif_ref[hh] for hh in range(ns)], axis=-1)
    mix = jnp.dot(a, w_ref[...].astype(BF16), preferred_element_type=F32)
    m = mod_ref[...]
    x1 = x_ref[...] + m[2:3] * mix
    x1_ref[...] = x1
    ms = jnp.mean(x1 * x1, axis=-1, keepdims=True)
    h2 = x1 * lax.rsqrt(ms + RMS_EPS) * g_ref[...] * (1.0 + m[4:5]) + m[3:4]
    n_chunks = h2.shape[1] // LANES
    for ch in range(n_chunks):
        h2_ref[pl.ds(ch, h2.shape[0], stride=n_chunks), :] = h2[:, ch * LANES:(ch + 1) * LANES]
    h_hi, h_lo = _split_bf16(h2)
    w_hi, w_lo = _split_bf16(rw_ref[...])
    lg = (jnp.dot(h_hi, w_hi, preferred_element_type=F32)
          + jnp.dot(h_lo, w_hi, preferred_element_type=F32)
          + jnp.dot(h_hi, w_lo, preferred_element_type=F32))
    n_exp = lt_ref.shape[0]
    lt_ref[...] = lg.T[:n_exp] + rb_ref[...]


def _outproj(nat, dif, xf, mod3, w, g2, rw_pad, rb, seq):
    t, d = xf.shape
    ns = nat.shape[0]
    n_exp = rb.shape[0]
    tm = _tile(seq, 512)
    per_b = seq // tm
    return pl.pallas_call(
        _outproj_kernel,
        grid=(t // tm,),
        in_specs=[pl.BlockSpec((ns, tm, HEAD_DIM), lambda i: (0, i, 0)),
                  pl.BlockSpec((ns, tm, HEAD_DIM), lambda i: (0, i, 0)),
                  pl.BlockSpec((tm, d), lambda i: (i, 0)),
                  pl.BlockSpec((None, 6, d), lambda i: (i // per_b, 0, 0)),
                  pl.BlockSpec((d, d), lambda i: (0, 0), pipeline_mode=pl.Buffered(1)),
                  pl.BlockSpec((1, d), lambda i: (0, 0)),
                  pl.BlockSpec((d, LANES), lambda i: (0, 0)),
                  pl.BlockSpec((n_exp, 1), lambda i: (0, 0))],
        out_specs=[pl.BlockSpec((tm, d), lambda i: (i, 0)),
                   pl.BlockSpec((tm * (d // LANES), LANES), lambda i: (i, 0)),
                   pl.BlockSpec((n_exp, tm), lambda i: (0, i))],
        out_shape=[jax.ShapeDtypeStruct((t, d), F32),
                   jax.ShapeDtypeStruct(_rows_shape(t, d), F32),
                   jax.ShapeDtypeStruct((n_exp, t), F32)],
        compiler_params=_params(1),
        name="outproj",
    )(nat, dif, xf, mod3, w, g2.reshape(1, d), rw_pad, rb.reshape(n_exp, 1))


def _route_kernel(lt_ref, idx_ref, gate_ref, pos_ref, tmeta_ref, emeta_ref, rank_scr, *, tm, tb):
    n_exp, t = lt_ref.shape
    ntp = tmeta_ref.shape[1]
    eidx = lax.broadcasted_iota(I32, (n_exp, tb), 0)
    tri = (lax.broadcasted_iota(I32, (tb, tb), 0) < lax.broadcasted_iota(I32, (tb, tb), 1)).astype(BF16)

    def pass1(jb, counts):
        off = pl.multiple_of(jb * tb, tb)
        l = lt_ref[:, pl.ds(off, tb)]
        vals, sels, hots = [], [], []
        for _ in range(TOP_K):
            m = jnp.max(l, axis=0, keepdims=True)
            sel = jnp.min(jnp.where(l == m, eidx, n_exp), axis=0, keepdims=True)
            hot = eidx == sel
            vals.append(m)
            sels.append(sel)
            hots.append(hot)
            l = jnp.where(hot, -jnp.inf, l)
        exps = [jnp.exp(v - vals[0]) for v in vals]
        den = exps[0]
        for e in exps[1:]:
            den = den + e
        member = hots[0].astype(F32)
        for hot in hots[1:]:
            member = member + hot.astype(F32)
        before = jnp.dot(member.astype(BF16), tri, preferred_element_type=F32) + counts
        for k in range(TOP_K):
            rank = jnp.sum(jnp.where(hots[k], before, 0.0), axis=0, keepdims=True)
            idx_ref[pl.ds(k, 1), pl.ds(off, tb)] = sels[k]
            gate_ref[pl.ds(k, 1), pl.ds(off, tb)] = exps[k] / den
            rank_scr[pl.ds(k, 1), pl.ds(off, tb)] = rank.astype(I32)
        return counts + jnp.sum(member, axis=1, keepdims=True)

    counts = lax.fori_loop(0, t // tb, pass1, jnp.zeros((n_exp, 1), F32))

    padded = jnp.ceil(counts * (1.0 / tm)) * tm
    er = lax.broadcasted_iota(I32, (n_exp, LANES), 0)
    ec = lax.broadcasted_iota(I32, (n_exp, LANES), 1)

    def to_lanes(col):
        return jnp.sum(jnp.where(er == ec, col, 0.0), axis=0, keepdims=True)

    start = jnp.sum(jnp.where(ec < er, to_lanes(padded), 0.0), axis=1, keepdims=True)
    end = start + padded

    def pass2(jb, carry):
        off = pl.multiple_of(jb * tb, tb)
        for k in range(TOP_K):
            sel = idx_ref[pl.ds(k, 1), pl.ds(off, tb)]
            st = jnp.sum(jnp.where(eidx == sel, start, 0.0), axis=0, keepdims=True)
            pos_ref[pl.ds(k, 1), pl.ds(off, tb)] = st.astype(I32) + rank_scr[pl.ds(k, 1), pl.ds(off, tb)]
        return carry

    lax.fori_loop(0, t // tb, pass2, 0)

    tile_row = (lax.broadcasted_iota(I32, (n_exp, ntp), 1) * tm).astype(F32)
    tile_exp = jnp.sum((end <= tile_row).astype(F32), axis=0, keepdims=True)
    tile_exp = jnp.minimum(tile_exp, n_exp - 1.0)
    n_used = jnp.sum(padded, axis=0, keepdims=True) * (1.0 / tm)
    tmeta_ref[...] = jnp.zeros(tmeta_ref.shape, I32)
    tmeta_ref[0:1, :] = tile_exp.astype(I32)
    tmeta_ref[1:2, :] = jnp.broadcast_to(n_used, (1, ntp)).astype(I32)
    emeta_ref[...] = jnp.zeros(emeta_ref.shape, I32)
    emeta_ref[0:1, :] = to_lanes(start).astype(I32)
    emeta_ref[1:2, :] = to_lanes(counts).astype(I32)
    emeta_ref[2:3, :] = to_lanes(padded).astype(I32)


def _route(logits_t, tm, n_tiles):
    n_exp, t = logits_t.shape
    tb = _tile(t, 512)
    ntp = pl.cdiv(n_tiles, LANES) * LANES
    full = lambda shape: pl.BlockSpec(shape, lambda: (0,) * len(shape))
    return pl.pallas_call(
        functools.partial(_route_kernel, tm=tm, tb=tb),
        in_specs=[full((n_exp, t))],
        out_specs=[full((TOP_K, t)), full((TOP_K, t)), full((TOP_K, t)), full((8, ntp)), full((8, LANES))],
        out_shape=[jax.ShapeDtypeStruct((TOP_K, t), I32),
                   jax.ShapeDtypeStruct((TOP_K, t), F32),
                   jax.ShapeDtypeStruct((TOP_K, t), I32),
                   jax.ShapeDtypeStruct((8, ntp), I32),
                   jax.ShapeDtypeStruct((8, LANES), I32)],
        scratch_shapes=[pltpu.VMEM((TOP_K, t), I32)],
        compiler_params=pltpu.CompilerParams(vmem_limit_bytes=V7X_VMEM_LIMIT_BYTES),
        name="route",
    )(logits_t)


def _dispatch_kernel(estart_ref, ecnt_ref, nused_ref, pos_hbm, h_ref, xs_hbm, pos_smem, zero_scr,
                     sem_idx, sem_row, sem_zero, *, tm, nch):
    i = pl.program_id(0)
    tmd = h_ref.shape[0] // nch
    n_exp = estart_ref.shape[0]
    n_tiles = xs_hbm.shape[0] // (tm * nch)

    idx_copy = pltpu.make_async_copy(pos_hbm.at[i], pos_smem, sem_idx)
    idx_copy.start()

    @pl.when(i == 0)
    def _():
        zero_scr[...] = jnp.zeros(zero_scr.shape, F32)

        def fill(e, carry):
            cnt = ecnt_ref[e]
            pad = (tm - cnt % tm) % tm
            base = estart_ref[e] + cnt
            size = tm // 2
            while size >= 1:
                off = pad & ~(2 * size - 1)

                @pl.when((pad & size) != 0)
                def _(size=size, off=off):
                    dst0 = pl.multiple_of((base + off) * nch, nch)
                    cp = pltpu.make_async_copy(zero_scr.at[pl.ds(0, size * nch)],
                                               xs_hbm.at[pl.ds(dst0, size * nch)], sem_zero)
                    cp.start()
                    cp.wait()
                size //= 2
            return carry

        lax.fori_loop(0, n_exp, fill, 0)

        def tail_copy(j, part):
            dst0 = pl.multiple_of((j * tm + part * (tm // 2)) * nch, nch)
            return pltpu.make_async_copy(zero_scr, xs_hbm.at[pl.ds(dst0, tm // 2 * nch)], sem_zero)

        def tail_start(j, carry):
            tail_copy(j, 0).start()
            tail_copy(j, 1).start()
            return carry

        def tail_wait(j, carry):
            tail_copy(j, 0).wait()
            tail_copy(j, 1).wait()
            return carry

        lax.fori_loop(nused_ref[0], n_tiles, tail_start, 0)
        lax.fori_loop(nused_ref[0], n_tiles, tail_wait, 0)

    idx_copy.wait()

    def issue(tok, carry):
        src0 = pl.multiple_of(tok * nch, nch)
        for k in range(TOP_K):
            dst0 = pl.multiple_of(pos_smem[k * tmd + tok] * nch, nch)
            pltpu.make_async_copy(h_ref.at[pl.ds(src0, nch)], xs_hbm.at[pl.ds(dst0, nch)],
                                  sem_row).start(priority=k % 2)
        return carry

    lax.fori_loop(0, tmd, issue, 0, unroll=4)
    for _ in range(TOP_K):
        pltpu.make_async_copy(h_ref, xs_hbm.at[pl.ds(0, tmd * nch)], sem_row).wait()


def _dispatch(h2, pos_tiles, estart, ecnt, n_used, tm, n_tiles, nch):
    ntt, per_tile = pos_tiles.shape
    tmd = per_tile // TOP_K
    grid_spec = pltpu.PrefetchScalarGridSpec(
        num_scalar_prefetch=3,
        grid=(ntt,),
        in_specs=[pl.BlockSpec(memory_space=pl.ANY),
                  pl.BlockSpec((tmd * nch, LANES), lambda i, es, ec, nu: (i, 0))],
        out_specs=pl.BlockSpec(memory_space=pl.ANY),
        scratch_shapes=[pltpu.SMEM((per_tile,), I32),
                        pltpu.VMEM((tm // 2 * nch, LANES), F32),
                        pltpu.SemaphoreType.DMA, pltpu.SemaphoreType.DMA, pltpu.SemaphoreType.DMA],
    )
    return pl.pallas_call(
        functools.partial(_dispatch_kernel, tm=tm, nch=nch),
        grid_spec=grid_spec,
        out_shape=jax.ShapeDtypeStruct((n_tiles * tm * nch, LANES), F32),
        compiler_params=_params(1),
        name="dispatch",
    )(estart, ecnt, n_used, pos_tiles, h2)


GATE_UP_SPLIT = 2


def _gate_up_kernel(stile_ref, shalf_ref, sexp_ref, nsteps_ref, x_ref, wg_ref, wu_ref, bg_ref, bu_ref, act_ref,
                    *, chunk, nch):
    s = pl.program_id(0)

    @pl.when(s < nsteps_ref[0])
    def _():
        tm, ffh = act_ref.shape
        x = jnp.concatenate([x_ref[pl.ds(ch, tm, stride=nch), :] for ch in range(nch)], axis=-1).astype(BF16)
        for c0 in range(0, ffh, chunk):
            cols = slice(c0, c0 + chunk)
            gate = jnp.dot(x, wg_ref[:, cols].astype(BF16), preferred_element_type=F32) + bg_ref[:, cols]
            up = jnp.dot(x, wu_ref[:, cols].astype(BF16), preferred_element_type=F32) + bu_ref[:, cols]
            gate = jnp.minimum(gate, SWIGLU_LIMIT)
            up = jnp.clip(up, -SWIGLU_LIMIT, SWIGLU_LIMIT)
            act = (up + 1.0) * gate * jax.nn.sigmoid(gate * SWIGLU_ALPHA)
            act_ref[:, cols] = act.astype(BF16)

    @pl.when(s >= nsteps_ref[0])
    def _():
        act_ref[...] = jnp.zeros(act_ref.shape, BF16)


def _gate_up_schedule(tile_exp, n_used, first_tile, group_tiles, n_tiles):
    ns = GATE_UP_SPLIT
    s = jnp.arange(ns * n_tiles, dtype=I32)
    e = tile_exp[s // ns]
    first = first_tile[e]
    count = jnp.maximum(group_tiles[e], 1)
    local = s - ns * first
    tile, half = first + local % count, local // count
    n_steps = ns * n_used
    last = jnp.maximum(n_steps - 1, 0)
    live = s < n_steps
    pick = lambda v: jnp.where(live, v, v[last])
    return pick(tile), pick(half), pick(e), n_steps


def _gate_up(xs, w, b, tile_exp, n_used, first_tile, group_tiles, tm):
    n_exp, d, ff2 = w.shape
    nch = d // LANES
    rows = xs.shape[0] // nch
    ff = ff2 // 2
    n_tiles = rows // tm
    ns = GATE_UP_SPLIT
    ffh = ff // ns
    stile, shalf, sexp, n_steps = _gate_up_schedule(tile_exp, n_used, first_tile, group_tiles, n_tiles)

    def out_index(s, st, sh, se, n):
        live = s < n[0]
        return jnp.where(live, st[s], s // ns), jnp.where(live, sh[s], s % ns)

    b3 = b.reshape(n_exp, 1, ff2)
    grid_spec = pltpu.PrefetchScalarGridSpec(
        num_scalar_prefetch=4,
        grid=(ns * n_tiles,),
        in_specs=[pl.BlockSpec((tm * nch, LANES), lambda s, st, sh, se, n: (st[s], 0)),
                  pl.BlockSpec((None, d, ffh), lambda s, st, sh, se, n: (se[s], 0, sh[s])),
                  pl.BlockSpec((None, d, ffh), lambda s, st, sh, se, n: (se[s], 0, ns + sh[s])),
                  pl.BlockSpec((None, 1, ffh), lambda s, st, sh, se, n: (se[s], 0, sh[s])),
                  pl.BlockSpec((None, 1, ffh), lambda s, st, sh, se, n: (se[s], 0, ns + sh[s]))],
        out_specs=pl.BlockSpec((tm, ffh), out_index),
    )
    return pl.pallas_call(
        functools.partial(_gate_up_kernel, chunk=_tile(ffh, 512), nch=nch),
        grid_spec=grid_spec,
        out_shape=jax.ShapeDtypeStruct((rows, ff), BF16),
        compiler_params=_params(1),
        name="gate_up",
    )(stile, shalf, sexp, n_steps, xs, w, w, b3, b3)


def _down_kernel(texp_ref, nused_ref, a_ref, w_ref, b_ref, y_ref):
    i = pl.program_id(0)

    @pl.when(i < nused_ref[0])
    def _():
        tm = a_ref.shape[0]
        nch = w_ref.shape[1] // LANES
        a = a_ref[...]
        chunk = _tile(w_ref.shape[1], 512)
        for c0 in range(0, w_ref.shape[1], chunk):
            y = (jnp.dot(a, w_ref[:, c0:c0 + chunk].astype(BF16), preferred_element_type=F32)
                 + b_ref[:, c0:c0 + chunk])
            for j in range(chunk // LANES):
                y_ref[pl.ds(c0 // LANES + j, tm, stride=nch), :] = y[:, j * LANES:(j + 1) * LANES]

    @pl.when(i >= nused_ref[0])
    def _():
        y_ref[...] = jnp.zeros(y_ref.shape, F32)


def _down(act, w, b, tile_exp, n_used, tm):
    rows, ff = act.shape
    n_exp, _, d = w.shape
    nch = d // LANES
    n_tiles = rows // tm
    live = lambda i, te, nu: jnp.minimum(i, nu[0] - 1)
    grid_spec = pltpu.PrefetchScalarGridSpec(
        num_scalar_prefetch=2,
        grid=(n_tiles,),
        in_specs=[pl.BlockSpec((tm, ff), lambda i, te, nu: (live(i, te, nu), 0)),
                  pl.BlockSpec((None, ff, d), lambda i, te, nu: (te[i], 0, 0)),
                  pl.BlockSpec((None, 1, d), lambda i, te, nu: (te[i], 0, 0))],
        out_specs=pl.BlockSpec((tm * nch, LANES), lambda i, te, nu: (i, 0)),
    )
    return pl.pallas_call(
        _down_kernel,
        grid_spec=grid_spec,
        out_shape=jax.ShapeDtypeStruct(_rows_shape(rows, d), F32),
        compiler_params=_params(1),
        name="down",
    )(tile_exp, n_used, act, w, b.reshape(n_exp, 1, d))


def _combine_kernel(pos_hbm, y_hbm, x1_ref, mod_ref, g_ref, o_ref, pos_smem, ybuf, sem_idx, sem_row):
    i = pl.program_id(0)
    n = pl.num_programs(0)
    tmc, d = x1_ref.shape
    nch = d // LANES
    slot = i % 2

    def idx_copy(step, s):
        return pltpu.make_async_copy(pos_hbm.at[step], pos_smem.at[s], sem_idx.at[s])

    def start_gathers_static(s):
        def issue(tok, carry):
            dst0 = pl.multiple_of(tok * nch, nch)
            for k in range(TOP_K):
                src0 = pl.multiple_of(pos_smem[s, k * tmc + tok] * nch, nch)
                pltpu.make_async_copy(y_hbm.at[pl.ds(src0, nch)], ybuf.at[s, k, pl.ds(dst0, nch)],
                                      sem_row.at[s]).start(priority=k % 2)
            return carry

        lax.fori_loop(0, tmc, issue, 0, unroll=4)

    def start_gathers(s):
        for static_s in range(2):
            pl.when(s == static_s)(functools.partial(start_gathers_static, static_s))

    @pl.when(i == 0)
    def _():
        idx_copy(0, 0).start()
        idx_copy(0, 0).wait()
        start_gathers_static(0)

        @pl.when(n > 1)
        def _():
            idx_copy(1, 1).start()

    @pl.when(i + 1 < n)
    def _():
        idx_copy(i + 1, 1 - slot).wait()
        start_gathers(1 - slot)

    @pl.when(i + 2 < n)
    def _():
        idx_copy(i + 2, slot).start()

    for k in range(TOP_K):
        pltpu.make_async_copy(y_hbm.at[pl.ds(0, tmc * nch)], ybuf.at[slot, k], sem_row.at[slot]).wait()

    g = g_ref[...]
    gate2 = mod_ref[...][5:6]
    for ch in range(nch):
        cols = slice(ch * LANES, (ch + 1) * LANES)
        moe = ybuf[slot, 0, pl.ds(ch, tmc, stride=nch), :] * g[:, 0:1]
        for k in range(1, TOP_K):
            moe = moe + ybuf[slot, k, pl.ds(ch, tmc, stride=nch), :] * g[:, k:k + 1]
        o_ref[:, cols] = x1_ref[:, cols] + gate2[:, cols] * moe


def _combine(y, pos_tiles, x1, mod3, gates_t, seq):
    t, d = x1.shape
    ntt, per_tile = pos_tiles.shape
    tmc = per_tile // TOP_K
    per_b = seq // tmc
    return pl.pallas_call(
        _combine_kernel,
        grid=(ntt,),
        in_specs=[pl.BlockSpec(memory_space=pl.ANY),
                  pl.BlockSpec(memory_space=pl.ANY),
                  pl.BlockSpec((tmc, d), lambda i: (i, 0)),
                  pl.BlockSpec((None, 6, d), lambda i: (i // per_b, 0, 0)),
                  pl.BlockSpec((tmc, TOP_K), lambda i: (i, 0))],
        out_specs=pl.BlockSpec((tmc, d), lambda i: (i, 0)),
        out_shape=jax.ShapeDtypeStruct((t, d), F32),
        scratch_shapes=[pltpu.SMEM((2, per_tile), I32),
                        pltpu.VMEM((2, TOP_K, tmc * (d // LANES), LANES), F32),
                        pltpu.SemaphoreType.DMA((2,)), pltpu.SemaphoreType.DMA((2,))],
        compiler_params=_params(1),
        name="combine",
    )(pos_tiles, y, x1, mod3, gates_t)


def _moe(h2, logits_t, x1, mod3, w_gu, b_gu, w_dn, b_dn, seq):
    t, d = x1.shape
    n_exp = logits_t.shape[0]
    tm = _tile(t * TOP_K, 256)
    n_tiles = (t * TOP_K) // tm + n_exp
    idx, gates, pos, tmeta, emeta = _route(logits_t, tm, n_tiles)
    del idx
    tok_tile = _tile(seq, 256)
    pos_tiles = (pos.reshape(TOP_K, t // tok_tile, tok_tile).transpose(1, 0, 2)
                 .reshape(t // tok_tile, TOP_K * tok_tile))
    tile_exp = tmeta[0, :n_tiles]
    n_used = tmeta[1, :1]
    xs = _dispatch(h2, pos_tiles, emeta[0, :n_exp], emeta[1, :n_exp], n_used, tm, n_tiles, d // LANES)
    act = _gate_up(xs, w_gu, b_gu, tile_exp, n_used, emeta[0, :n_exp] // tm, emeta[2, :n_exp] // tm, tm)
    y = _down(act, w_dn, b_dn, tile_exp, n_used, tm)
    return _combine(y, pos_tiles, x1, mod3, gates.T, seq)


def kernel(x, c, w_ada, b_ada, norm1_g, w_in, nat_q_g, nat_k_g, nat_rpb, diff_q_g, diff_k_g, diff_lambda,
           diff_sub_g, rel_bias_table, w_out, norm2_g, router_w, router_b, w_gate_up, b_gate_up, w_down, b_down):
    bsz, seq, d = x.shape
    t = bsz * seq
    ns = (d // 2) // HEAD_DIM
    n_exp = router_w.shape[-1]
    scale = HEAD_DIM ** -0.5
    rows = seq // GRID_W
    kr = min(NAT_KR_MAX, rows)
    strip = _t5_bias_strip(rel_bias_table, seq, _tile(seq, 256))
    ones = jnp.ones((HEAD_DIM,), F32)
    xf = x.reshape(t, d)
    for l in range(w_ada.shape[0]):
        lambda_init = 0.8 - 0.6 * math.exp(-0.3 * l)
        mod3 = _adaln(c, w_ada[l], b_ada[l]).reshape(bsz, 6, d)
        gains = jnp.stack([nat_q_g[l] * scale, nat_k_g[l], ones, diff_q_g[l] * (scale * LOG2_E), diff_k_g[l],
                           ones, ones, ones])
        qkv = _inproj(xf, mod3, norm1_g[l], w_in[l], gains, seq)
        nat = _nat_attention(qkv, _nat_bias_table(nat_rpb[l], kr), bsz, seq, ns)
        dif = _diff_attention(qkv, strip, diff_lambda[l], diff_sub_g[l], lambda_init, bsz, seq, ns)
        rw_pad = jnp.pad(router_w[l], ((0, 0), (0, LANES - n_exp)))
        x1, h2, logits_t = _outproj(nat, dif, xf, mod3, w_out[l], norm2_g[l], rw_pad, router_b[l], seq)
        xf = _moe(h2, logits_t, x1, mod3, w_gate_up[l], b_gate_up[l], w_down[l], b_down[l], seq)
    return xf.reshape(bsz, seq, d)
```

```python
import functools
import math

import jax
import jax.numpy as jnp
from jax import lax
from jax.experimental import pallas as pl
from jax.experimental.pallas import tpu as pltpu

HEAD_DIM = 128
GRID_W = 64
NAT_KR_MAX = 8
NAT_KC = 16
N_BUCKETS = 32
MAX_DISTANCE = 128
TOP_K = 4
SWIGLU_LIMIT = 7.0
SWIGLU_ALPHA = 1.702
RMS_EPS = 1e-6
NEG_INF = -1e30
LOG2_E = 1.4426950408889634

LANES = 128
V7X_VMEM_LIMIT_BYTES = 56 * 1024 * 1024

F32 = jnp.float32
BF16 = jnp.bfloat16
I32 = jnp.int32


def _tile(n, pref):
    t = min(n, pref)
    assert n % t == 0, (n, pref)
    return t


def _params(n_axes):
    return pltpu.CompilerParams(dimension_semantics=("arbitrary",) * n_axes,
                                vmem_limit_bytes=V7X_VMEM_LIMIT_BYTES)


def _adaln_kernel(c_ref, w_ref, b_ref, o_ref):
    c = c_ref[...]
    a = (c * jax.nn.sigmoid(c)).astype(BF16)
    o_ref[...] = jnp.dot(a, w_ref[...].astype(BF16), preferred_element_type=F32) + b_ref[...]


def _adaln(c, w, b):
    bsz, d = c.shape
    n = w.shape[1]
    tn = _tile(n, 1024)
    return pl.pallas_call(
        _adaln_kernel,
        grid=(n // tn,),
        in_specs=[pl.BlockSpec((bsz, d), lambda j: (0, 0)),
                  pl.BlockSpec((d, tn), lambda j: (0, j)),
                  pl.BlockSpec((1, tn), lambda j: (0, j))],
        out_specs=pl.BlockSpec((bsz, tn), lambda j: (0, j)),
        out_shape=jax.ShapeDtypeStruct((bsz, n), F32),
        compiler_params=_params(1),
        name="adaln",
    )(c, w, b.reshape(1, n))


def _inproj_kernel(x_ref, mod_ref, g_ref, w_ref, gain_ref, o_ref, h_scr):
    j = pl.program_id(1)

    @pl.when(j == 0)
    def _():
        x = x_ref[...]
        ms = jnp.mean(x * x, axis=-1, keepdims=True)
        y = x * lax.rsqrt(ms + RMS_EPS) * g_ref[...]
        m = mod_ref[...]
        h_scr[...] = (y * (1.0 + m[1:2]) + m[0:1]).astype(BF16)

    y = jnp.dot(h_scr[...], w_ref[...].astype(BF16), preferred_element_type=F32)
    heads = o_ref.shape[0]
    is_value = (j == 2) | (j == 5)

    @pl.when(is_value)
    def _():
        for hh in range(heads):
            o_ref[hh] = y[:, hh * HEAD_DIM:(hh + 1) * HEAD_DIM].astype(BF16)

    @pl.when(jnp.logical_not(is_value))
    def _():
        g = gain_ref[pl.ds(j, 1), :]
        for hh in range(heads):
            yh = y[:, hh * HEAD_DIM:(hh + 1) * HEAD_DIM]
            ms = jnp.mean(yh * yh, axis=-1, keepdims=True)
            o_ref[hh] = (yh * lax.rsqrt(ms + RMS_EPS) * g).astype(BF16)


def _inproj(xf, mod3, g1, w, gains, seq):
    t, d = xf.shape
    seg = d // 2
    ns = seg // HEAD_DIM
    tm = _tile(seq, 1024)
    per_b = seq // tm
    return pl.pallas_call(
        _inproj_kernel,
        grid=(t // tm, 6),
        in_specs=[pl.BlockSpec((tm, d), lambda i, j: (i, 0)),
                  pl.BlockSpec((None, 6, d), lambda i, j: (i // per_b, 0, 0)),
                  pl.BlockSpec((1, d), lambda i, j: (0, 0)),
                  pl.BlockSpec((d, seg), lambda i, j: (0, j)),
                  pl.BlockSpec((8, HEAD_DIM), lambda i, j: (0, 0))],
        out_specs=pl.BlockSpec((ns, tm, HEAD_DIM), lambda i, j: (j, i, 0)),
        out_shape=jax.ShapeDtypeStruct((6 * ns, t, HEAD_DIM), BF16),
        scratch_shapes=[pltpu.VMEM((tm, d), BF16)],
        compiler_params=_params(2),
        name="inproj",
    )(xf, mod3, g1.reshape(1, d), w, gains)


def _nat_kernel(q_ref, k_ref, v_ref, b_ref, o_ref, *, rows, kr, group):
    band = kr * GRID_W

    def body(g, carry):
        k0s, scores = [], []
        for i in range(group):
            r = g * group + i
            rs = jnp.clip(r - kr // 2, 0, rows - kr)
            k0 = pl.multiple_of(rs * GRID_W, GRID_W)
            q = q_ref[pl.ds(pl.multiple_of(r * GRID_W, GRID_W), GRID_W), :]
            s = lax.dot_general(q, k_ref[pl.ds(k0, band), :], (((1,), (1,)), ((), ())),
                                preferred_element_type=F32)
            k0s.append(k0)
            scores.append(s + b_ref[r - rs])
        s = jnp.concatenate(scores, axis=0)
        p = jnp.exp(s - jnp.max(s, axis=-1, keepdims=True))
        den = jnp.sum(p, axis=-1, keepdims=True)
        pb = p.astype(BF16)
        outs = [jnp.dot(pb[i * GRID_W:(i + 1) * GRID_W], v_ref[pl.ds(k0s[i], band), :],
                        preferred_element_type=F32) for i in range(group)]
        o = jnp.concatenate(outs, axis=0) / den
        o_ref[pl.ds(pl.multiple_of(g * (group * GRID_W), group * GRID_W), group * GRID_W), :] = o.astype(BF16)
        return carry

    lax.fori_loop(0, rows // group, body, 0)


def _nat_bias_table(rpb, kr):
    j = jnp.arange(kr)
    row_off = j[None, :] - j[:, None] + (NAT_KR_MAX - 1)
    col = jnp.arange(GRID_W)
    col_start = jnp.clip(col - NAT_KC // 2, 0, GRID_W - NAT_KC)
    col_mask = (col[None, :] >= col_start[:, None]) & (col[None, :] < col_start[:, None] + NAT_KC)
    edge = GRID_W - NAT_KC
    u = jnp.pad(rpb[:, row_off].astype(F32), ((0, 0), (0, 0), (0, 0), (edge, edge + 1)), mode="edge")
    bias = _toeplitz(u, GRID_W)[..., GRID_W - 1:2 * GRID_W - 1]
    bias = jnp.where(col_mask[None, None, None], bias, NEG_INF)
    bias = bias.transpose(0, 1, 3, 2, 4)
    return bias.reshape(rpb.shape[0], kr, GRID_W, kr * GRID_W)


def _nat_attention(qkv, bias_tab, bsz, seq, ns):
    t = qkv.shape[1]
    rows = seq // GRID_W
    kr = min(NAT_KR_MAX, rows)
    blk = (None, seq, HEAD_DIM)
    return pl.pallas_call(
        functools.partial(_nat_kernel, rows=rows, kr=kr, group=_tile(rows, 8)),
        grid=(ns, bsz),
        in_specs=[pl.BlockSpec(blk, lambda h, b: (h, b, 0)),
                  pl.BlockSpec(blk, lambda h, b: (ns + h, b, 0)),
                  pl.BlockSpec(blk, lambda h, b: (2 * ns + h, b, 0)),
                  pl.BlockSpec((None, kr, GRID_W, kr * GRID_W), lambda h, b: (h, 0, 0, 0))],
        out_specs=pl.BlockSpec(blk, lambda h, b: (h, b, 0)),
        out_shape=jax.ShapeDtypeStruct((ns, t, HEAD_DIM), BF16),
        compiler_params=_params(2),
        name="nat_attn",
    )(qkv, qkv, qkv, bias_tab)


def _t5_bucket(rel):
    nb = N_BUCKETS // 2
    max_exact = nb // 2
    ret = jnp.where(rel > 0, nb, 0)
    n = jnp.abs(rel)
    nf = jnp.maximum(n, 1).astype(F32)
    large = max_exact + (jnp.log(nf / max_exact) / math.log(MAX_DISTANCE / max_exact)
                         * (nb - max_exact)).astype(I32)
    large = jnp.minimum(large, nb - 1)
    return ret + jnp.where(n < max_exact, n, large)


def _t5_bias_strip(rel_table, seq, tq):
    n = 2 * seq
    rel = jnp.arange(n, dtype=I32) - (seq - 1)
    u = rel_table[_t5_bucket(rel)].T.astype(F32) * LOG2_E
    return _toeplitz(u, tq)[:, :, tq - 1:2 * seq - 1]


def _toeplitz(u, rows):
    n = u.shape[-1]
    tiled = jnp.tile(u, (1,) * (u.ndim - 1) + (rows,))[..., :rows * (n - 1)]
    return tiled.reshape(u.shape[:-1] + (rows, n - 1))


def _diff_kernel(q_ref, k_ref, v_ref, e_ref, lam_ref, sg_ref, o_ref, *, seq, tq, nq, n_split, lambda_init):
    qi = pl.program_id(2)
    m0 = pl.multiple_of((nq - 1 - qi) * tq, tq)
    bias = e_ref[:, pl.ds(m0, seq)]
    lam = lam_ref[...]
    lam_full = (jnp.exp(jnp.sum(lam[0:1] * lam[1:2], axis=-1, keepdims=True))
                - jnp.exp(jnp.sum(lam[2:3] * lam[3:4], axis=-1, keepdims=True)) + lambda_init)

    v = jnp.concatenate([v_ref[0], v_ref[1]], axis=-1)
    half = tq // n_split

    def scores(p, rows):
        s = lax.dot_general(q_ref[p, rows, :], k_ref[p], (((1,), (1,)), ((), ())), preferred_element_type=F32)
        return s + bias[rows]

    def attend(s1, s2):
        e1 = jnp.exp2(s1 - jnp.max(s1, axis=-1, keepdims=True))
        e2 = jnp.exp2(s2 - jnp.max(s2, axis=-1, keepdims=True))
        r1 = 1.0 / jnp.sum(e1, axis=-1, keepdims=True)
        r2 = lam_full / jnp.sum(e2, axis=-1, keepdims=True)
        o = jnp.dot((e1 * r1 - e2 * r2).astype(BF16), v, preferred_element_type=F32)
        ms = jnp.mean(o * o, axis=-1, keepdims=True)
        return o * lax.rsqrt(ms + RMS_EPS) * sg_ref[...] * (1.0 - lambda_init)

    blocks = [slice(j * half, (j + 1) * half) for j in range(n_split)]
    all_scores = [(scores(0, rows), scores(1, rows)) for rows in blocks]
    for rows, (s1, s2) in zip(blocks, all_scores):
        o = attend(s1, s2)
        o_ref[0, rows, :] = o[:, :HEAD_DIM].astype(BF16)
        o_ref[1, rows, :] = o[:, HEAD_DIM:].astype(BF16)


def _diff_attention(qkv, strip, lam, sub_g, lambda_init, bsz, seq, ns):
    t = qkv.shape[1]
    hd = ns // 2
    tq = strip.shape[1]
    nq = seq // tq
    qb, kb, vb = 3 * ns // 2, 4 * ns // 2, 5 * ns // 2
    return pl.pallas_call(
        functools.partial(_diff_kernel, seq=seq, tq=tq, nq=nq, n_split=2, lambda_init=lambda_init),
        grid=(hd, bsz, nq),
        in_specs=[pl.BlockSpec((2, tq, HEAD_DIM), lambda h, b, i: (qb + h, b * nq + i, 0)),
                  pl.BlockSpec((2, seq, HEAD_DIM), lambda h, b, i: (kb + h, b, 0)),
                  pl.BlockSpec((2, seq, HEAD_DIM), lambda h, b, i: (vb + h, b, 0)),
                  pl.BlockSpec((None, tq, 2 * seq - tq), lambda h, b, i: (h, 0, 0)),
                  pl.BlockSpec((4, HEAD_DIM), lambda h, b, i: (0, 0)),
                  pl.BlockSpec((1, 2 * HEAD_DIM), lambda h, b, i: (0, 0))],
        out_specs=pl.BlockSpec((2, tq, HEAD_DIM), lambda h, b, i: (h, b * nq + i, 0)),
        out_shape=jax.ShapeDtypeStruct((ns, t, HEAD_DIM), BF16),
        compiler_params=_params(3),
        name="diff_attn",
    )(qkv, qkv, qkv, strip, lam, sub_g.reshape(1, 2 * HEAD_DIM))


def _rows_shape(rows, d):
    return (rows * (d // LANES), LANES)


def _split_bf16(v):
    hi = v.astype(BF16)
    lo = (v - hi.astype(F32)).astype(BF16)
    return hi, lo


def _outproj_kernel(nat_ref, dif_ref, x_ref, mod_ref, w_ref, g_ref, rw_ref, rb_ref,
                    x1_ref, h2_ref, lt_ref):
    ns = nat_ref.shape[0]
    a = jnp.concatenate([nat_ref[hh] for hh in range(ns)] + [dif_ref[hh] for hh in range(ns)], axis=-1)
    mix = jnp.dot(a, w_ref[...].astype(BF16), preferred_element_type=F32)
    m = mod_ref[...]
    x1 = x_ref[...] + m[2:3] * mix
    x1_ref[...] = x1
    ms = jnp.mean(x1 * x1, axis=-1, keepdims=True)
    h2 = x1 * lax.rsqrt(ms + RMS_EPS) * g_ref[...] * (1.0 + m[4:5]) + m[3:4]
    n_chunks = h2.shape[1] // LANES
    for ch in range(n_chunks):
        h2_ref[pl.ds(ch, h2.shape[0], stride=n_chunks), :] = h2[:, ch * LANES:(ch + 1) * LANES]
    h_hi, h_lo = _split_bf16(h2)
    w_hi, w_lo = _split_bf16(rw_ref[...])
    lg = (jnp.dot(h_hi, w_hi, preferred_element_type=F32)
          + jnp.dot(h_lo, w_hi, preferred_element_type=F32)
          + jnp.dot(h_hi, w_lo, preferred_element_type=F32))
    n_exp = lt_ref.shape[0]
    lt_ref[...] = lg.T[:n_exp] + rb_ref[...]


def _outproj(nat, dif, xf, mod3, w, g2, rw_pad, rb, seq):
    t, d = xf.shape
    ns = nat.shape[0]
    n_exp = rb.shape[0]
    tm = _tile(seq, 512)
    per_b = seq // tm
    return pl.pallas_call(
        _outproj_kernel,
        grid=(t // tm,),
        in_specs=[pl.BlockSpec((ns, tm, HEAD_DIM), lambda i: (0, i, 0)),
                  pl.BlockSpec((ns, tm, HEAD_DIM), lambda i: (0, i, 0)),
                  pl.BlockSpec((tm, d), lambda i: (i, 0)),
                  pl.BlockSpec((None, 6, d), lambda i: (i // per_b, 0, 0)),
                  pl.BlockSpec((d, d), lambda i: (0, 0), pipeline_mode=pl.Buffered(1)),
                  pl.BlockSpec((1, d), lambda i: (0, 0)),
                  pl.BlockSpec((d, LANES), lambda i: (0, 0)),
                  pl.BlockSpec((n_exp, 1), lambda i: (0, 0))],
        out_specs=[pl.BlockSpec((tm, d), lambda i: (i, 0)),
                   pl.BlockSpec((tm * (d // LANES), LANES), lambda i: (i, 0)),
                   pl.BlockSpec((n_exp, tm), lambda i: (0, i))],
        out_shape=[jax.ShapeDtypeStruct((t, d), F32),
                   jax.ShapeDtypeStruct(_rows_shape(t, d), F32),
                   jax.ShapeDtypeStruct((n_exp, t), F32)],
        compiler_params=_params(1),
        name="outproj",
    )(nat, dif, xf, mod3, w, g2.reshape(1, d), rw_pad, rb.reshape(n_exp, 1))


def _route_kernel(lt_ref, idx_ref, gate_ref, pos_ref, tmeta_ref, emeta_ref, rank_scr, *, tm, tb):
    n_exp, t = lt_ref.shape
    ntp = tmeta_ref.shape[1]
    eidx = lax.broadcasted_iota(I32, (n_exp, tb), 0)
    tri = (lax.broadcasted_iota(I32, (tb, tb), 0) < lax.broadcasted_iota(I32, (tb, tb), 1)).astype(BF16)

    def pass1(jb, counts):
        off = pl.multiple_of(jb * tb, tb)
        l = lt_ref[:, pl.ds(off, tb)]
        vals, sels, hots = [], [], []
        for _ in range(TOP_K):
            m = jnp.max(l, axis=0, keepdims=True)
            sel = jnp.min(jnp.where(l == m, eidx, n_exp), axis=0, keepdims=True)
            hot = eidx == sel
            vals.append(m)
            sels.append(sel)
            hots.append(hot)
            l = jnp.where(hot, -jnp.inf, l)
        exps = [jnp.exp(v - vals[0]) for v in vals]
        den = exps[0]
        for e in exps[1:]:
            den = den + e
        member = hots[0].astype(F32)
        for hot in hots[1:]:
            member = member + hot.astype(F32)
        before = jnp.dot(member.astype(BF16), tri, preferred_element_type=F32) + counts
        for k in range(TOP_K):
            rank = jnp.sum(jnp.where(hots[k], before, 0.0), axis=0, keepdims=True)
            idx_ref[pl.ds(k, 1), pl.ds(off, tb)] = sels[k]
            gate_ref[pl.ds(k, 1), pl.ds(off, tb)] = exps[k] / den
            rank_scr[pl.ds(k, 1), pl.ds(off, tb)] = rank.astype(I32)
        return counts + jnp.sum(member, axis=1, keepdims=True)

    counts = lax.fori_loop(0, t // tb, pass1, jnp.zeros((n_exp, 1), F32))

    padded = jnp.ceil(counts * (1.0 / tm)) * tm
    er = lax.broadcasted_iota(I32, (n_exp, LANES), 0)
    ec = lax.broadcasted_iota(I32, (n_exp, LANES), 1)

    def to_lanes(col):
        return jnp.sum(jnp.where(er == ec, col, 0.0), axis=0, keepdims=True)

    start = jnp.sum(jnp.where(ec < er, to_lanes(padded), 0.0), axis=1, keepdims=True)
    end = start + padded

    def pass2(jb, carry):
        off = pl.multiple_of(jb * tb, tb)
        for k in range(TOP_K):
            sel = idx_ref[pl.ds(k, 1), pl.ds(off, tb)]
            st = jnp.sum(jnp.where(eidx == sel, start, 0.0), axis=0, keepdims=True)
            pos_ref[pl.ds(k, 1), pl.ds(off, tb)] = st.astype(I32) + rank_scr[pl.ds(k, 1), pl.ds(off, tb)]
        return carry

    lax.fori_loop(0, t // tb, pass2, 0)

    tile_row = (lax.broadcasted_iota(I32, (n_exp, ntp), 1) * tm).astype(F32)
    tile_exp = jnp.sum((end <= tile_row).astype(F32), axis=0, keepdims=True)
    tile_exp = jnp.minimum(tile_exp, n_exp - 1.0)
    n_used = jnp.sum(padded, axis=0, keepdims=True) * (1.0 / tm)
    tmeta_ref[...] = jnp.zeros(tmeta_ref.shape, I32)
    tmeta_ref[0:1, :] = tile_exp.astype(I32)
    tmeta_ref[1:2, :] = jnp.broadcast_to(n_used, (1, ntp)).astype(I32)
    emeta_ref[...] = jnp.zeros(emeta_ref.shape, I32)
    emeta_ref[0:1, :] = to_lanes(start).astype(I32)
    emeta_ref[1:2, :] = to_lanes(counts).astype(I32)
    emeta_ref[2:3, :] = to_lanes(padded).astype(I32)


def _route(logits_t, tm, n_tiles):
    n_exp, t = logits_t.shape
    tb = _tile(t, 512)
    ntp = pl.cdiv(n_tiles, LANES) * LANES
    full = lambda shape: pl.BlockSpec(shape, lambda: (0,) * len(shape))
    return pl.pallas_call(
        functools.partial(_route_kernel, tm=tm, tb=tb),
        in_specs=[full((n_exp, t))],
        out_specs=[full((TOP_K, t)), full((TOP_K, t)), full((TOP_K, t)), full((8, ntp)), full((8, LANES))],
        out_shape=[jax.ShapeDtypeStruct((TOP_K, t), I32),
                   jax.ShapeDtypeStruct((TOP_K, t), F32),
                   jax.ShapeDtypeStruct((TOP_K, t), I32),
                   jax.ShapeDtypeStruct((8, ntp), I32),
                   jax.ShapeDtypeStruct((8, LANES), I32)],
        scratch_shapes=[pltpu.VMEM((TOP_K, t), I32)],
        compiler_params=pltpu.CompilerParams(vmem_limit_bytes=V7X_VMEM_LIMIT_BYTES),
        name="route",
    )(logits_t)


def _dispatch_kernel(estart_ref, ecnt_ref, nused_ref, pos_hbm, h_ref, xs_hbm, pos_smem, zero_scr,
                     sem_idx, sem_row, sem_zero, *, tm, nch):
    i = pl.program_id(0)
    tmd = h_ref.shape[0] // nch
    n_exp = estart_ref.shape[0]
    n_tiles = xs_hbm.shape[0] // (tm * nch)

    idx_copy = pltpu.make_async_copy(pos_hbm.at[i], pos_smem, sem_idx)
    idx_copy.start()

    @pl.when(i == 0)
    def _():
        zero_scr[...] = jnp.zeros(zero_scr.shape, F32)

        def fill(e, carry):
            cnt = ecnt_ref[e]
            pad = (tm - cnt % tm) % tm
            base = estart_ref[e] + cnt
            size = tm // 2
            while size >= 1:
                off = pad & ~(2 * size - 1)

                @pl.when((pad & size) != 0)
                def _(size=size, off=off):
                    dst0 = pl.multiple_of((base + off) * nch, nch)
                    cp = pltpu.make_async_copy(zero_scr.at[pl.ds(0, size * nch)],
                                               xs_hbm.at[pl.ds(dst0, size * nch)], sem_zero)
                    cp.start()
                    cp.wait()
                size //= 2
            return carry

        lax.fori_loop(0, n_exp, fill, 0)

        def tail_copy(j, part):
            dst0 = pl.multiple_of((j * tm + part * (tm // 2)) * nch, nch)
            return pltpu.make_async_copy(zero_scr, xs_hbm.at[pl.ds(dst0, tm // 2 * nch)], sem_zero)

        def tail_start(j, carry):
            tail_copy(j, 0).start()
            tail_copy(j, 1).start()
            return carry

        def tail_wait(j, carry):
            tail_copy(j, 0).wait()
            tail_copy(j, 1).wait()
            return carry

        lax.fori_loop(nused_ref[0], n_tiles, tail_start, 0)
        lax.fori_loop(nused_ref[0], n_tiles, tail_wait, 0)

    idx_copy.wait()

    def issue(tok, carry):
        src0 = pl.multiple_of(tok * nch, nch)
        for k in range(TOP_K):
            dst0 = pl.multiple_of(pos_smem[k * tmd + tok] * nch, nch)
            pltpu.make_async_copy(h_ref.at[pl.ds(src0, nch)], xs_hbm.at[pl.ds(dst0, nch)],
                                  sem_row).start(priority=k % 2)
        return carry

    lax.fori_loop(0, tmd, issue, 0, unroll=4)
    for _ in range(TOP_K):
        pltpu.make_async_copy(h_ref, xs_hbm.at[pl.ds(0, tmd * nch)], sem_row).wait()


def _dispatch(h2, pos_tiles, estart, ecnt, n_used, tm, n_tiles, nch):
    ntt, per_tile = pos_tiles.shape
    tmd = per_tile // TOP_K
    grid_spec = pltpu.PrefetchScalarGridSpec(
        num_scalar_prefetch=3,
        grid=(ntt,),
        in_specs=[pl.BlockSpec(memory_space=pl.ANY),
                  pl.BlockSpec((tmd * nch, LANES), lambda i, es, ec, nu: (i, 0))],
        out_specs=pl.BlockSpec(memory_space=pl.ANY),
        scratch_shapes=[pltpu.SMEM((per_tile,), I32),
                        pltpu.VMEM((tm // 2 * nch, LANES), F32),
                        pltpu.SemaphoreType.DMA, pltpu.SemaphoreType.DMA, pltpu.SemaphoreType.DMA],
    )
    return pl.pallas_call(
        functools.partial(_dispatch_kernel, tm=tm, nch=nch),
        grid_spec=grid_spec,
        out_shape=jax.ShapeDtypeStruct((n_tiles * tm * nch, LANES), F32),
        compiler_params=_params(1),
        name="dispatch",
    )(estart, ecnt, n_used, pos_tiles, h2)


GATE_UP_SPLIT = 2


def _tile_loop(n, in_copy, out_copy, compute):
    @pl.when(n > 0)
    def _():
        in_copy(0, 0).start()

    def body(j, carry):
        slot = j % 2

        @pl.when(j + 1 < n)
        def _():
            in_copy(j + 1, 1 - slot).start()

        in_copy(j, slot).wait()

        @pl.when(j >= 2)
        def _():
            out_copy(j - 2, slot).wait()

        compute(slot)
        out_copy(j, slot).start()
        return carry

    lax.fori_loop(0, n, body, 0)

    @pl.when(n >= 2)
    def _():
        out_copy(n - 2, n % 2).wait()

    @pl.when(n >= 1)
    def _():
        out_copy(n - 1, (n - 1) % 2).wait()


def _zero_tail(first, n_tiles, copy):
    def start(j, carry):
        copy(j).start()
        return carry

    def wait(j, carry):
        copy(j).wait()
        return carry

    lax.fori_loop(first, n_tiles, start, 0)
    lax.fori_loop(first, n_tiles, wait, 0)


def _gate_up_kernel(first_ref, count_ref, nused_ref, xs_hbm, wg_ref, wu_ref, bg_ref, bu_ref, act_hbm,
                    xbuf, abuf, sem_in, sem_out, *, tm, chunk, nch, n_tiles):
    e, h = pl.program_id(0), pl.program_id(1)
    ffh = abuf.shape[2]
    first = first_ref[e]

    def in_copy(j, slot):
        r0 = pl.multiple_of((first + j) * (tm * nch), tm * nch)
        return pltpu.make_async_copy(xs_hbm.at[pl.ds(r0, tm * nch)], xbuf.at[slot], sem_in.at[slot])

    def act_tile(j, half):
        return act_hbm.at[pl.ds(pl.multiple_of(j * tm, tm), tm), pl.ds(pl.multiple_of(half * ffh, ffh), ffh)]

    def out_copy(j, slot):
        return pltpu.make_async_copy(abuf.at[slot], act_tile(first + j, h), sem_out.at[slot])

    def compute(slot):
        x = jnp.concatenate([xbuf[slot, pl.ds(ch, tm, stride=nch), :] for ch in range(nch)], axis=-1).astype(BF16)
        for c0 in range(0, ffh, chunk):
            cols = slice(c0, c0 + chunk)
            gate = jnp.dot(x, wg_ref[:, cols].astype(BF16), preferred_element_type=F32) + bg_ref[:, cols]
            up = jnp.dot(x, wu_ref[:, cols].astype(BF16), preferred_element_type=F32) + bu_ref[:, cols]
            gate = jnp.minimum(gate, SWIGLU_LIMIT)
            up = jnp.clip(up, -SWIGLU_LIMIT, SWIGLU_LIMIT)
            act = (up + 1.0) * gate * jax.nn.sigmoid(gate * SWIGLU_ALPHA)
            abuf[slot, :, cols] = act.astype(BF16)

    _tile_loop(count_ref[e], in_copy, out_copy, compute)

    @pl.when((e == pl.num_programs(0) - 1) & (h == pl.num_programs(1) - 1))
    def _():
        abuf[0] = jnp.zeros(abuf.shape[1:], BF16)
        for half in range(GATE_UP_SPLIT):
            _zero_tail(nused_ref[0], n_tiles,
                       lambda j, half=half: pltpu.make_async_copy(abuf.at[0], act_tile(j, half), sem_out.at[0]))


def _gate_up(xs, w, b, first_tile, group_tiles, n_used, tm):
    n_exp, d, ff2 = w.shape
    nch = d // LANES
    rows = xs.shape[0] // nch
    ff = ff2 // 2
    n_tiles = rows // tm
    ns = GATE_UP_SPLIT
    ffh = ff // ns
    b3 = b.reshape(n_exp, 1, ff2)
    grid_spec = pltpu.PrefetchScalarGridSpec(
        num_scalar_prefetch=3,
        grid=(n_exp, ns),
        in_specs=[pl.BlockSpec(memory_space=pl.ANY),
                  pl.BlockSpec((None, d, ffh), lambda e, h, *_: (e, 0, h)),
                  pl.BlockSpec((None, d, ffh), lambda e, h, *_: (e, 0, ns + h)),
                  pl.BlockSpec((None, 1, ffh), lambda e, h, *_: (e, 0, h)),
                  pl.BlockSpec((None, 1, ffh), lambda e, h, *_: (e, 0, ns + h))],
        out_specs=pl.BlockSpec(memory_space=pl.ANY),
        scratch_shapes=[pltpu.VMEM((2, tm * nch, LANES), F32),
                        pltpu.VMEM((2, tm, ffh), BF16),
                        pltpu.SemaphoreType.DMA((2,)), pltpu.SemaphoreType.DMA((2,))],
    )
    return pl.pallas_call(
        functools.partial(_gate_up_kernel, tm=tm, chunk=_tile(ffh, 512), nch=nch, n_tiles=n_tiles),
        grid_spec=grid_spec,
        out_shape=jax.ShapeDtypeStruct((rows, ff), BF16),
        compiler_params=_params(2),
        name="gate_up",
    )(first_tile, group_tiles, n_used, xs, w, w, b3, b3)


def _down_kernel(first_ref, count_ref, nused_ref, act_hbm, w_ref, b_ref, y_hbm, abuf, ybuf, sem_in, sem_out,
                 *, tm, nch, n_tiles):
    e = pl.program_id(0)
    first = first_ref[e]
    d = w_ref.shape[1]
    chunk = _tile(d, 512)

    def in_copy(j, slot):
        r0 = pl.multiple_of((first + j) * tm, tm)
        return pltpu.make_async_copy(act_hbm.at[pl.ds(r0, tm)], abuf.at[slot], sem_in.at[slot])

    def y_tile(j):
        return y_hbm.at[pl.ds(pl.multiple_of(j * (tm * nch), tm * nch), tm * nch)]

    def out_copy(j, slot):
        return pltpu.make_async_copy(ybuf.at[slot], y_tile(first + j), sem_out.at[slot])

    def compute(slot):
        a = abuf[slot]
        for c0 in range(0, d, chunk):
            y = (jnp.dot(a, w_ref[:, c0:c0 + chunk].astype(BF16), preferred_element_type=F32)
                 + b_ref[:, c0:c0 + chunk])
            for j in range(chunk // LANES):
                ybuf[slot, pl.ds(c0 // LANES + j, tm, stride=nch), :] = y[:, j * LANES:(j + 1) * LANES]

    _tile_loop(count_ref[e], in_copy, out_copy, compute)

    @pl.when(e == pl.num_programs(0) - 1)
    def _():
        ybuf[0] = jnp.zeros(ybuf.shape[1:], F32)
        _zero_tail(nused_ref[0], n_tiles, lambda j: pltpu.make_async_copy(ybuf.at[0], y_tile(j), sem_out.at[0]))


def _down(act, w, b, first_tile, group_tiles, n_used, tm):
    rows, ff = act.shape
    n_exp, _, d = w.shape
    nch = d // LANES
    n_tiles = rows // tm
    grid_spec = pltpu.PrefetchScalarGridSpec(
        num_scalar_prefetch=3,
        grid=(n_exp,),
        in_specs=[pl.BlockSpec(memory_space=pl.ANY),
                  pl.BlockSpec((None, ff, d), lambda e, *_: (e, 0, 0)),
                  pl.BlockSpec((None, 1, d), lambda e, *_: (e, 0, 0))],
        out_specs=pl.BlockSpec(memory_space=pl.ANY),
        scratch_shapes=[pltpu.VMEM((2, tm, ff), BF16),
                        pltpu.VMEM((2, tm * nch, LANES), F32),
                        pltpu.SemaphoreType.DMA((2,)), pltpu.SemaphoreType.DMA((2,))],
    )
    return pl.pallas_call(
        functools.partial(_down_kernel, tm=tm, nch=nch, n_tiles=n_tiles),
        grid_spec=grid_spec,
        out_shape=jax.ShapeDtypeStruct(_rows_shape(rows, d), F32),
        compiler_params=_params(1),
        name="down",
    )(first_tile, group_tiles, n_used, act, w, b.reshape(n_exp, 1, d))


def _combine_kernel(pos_hbm, y_hbm, x1_ref, mod_ref, g_ref, o_ref, pos_smem, ybuf, sem_idx, sem_row):
    i = pl.program_id(0)
    n = pl.num_programs(0)
    tmc, d = x1_ref.shape
    nch = d // LANES
    slot = i % 2

    def idx_copy(step, s):
        return pltpu.make_async_copy(pos_hbm.at[step], pos_smem.at[s], sem_idx.at[s])

    def start_gathers_static(s):
        def issue(tok, carry):
            dst0 = pl.multiple_of(tok * nch, nch)
            for k in range(TOP_K):
                src0 = pl.multiple_of(pos_smem[s, k * tmc + tok] * nch, nch)
                pltpu.make_async_copy(y_hbm.at[pl.ds(src0, nch)], ybuf.at[s, k, pl.ds(dst0, nch)],
                                      sem_row.at[s]).start(priority=k % 2)
            return carry

        lax.fori_loop(0, tmc, issue, 0, unroll=4)

    def start_gathers(s):
        for static_s in range(2):
            pl.when(s == static_s)(functools.partial(start_gathers_static, static_s))

    @pl.when(i == 0)
    def _():
        idx_copy(0, 0).start()
        idx_copy(0, 0).wait()
        start_gathers_static(0)

        @pl.when(n > 1)
        def _():
            idx_copy(1, 1).start()

    @pl.when(i + 1 < n)
    def _():
        idx_copy(i + 1, 1 - slot).wait()
        start_gathers(1 - slot)

    @pl.when(i + 2 < n)
    def _():
        idx_copy(i + 2, slot).start()

    for k in range(TOP_K):
        pltpu.make_async_copy(y_hbm.at[pl.ds(0, tmc * nch)], ybuf.at[slot, k], sem_row.at[slot]).wait()

    g = g_ref[...]
    gate2 = mod_ref[...][5:6]
    for ch in range(nch):
        cols = slice(ch * LANES, (ch + 1) * LANES)
        moe = ybuf[slot, 0, pl.ds(ch, tmc, stride=nch), :] * g[:, 0:1]
        for k in range(1, TOP_K):
            moe = moe + ybuf[slot, k, pl.ds(ch, tmc, stride=nch), :] * g[:, k:k + 1]
        o_ref[:, cols] = x1_ref[:, cols] + gate2[:, cols] * moe


def _combine(y, pos_tiles, x1, mod3, gates_t, seq):
    t, d = x1.shape
    ntt, per_tile = pos_tiles.shape
    tmc = per_tile // TOP_K
    per_b = seq // tmc
    return pl.pallas_call(
        _combine_kernel,
        grid=(ntt,),
        in_specs=[pl.BlockSpec(memory_space=pl.ANY),
                  pl.BlockSpec(memory_space=pl.ANY),
                  pl.BlockSpec((tmc, d), lambda i: (i, 0)),
                  pl.BlockSpec((None, 6, d), lambda i: (i // per_b, 0, 0)),
                  pl.BlockSpec((tmc, TOP_K), lambda i: (i, 0))],
        out_specs=pl.BlockSpec((tmc, d), lambda i: (i, 0)),
        out_shape=jax.ShapeDtypeStruct((t, d), F32),
        scratch_shapes=[pltpu.SMEM((2, per_tile), I32),
                        pltpu.VMEM((2, TOP_K, tmc * (d // LANES), LANES), F32),
                        pltpu.SemaphoreType.DMA((2,)), pltpu.SemaphoreType.DMA((2,))],
        compiler_params=_params(1),
        name="combine",
    )(pos_tiles, y, x1, mod3, gates_t)


def _moe(h2, logits_t, x1, mod3, w_gu, b_gu, w_dn, b_dn, seq):
    t, d = x1.shape
    n_exp = logits_t.shape[0]
    tm = _tile(t * TOP_K, 256)
    n_tiles = (t * TOP_K) // tm + n_exp
    idx, gates, pos, tmeta, emeta = _route(logits_t, tm, n_tiles)
    del idx
    tok_tile = _tile(seq, 256)
    pos_tiles = (pos.reshape(TOP_K, t // tok_tile, tok_tile).transpose(1, 0, 2)
                 .reshape(t // tok_tile, TOP_K * tok_tile))
    n_used = tmeta[1, :1]
    xs = _dispatch(h2, pos_tiles, emeta[0, :n_exp], emeta[1, :n_exp], n_used, tm, n_tiles, d // LANES)
    first_tile, group_tiles = emeta[0, :n_exp] // tm, emeta[2, :n_exp] // tm
    act = _gate_up(xs, w_gu, b_gu, first_tile, group_tiles, n_used, tm)
    y = _down(act, w_dn, b_dn, first_tile, group_tiles, n_used, tm)
    return _combine(y, pos_tiles, x1, mod3, gates.T, seq)


def kernel(x, c, w_ada, b_ada, norm1_g, w_in, nat_q_g, nat_k_g, nat_rpb, diff_q_g, diff_k_g, diff_lambda,
           diff_sub_g, rel_bias_table, w_out, norm2_g, router_w, router_b, w_gate_up, b_gate_up, w_down, b_down):
    bsz, seq, d = x.shape
    t = bsz * seq
    ns = (d // 2) // HEAD_DIM
    n_exp = router_w.shape[-1]
    scale = HEAD_DIM ** -0.5
    rows = seq // GRID_W
    kr = min(NAT_KR_MAX, rows)
    strip = _t5_bias_strip(rel_bias_table, seq, _tile(seq, 256))
    ones = jnp.ones((HEAD_DIM,), F32)
    xf = x.reshape(t, d)
    for l in range(w_ada.shape[0]):
        lambda_init = 0.8 - 0.6 * math.exp(-0.3 * l)
        mod3 = _adaln(c, w_ada[l], b_ada[l]).reshape(bsz, 6, d)
        gains = jnp.stack([nat_q_g[l] * scale, nat_k_g[l], ones, diff_q_g[l] * (scale * LOG2_E), diff_k_g[l],
                           ones, ones, ones])
        qkv = _inproj(xf, mod3, norm1_g[l], w_in[l], gains, seq)
        nat = _nat_attention(qkv, _nat_bias_table(nat_rpb[l], kr), bsz, seq, ns)
        dif = _diff_attention(qkv, strip, diff_lambda[l], diff_sub_g[l], lambda_init, bsz, seq, ns)
        rw_pad = jnp.pad(router_w[l], ((0, 0), (0, LANES - n_exp)))
        x1, h2, logits_t = _outproj(nat, dif, xf, mod3, w_out[l], norm2_g[l], rw_pad, router_b[l], seq)
        xf = _moe(h2, logits_t, x1, mod3, w_gate_up[l], b_gate_up[l], w_down[l], b_down[l], seq)
    return xf.reshape(bsz, seq, d)
```

```python
import functools
import math

import jax
import jax.numpy as jnp
from jax import lax
from jax.experimental import pallas as pl
from jax.experimental.pallas import tpu as pltpu

HEAD_DIM = 128
GRID_W = 64
NAT_KR_MAX = 8
NAT_KC = 16
N_BUCKETS = 32
MAX_DISTANCE = 128
TOP_K = 4
SWIGLU_LIMIT = 7.0
SWIGLU_ALPHA = 1.702
RMS_EPS = 1e-6
NEG_INF = -1e30
LOG2_E = 1.4426950408889634

LANES = 128
V7X_VMEM_LIMIT_BYTES = 56 * 1024 * 1024

F32 = jnp.float32
BF16 = jnp.bfloat16
I32 = jnp.int32


def _tile(n, pref):
    t = min(n, pref)
    assert n % t == 0, (n, pref)
    return t


def _params(n_axes):
    return pltpu.CompilerParams(dimension_semantics=("arbitrary",) * n_axes,
                                vmem_limit_bytes=V7X_VMEM_LIMIT_BYTES)


def _adaln_kernel(c_ref, w_ref, b_ref, o_ref):
    c = c_ref[...]
    a = (c * jax.nn.sigmoid(c)).astype(BF16)
    o_ref[...] = jnp.dot(a, w_ref[...].astype(BF16), preferred_element_type=F32) + b_ref[...]


def _adaln(c, w, b):
    bsz, d = c.shape
    n = w.shape[1]
    tn = _tile(n, 1024)
    return pl.pallas_call(
        _adaln_kernel,
        grid=(n // tn,),
        in_specs=[pl.BlockSpec((bsz, d), lambda j: (0, 0)),
                  pl.BlockSpec((d, tn), lambda j: (0, j)),
                  pl.BlockSpec((1, tn), lambda j: (0, j))],
        out_specs=pl.BlockSpec((bsz, tn), lambda j: (0, j)),
        out_shape=jax.ShapeDtypeStruct((bsz, n), F32),
        compiler_params=_params(1),
        name="adaln",
    )(c, w, b.reshape(1, n))


def _inproj_kernel(x_ref, mod_ref, g_ref, w_ref, gain_ref, o_ref, h_scr):
    j = pl.program_id(1)

    @pl.when(j == 0)
    def _():
        x = x_ref[...]
        ms = jnp.mean(x * x, axis=-1, keepdims=True)
        y = x * lax.rsqrt(ms + RMS_EPS) * g_ref[...]
        m = mod_ref[...]
        h_scr[...] = (y * (1.0 + m[1:2]) + m[0:1]).astype(BF16)

    y = jnp.dot(h_scr[...], w_ref[...].astype(BF16), preferred_element_type=F32)
    heads = o_ref.shape[0]
    is_value = (j == 2) | (j == 5)

    @pl.when(is_value)
    def _():
        for hh in range(heads):
            o_ref[hh] = y[:, hh * HEAD_DIM:(hh + 1) * HEAD_DIM].astype(BF16)

    @pl.when(jnp.logical_not(is_value))
    def _():
        g = gain_ref[pl.ds(j, 1), :]
        for hh in range(heads):
            yh = y[:, hh * HEAD_DIM:(hh + 1) * HEAD_DIM]
            ms = jnp.mean(yh * yh, axis=-1, keepdims=True)
            o_ref[hh] = (yh * lax.rsqrt(ms + RMS_EPS) * g).astype(BF16)


def _inproj(xf, mod3, g1, w, gains, seq):
    t, d = xf.shape
    seg = d // 2
    ns = seg // HEAD_DIM
    tm = _tile(seq, 1024)
    per_b = seq // tm
    return pl.pallas_call(
        _inproj_kernel,
        grid=(t // tm, 6),
        in_specs=[pl.BlockSpec((tm, d), lambda i, j: (i, 0)),
                  pl.BlockSpec((None, 6, d), lambda i, j: (i // per_b, 0, 0)),
                  pl.BlockSpec((1, d), lambda i, j: (0, 0)),
                  pl.BlockSpec((d, seg), lambda i, j: (0, j)),
                  pl.BlockSpec((8, HEAD_DIM), lambda i, j: (0, 0))],
        out_specs=pl.BlockSpec((ns, tm, HEAD_DIM), lambda i, j: (j, i, 0)),
        out_shape=jax.ShapeDtypeStruct((6 * ns, t, HEAD_DIM), BF16),
        scratch_shapes=[pltpu.VMEM((tm, d), BF16)],
        compiler_params=_params(2),
        name="inproj",
    )(xf, mod3, g1.reshape(1, d), w, gains)


def _nat_kernel(q_ref, k_ref, v_ref, b_ref, o_ref, *, rows, kr, group):
    band = kr * GRID_W

    def body(g, carry):
        k0s, scores = [], []
        for i in range(group):
            r = g * group + i
            rs = jnp.clip(r - kr // 2, 0, rows - kr)
            k0 = pl.multiple_of(rs * GRID_W, GRID_W)
            q = q_ref[pl.ds(pl.multiple_of(r * GRID_W, GRID_W), GRID_W), :]
            s = lax.dot_general(q, k_ref[pl.ds(k0, band), :], (((1,), (1,)), ((), ())),
                                preferred_element_type=F32)
            k0s.append(k0)
            scores.append(s + b_ref[r - rs])
        s = jnp.concatenate(scores, axis=0)
        p = jnp.exp(s - jnp.max(s, axis=-1, keepdims=True))
        den = jnp.sum(p, axis=-1, keepdims=True)
        pb = p.astype(BF16)
        outs = [jnp.dot(pb[i * GRID_W:(i + 1) * GRID_W], v_ref[pl.ds(k0s[i], band), :],
                        preferred_element_type=F32) for i in range(group)]
        o = jnp.concatenate(outs, axis=0) / den
        o_ref[pl.ds(pl.multiple_of(g * (group * GRID_W), group * GRID_W), group * GRID_W), :] = o.astype(BF16)
        return carry

    lax.fori_loop(0, rows // group, body, 0)


def _nat_bias_table(rpb, kr):
    j = jnp.arange(kr)
    row_off = j[None, :] - j[:, None] + (NAT_KR_MAX - 1)
    col = jnp.arange(GRID_W)
    col_start = jnp.clip(col - NAT_KC // 2, 0, GRID_W - NAT_KC)
    col_mask = (col[None, :] >= col_start[:, None]) & (col[None, :] < col_start[:, None] + NAT_KC)
    edge = GRID_W - NAT_KC
    u = jnp.pad(rpb[:, row_off].astype(F32), ((0, 0), (0, 0), (0, 0), (edge, edge + 1)), mode="edge")
    bias = _toeplitz(u, GRID_W)[..., GRID_W - 1:2 * GRID_W - 1]
    bias = jnp.where(col_mask[None, None, None], bias, NEG_INF)
    bias = bias.transpose(0, 1, 3, 2, 4)
    return bias.reshape(rpb.shape[0], kr, GRID_W, kr * GRID_W)


def _nat_attention(qkv, bias_tab, bsz, seq, ns):
    t = qkv.shape[1]
    rows = seq // GRID_W
    kr = min(NAT_KR_MAX, rows)
    blk = (None, seq, HEAD_DIM)
    return pl.pallas_call(
        functools.partial(_nat_kernel, rows=rows, kr=kr, group=_tile(rows, 8)),
        grid=(ns, bsz),
        in_specs=[pl.BlockSpec(blk, lambda h, b: (h, b, 0)),
                  pl.BlockSpec(blk, lambda h, b: (ns + h, b, 0)),
                  pl.BlockSpec(blk, lambda h, b: (2 * ns + h, b, 0)),
                  pl.BlockSpec((None, kr, GRID_W, kr * GRID_W), lambda h, b: (h, 0, 0, 0))],
        out_specs=pl.BlockSpec(blk, lambda h, b: (h, b, 0)),
        out_shape=jax.ShapeDtypeStruct((ns, t, HEAD_DIM), BF16),
        compiler_params=_params(2),
        name="nat_attn",
    )(qkv, qkv, qkv, bias_tab)


def _t5_bucket(rel):
    nb = N_BUCKETS // 2
    max_exact = nb // 2
    ret = jnp.where(rel > 0, nb, 0)
    n = jnp.abs(rel)
    nf = jnp.maximum(n, 1).astype(F32)
    large = max_exact + (jnp.log(nf / max_exact) / math.log(MAX_DISTANCE / max_exact)
                         * (nb - max_exact)).astype(I32)
    large = jnp.minimum(large, nb - 1)
    return ret + jnp.where(n < max_exact, n, large)


def _t5_bias_by_offset(rel_table, seq):
    rel = jnp.arange(2 * seq, dtype=I32) - seq
    u = rel_table[_t5_bucket(rel)].T.astype(F32) * LOG2_E
    return u.reshape(u.shape[0], 1, 2 * seq)


def _toeplitz(u, rows):
    n = u.shape[-1]
    tiled = jnp.tile(u, (1,) * (u.ndim - 1) + (rows,))[..., :rows * (n - 1)]
    return tiled.reshape(u.shape[:-1] + (rows, n - 1))


def _diff_kernel(q_ref, k_ref, v_ref, u_ref, lam_ref, sg_ref, o_ref, strip_scr, *, seq, tq, nq, n_split,
                 lambda_init):
    qi = pl.program_id(2)

    @pl.when((pl.program_id(1) == 0) & (qi == 0))
    def _():
        strip_scr[...] = pltpu.roll(jnp.broadcast_to(u_ref[...], strip_scr.shape), 0, 1, stride=1, stride_axis=0)

    bias = strip_scr[:, pl.ds(pl.multiple_of((nq - qi) * tq, tq), seq)]
    lam = lam_ref[...]
    lam_full = (jnp.exp(jnp.sum(lam[0:1] * lam[1:2], axis=-1, keepdims=True))
                - jnp.exp(jnp.sum(lam[2:3] * lam[3:4], axis=-1, keepdims=True)) + lambda_init)

    v = jnp.concatenate([v_ref[0], v_ref[1]], axis=-1)
    half = tq // n_split

    def scores(p, rows):
        s = lax.dot_general(q_ref[p, rows, :], k_ref[p], (((1,), (1,)), ((), ())), preferred_element_type=F32)
        return s + bias[rows]

    def attend(s1, s2):
        e1 = jnp.exp2(s1 - jnp.max(s1, axis=-1, keepdims=True))
        e2 = jnp.exp2(s2 - jnp.max(s2, axis=-1, keepdims=True))
        r1 = 1.0 / jnp.sum(e1, axis=-1, keepdims=True)
        r2 = lam_full / jnp.sum(e2, axis=-1, keepdims=True)
        o = jnp.dot((e1 * r1 - e2 * r2).astype(BF16), v, preferred_element_type=F32)
        ms = jnp.mean(o * o, axis=-1, keepdims=True)
        return o * lax.rsqrt(ms + RMS_EPS) * sg_ref[...] * (1.0 - lambda_init)

    blocks = [slice(j * half, (j + 1) * half) for j in range(n_split)]
    all_scores = [(scores(0, rows), scores(1, rows)) for rows in blocks]
    for rows, (s1, s2) in zip(blocks, all_scores):
        o = attend(s1, s2)
        o_ref[0, rows, :] = o[:, :HEAD_DIM].astype(BF16)
        o_ref[1, rows, :] = o[:, HEAD_DIM:].astype(BF16)


def _diff_attention(qkv, bias_by_offset, lam, sub_g, lambda_init, bsz, seq, ns):
    t = qkv.shape[1]
    hd = ns // 2
    tq = _tile(seq, 256)
    nq = seq // tq
    qb, kb, vb = 3 * ns // 2, 4 * ns // 2, 5 * ns // 2
    return pl.pallas_call(
        functools.partial(_diff_kernel, seq=seq, tq=tq, nq=nq, n_split=2, lambda_init=lambda_init),
        grid=(hd, bsz, nq),
        in_specs=[pl.BlockSpec((2, tq, HEAD_DIM), lambda h, b, i: (qb + h, b * nq + i, 0)),
                  pl.BlockSpec((2, seq, HEAD_DIM), lambda h, b, i: (kb + h, b, 0)),
                  pl.BlockSpec((2, seq, HEAD_DIM), lambda h, b, i: (vb + h, b, 0)),
                  pl.BlockSpec((None, 1, 2 * seq), lambda h, b, i: (h, 0, 0)),
                  pl.BlockSpec((4, HEAD_DIM), lambda h, b, i: (0, 0)),
                  pl.BlockSpec((1, 2 * HEAD_DIM), lambda h, b, i: (0, 0))],
        out_specs=pl.BlockSpec((2, tq, HEAD_DIM), lambda h, b, i: (h, b * nq + i, 0)),
        out_shape=jax.ShapeDtypeStruct((ns, t, HEAD_DIM), BF16),
        scratch_shapes=[pltpu.VMEM((tq, 2 * seq), F32)],
        compiler_params=_params(3),
        name="diff_attn",
    )(qkv, qkv, qkv, bias_by_offset, lam, sub_g.reshape(1, 2 * HEAD_DIM))


def _rows_shape(rows, d):
    return (rows * (d // LANES), LANES)


def _split_bf16(v):
    hi = v.astype(BF16)
    lo = (v - hi.astype(F32)).astype(BF16)
    return hi, lo


def _outproj_kernel(nat_ref, dif_ref, x_ref, mod_ref, w_ref, g_ref, rw_ref, rb_ref,
                    x1_ref, h2_ref, lt_ref):
    ns = nat_ref.shape[0]
    a = jnp.concatenate([nat_ref[hh] for hh in range(ns)] + [dif_ref[hh] for hh in range(ns)], axis=-1)
    mix = jnp.dot(a, w_ref[...].astype(BF16), preferred_element_type=F32)
    m = mod_ref[...]
    x1 = x_ref[...] + m[2:3] * mix
    x1_ref[...] = x1
    ms = jnp.mean(x1 * x1, axis=-1, keepdims=True)
    h2 = x1 * lax.rsqrt(ms + RMS_EPS) * g_ref[...] * (1.0 + m[4:5]) + m[3:4]
    n_chunks = h2.shape[1] // LANES
    for ch in range(n_chunks):
        h2_ref[pl.ds(ch, h2.shape[0], stride=n_chunks), :] = h2[:, ch * LANES:(ch + 1) * LANES]
    h_hi, h_lo = _split_bf16(h2)
    w_hi, w_lo = _split_bf16(rw_ref[...])
    lg = (jnp.dot(h_hi, w_hi, preferred_element_type=F32)
          + jnp.dot(h_lo, w_hi, preferred_element_type=F32)
          + jnp.dot(h_hi, w_lo, preferred_element_type=F32))
    n_exp = lt_ref.shape[0]
    lt_ref[...] = lg.T[:n_exp] + rb_ref[...]


def _outproj(nat, dif, xf, mod3, w, g2, rw_pad, rb, seq):
    t, d = xf.shape
    ns = nat.shape[0]
    n_exp = rb.shape[0]
    tm = _tile(seq, 512)
    per_b = seq // tm
    return pl.pallas_call(
        _outproj_kernel,
        grid=(t // tm,),
        in_specs=[pl.BlockSpec((ns, tm, HEAD_DIM), lambda i: (0, i, 0)),
                  pl.BlockSpec((ns, tm, HEAD_DIM), lambda i: (0, i, 0)),
                  pl.BlockSpec((tm, d), lambda i: (i, 0)),
                  pl.BlockSpec((None, 6, d), lambda i: (i // per_b, 0, 0)),
                  pl.BlockSpec((d, d), lambda i: (0, 0), pipeline_mode=pl.Buffered(1)),
                  pl.BlockSpec((1, d), lambda i: (0, 0)),
                  pl.BlockSpec((d, LANES), lambda i: (0, 0)),
                  pl.BlockSpec((n_exp, 1), lambda i: (0, 0))],
        out_specs=[pl.BlockSpec((tm, d), lambda i: (i, 0)),
                   pl.BlockSpec((tm * (d // LANES), LANES), lambda i: (i, 0)),
                   pl.BlockSpec((n_exp, tm), lambda i: (0, i))],
        out_shape=[jax.ShapeDtypeStruct((t, d), F32),
                   jax.ShapeDtypeStruct(_rows_shape(t, d), F32),
                   jax.ShapeDtypeStruct((n_exp, t), F32)],
        compiler_params=_params(1),
        name="outproj",
    )(nat, dif, xf, mod3, w, g2.reshape(1, d), rw_pad, rb.reshape(n_exp, 1))


def _route_kernel(lt_ref, idx_ref, gate_ref, pos_ref, tmeta_ref, emeta_ref, rank_scr, *, tm, tb):
    n_exp, t = lt_ref.shape
    ntp = tmeta_ref.shape[1]
    eidx = lax.broadcasted_iota(I32, (n_exp, tb), 0)
    tri = (lax.broadcasted_iota(I32, (tb, tb), 0) < lax.broadcasted_iota(I32, (tb, tb), 1)).astype(BF16)

    def pass1(jb, counts):
        off = pl.multiple_of(jb * tb, tb)
        l = lt_ref[:, pl.ds(off, tb)]
        vals, sels, hots = [], [], []
        for _ in range(TOP_K):
            m = jnp.max(l, axis=0, keepdims=True)
            sel = jnp.min(jnp.where(l == m, eidx, n_exp), axis=0, keepdims=True)
            hot = eidx == sel
            vals.append(m)
            sels.append(sel)
            hots.append(hot)
            l = jnp.where(hot, -jnp.inf, l)
        exps = [jnp.exp(v - vals[0]) for v in vals]
        den = exps[0]
        for e in exps[1:]:
            den = den + e
        member = hots[0].astype(F32)
        for hot in hots[1:]:
            member = member + hot.astype(F32)
        before = jnp.dot(member.astype(BF16), tri, preferred_element_type=F32) + counts
        for k in range(TOP_K):
            rank = jnp.sum(jnp.where(hots[k], before, 0.0), axis=0, keepdims=True)
            idx_ref[pl.ds(k, 1), pl.ds(off, tb)] = sels[k]
            gate_ref[pl.ds(k, 1), pl.ds(off, tb)] = exps[k] / den
            rank_scr[pl.ds(k, 1), pl.ds(off, tb)] = rank.astype(I32)
        return counts + jnp.sum(member, axis=1, keepdims=True)

    counts = lax.fori_loop(0, t // tb, pass1, jnp.zeros((n_exp, 1), F32))

    padded = jnp.ceil(counts * (1.0 / tm)) * tm
    er = lax.broadcasted_iota(I32, (n_exp, LANES), 0)
    ec = lax.broadcasted_iota(I32, (n_exp, LANES), 1)

    def to_lanes(col):
        return jnp.sum(jnp.where(er == ec, col, 0.0), axis=0, keepdims=True)

    start = jnp.sum(jnp.where(ec < er, to_lanes(padded), 0.0), axis=1, keepdims=True)
    end = start + padded

    def pass2(jb, carry):
        off = pl.multiple_of(jb * tb, tb)
        for k in range(TOP_K):
            sel = idx_ref[pl.ds(k, 1), pl.ds(off, tb)]
            st = jnp.sum(jnp.where(eidx == sel, start, 0.0), axis=0, keepdims=True)
            pos_ref[pl.ds(k, 1), pl.ds(off, tb)] = st.astype(I32) + rank_scr[pl.ds(k, 1), pl.ds(off, tb)]
        return carry

    lax.fori_loop(0, t // tb, pass2, 0)

    tile_row = (lax.broadcasted_iota(I32, (n_exp, ntp), 1) * tm).astype(F32)
    tile_exp = jnp.sum((end <= tile_row).astype(F32), axis=0, keepdims=True)
    tile_exp = jnp.minimum(tile_exp, n_exp - 1.0)
    n_used = jnp.sum(padded, axis=0, keepdims=True) * (1.0 / tm)
    tmeta_ref[...] = jnp.zeros(tmeta_ref.shape, I32)
    tmeta_ref[0:1, :] = tile_exp.astype(I32)
    tmeta_ref[1:2, :] = jnp.broadcast_to(n_used, (1, ntp)).astype(I32)
    emeta_ref[...] = jnp.zeros(emeta_ref.shape, I32)
    emeta_ref[0:1, :] = to_lanes(start).astype(I32)
    emeta_ref[1:2, :] = to_lanes(counts).astype(I32)
    emeta_ref[2:3, :] = to_lanes(padded).astype(I32)


def _route(logits_t, tm, n_tiles):
    n_exp, t = logits_t.shape
    tb = _tile(t, 512)
    ntp = pl.cdiv(n_tiles, LANES) * LANES
    full = lambda shape: pl.BlockSpec(shape, lambda: (0,) * len(shape))
    return pl.pallas_call(
        functools.partial(_route_kernel, tm=tm, tb=tb),
        in_specs=[full((n_exp, t))],
        out_specs=[full((TOP_K, t)), full((TOP_K, t)), full((TOP_K, t)), full((8, ntp)), full((8, LANES))],
        out_shape=[jax.ShapeDtypeStruct((TOP_K, t), I32),
                   jax.ShapeDtypeStruct((TOP_K, t), F32),
                   jax.ShapeDtypeStruct((TOP_K, t), I32),
                   jax.ShapeDtypeStruct((8, ntp), I32),
                   jax.ShapeDtypeStruct((8, LANES), I32)],
        scratch_shapes=[pltpu.VMEM((TOP_K, t), I32)],
        compiler_params=pltpu.CompilerParams(vmem_limit_bytes=V7X_VMEM_LIMIT_BYTES),
        name="route",
    )(logits_t)


def _dispatch_kernel(estart_ref, ecnt_ref, nused_ref, pos_hbm, h_ref, xs_hbm, pos_smem, zero_scr,
                     sem_idx, sem_row, sem_zero, *, tm, nch):
    i = pl.program_id(0)
    tmd = h_ref.shape[0] // nch
    n_exp = estart_ref.shape[0]
    n_tiles = xs_hbm.shape[0] // (tm * nch)

    idx_copy = pltpu.make_async_copy(pos_hbm.at[i], pos_smem, sem_idx)
    idx_copy.start()

    @pl.when(i == 0)
    def _():
        zero_scr[...] = jnp.zeros(zero_scr.shape, F32)

        def fill(e, carry):
            cnt = ecnt_ref[e]
            pad = (tm - cnt % tm) % tm
            base = estart_ref[e] + cnt
            size = tm // 2
            while size >= 1:
                off = pad & ~(2 * size - 1)

                @pl.when((pad & size) != 0)
                def _(size=size, off=off):
                    dst0 = pl.multiple_of((base + off) * nch, nch)
                    cp = pltpu.make_async_copy(zero_scr.at[pl.ds(0, size * nch)],
                                               xs_hbm.at[pl.ds(dst0, size * nch)], sem_zero)
                    cp.start()
                    cp.wait()
                size //= 2
            return carry

        lax.fori_loop(0, n_exp, fill, 0)

        def tail_copy(j, part):
            dst0 = pl.multiple_of((j * tm + part * (tm // 2)) * nch, nch)
            return pltpu.make_async_copy(zero_scr, xs_hbm.at[pl.ds(dst0, tm // 2 * nch)], sem_zero)

        def tail_start(j, carry):
            tail_copy(j, 0).start()
            tail_copy(j, 1).start()
            return carry

        def tail_wait(j, carry):
            tail_copy(j, 0).wait()
            tail_copy(j, 1).wait()
            return carry

        lax.fori_loop(nused_ref[0], n_tiles, tail_start, 0)
        lax.fori_loop(nused_ref[0], n_tiles, tail_wait, 0)

    idx_copy.wait()

    def issue(tok, carry):
        src0 = pl.multiple_of(tok * nch, nch)
        for k in range(TOP_K):
            dst0 = pl.multiple_of(pos_smem[k * tmd + tok] * nch, nch)
            pltpu.make_async_copy(h_ref.at[pl.ds(src0, nch)], xs_hbm.at[pl.ds(dst0, nch)],
                                  sem_row).start(priority=k % 2)
        return carry

    lax.fori_loop(0, tmd, issue, 0, unroll=4)
    for _ in range(TOP_K):
        pltpu.make_async_copy(h_ref, xs_hbm.at[pl.ds(0, tmd * nch)], sem_row).wait()


def _dispatch(h2, pos_tiles, estart, ecnt, n_used, tm, n_tiles, nch):
    ntt, per_tile = pos_tiles.shape
    tmd = per_tile // TOP_K
    grid_spec = pltpu.PrefetchScalarGridSpec(
        num_scalar_prefetch=3,
        grid=(ntt,),
        in_specs=[pl.BlockSpec(memory_space=pl.ANY),
                  pl.BlockSpec((tmd * nch, LANES), lambda i, es, ec, nu: (i, 0))],
        out_specs=pl.BlockSpec(memory_space=pl.ANY),
        scratch_shapes=[pltpu.SMEM((per_tile,), I32),
                        pltpu.VMEM((tm // 2 * nch, LANES), F32),
                        pltpu.SemaphoreType.DMA, pltpu.SemaphoreType.DMA, pltpu.SemaphoreType.DMA],
    )
    return pl.pallas_call(
        functools.partial(_dispatch_kernel, tm=tm, nch=nch),
        grid_spec=grid_spec,
        out_shape=jax.ShapeDtypeStruct((n_tiles * tm * nch, LANES), F32),
        compiler_params=_params(1),
        name="dispatch",
    )(estart, ecnt, n_used, pos_tiles, h2)


GATE_UP_SPLIT = 2


def _gate_up_kernel(stile_ref, shalf_ref, sexp_ref, nsteps_ref, x_ref, wg_ref, wu_ref, bg_ref, bu_ref, act_ref,
                    *, chunk, nch):
    s = pl.program_id(0)

    @pl.when(s < nsteps_ref[0])
    def _():
        tm, ffh = act_ref.shape
        x = jnp.concatenate([x_ref[pl.ds(ch, tm, stride=nch), :] for ch in range(nch)], axis=-1).astype(BF16)
        for c0 in range(0, ffh, chunk):
            cols = slice(c0, c0 + chunk)
            gate = jnp.dot(x, wg_ref[:, cols].astype(BF16), preferred_element_type=F32) + bg_ref[:, cols]
            up = jnp.dot(x, wu_ref[:, cols].astype(BF16), preferred_element_type=F32) + bu_ref[:, cols]
            gate = jnp.minimum(gate, SWIGLU_LIMIT)
            up = jnp.clip(up, -SWIGLU_LIMIT, SWIGLU_LIMIT)
            act = (up + 1.0) * gate * jax.nn.sigmoid(gate * SWIGLU_ALPHA)
            act_ref[:, cols] = act.astype(BF16)

    @pl.when(s >= nsteps_ref[0])
    def _():
        act_ref[...] = jnp.zeros(act_ref.shape, BF16)


def _gate_up_schedule(tile_exp, n_used, first_tile, group_tiles, n_tiles):
    ns = GATE_UP_SPLIT
    s = jnp.arange(ns * n_tiles, dtype=I32)
    e = tile_exp[s // ns]
    first = first_tile[e]
    count = jnp.maximum(group_tiles[e], 1)
    local = s - ns * first
    tile, half = first + local % count, local // count
    n_steps = ns * n_used
    last = jnp.maximum(n_steps - 1, 0)
    live = s < n_steps
    pick = lambda v: jnp.where(live, v, v[last])
    return pick(tile), pick(half), pick(e), n_steps


def _gate_up(xs, w, b, tile_exp, n_used, first_tile, group_tiles, tm):
    n_exp, d, ff2 = w.shape
    nch = d // LANES
    rows = xs.shape[0] // nch
    ff = ff2 // 2
    n_tiles = rows // tm
    ns = GATE_UP_SPLIT
    ffh = ff // ns
    stile, shalf, sexp, n_steps = _gate_up_schedule(tile_exp, n_used, first_tile, group_tiles, n_tiles)

    def out_index(s, st, sh, se, n):
        live = s < n[0]
        return jnp.where(live, st[s], s // ns), jnp.where(live, sh[s], s % ns)

    b3 = b.reshape(n_exp, 1, ff2)
    grid_spec = pltpu.PrefetchScalarGridSpec(
        num_scalar_prefetch=4,
        grid=(ns * n_tiles,),
        in_specs=[pl.BlockSpec((tm * nch, LANES), lambda s, st, sh, se, n: (st[s], 0)),
                  pl.BlockSpec((None, d, ffh), lambda s, st, sh, se, n: (se[s], 0, sh[s])),
                  pl.BlockSpec((None, d, ffh), lambda s, st, sh, se, n: (se[s], 0, ns + sh[s])),
                  pl.BlockSpec((None, 1, ffh), lambda s, st, sh, se, n: (se[s], 0, sh[s])),
                  pl.BlockSpec((None, 1, ffh), lambda s, st, sh, se, n: (se[s], 0, ns + sh[s]))],
        out_specs=pl.BlockSpec((tm, ffh), out_index),
    )
    return pl.pallas_call(
        functools.partial(_gate_up_kernel, chunk=_tile(ffh, 512), nch=nch),
        grid_spec=grid_spec,
        out_shape=jax.ShapeDtypeStruct((rows, ff), BF16),
        compiler_params=_params(1),
        name="gate_up",
    )(stile, shalf, sexp, n_steps, xs, w, w, b3, b3)


def _down_kernel(texp_ref, nused_ref, a_ref, w_ref, b_ref, y_ref):
    i = pl.program_id(0)

    @pl.when(i < nused_ref[0])
    def _():
        tm = a_ref.shape[0]
        nch = w_ref.shape[1] // LANES
        a = a_ref[...]
        chunk = _tile(w_ref.shape[1], 512)
        for c0 in range(0, w_ref.shape[1], chunk):
            y = (jnp.dot(a, w_ref[:, c0:c0 + chunk].astype(BF16), preferred_element_type=F32)
                 + b_ref[:, c0:c0 + chunk])
            for j in range(chunk // LANES):
                y_ref[pl.ds(c0 // LANES + j, tm, stride=nch), :] = y[:, j * LANES:(j + 1) * LANES]

    @pl.when(i >= nused_ref[0])
    def _():
        y_ref[...] = jnp.zeros(y_ref.shape, F32)


def _down(act, w, b, tile_exp, n_used, tm):
    rows, ff = act.shape
    n_exp, _, d = w.shape
    nch = d // LANES
    n_tiles = rows // tm
    live = lambda i, te, nu: jnp.minimum(i, nu[0] - 1)
    grid_spec = pltpu.PrefetchScalarGridSpec(
        num_scalar_prefetch=2,
        grid=(n_tiles,),
        in_specs=[pl.BlockSpec((tm, ff), lambda i, te, nu: (live(i, te, nu), 0)),
                  pl.BlockSpec((None, ff, d), lambda i, te, nu: (te[i], 0, 0)),
                  pl.BlockSpec((None, 1, d), lambda i, te, nu: (te[i], 0, 0))],
        out_specs=pl.BlockSpec((tm * nch, LANES), lambda i, te, nu: (i, 0)),
    )
    return pl.pallas_call(
        _down_kernel,
        grid_spec=grid_spec,
        out_shape=jax.ShapeDtypeStruct(_rows_shape(rows, d), F32),
        compiler_params=_params(1),
        name="down",
    )(tile_exp, n_used, act, w, b.reshape(n_exp, 1, d))


def _combine_kernel(pos_hbm, y_hbm, x1_ref, mod_ref, g_ref, o_ref, pos_smem, ybuf, sem_idx, sem_row):
    i = pl.program_id(0)
    n = pl.num_programs(0)
    tmc, d = x1_ref.shape
    nch = d // LANES
    slot = i % 2

    def idx_copy(step, s):
        return pltpu.make_async_copy(pos_hbm.at[step], pos_smem.at[s], sem_idx.at[s])

    def start_gathers_static(s):
        def issue(tok, carry):
            dst0 = pl.multiple_of(tok * nch, nch)
            for k in range(TOP_K):
                src0 = pl.multiple_of(pos_smem[s, k * tmc + tok] * nch, nch)
                pltpu.make_async_copy(y_hbm.at[pl.ds(src0, nch)], ybuf.at[s, k, pl.ds(dst0, nch)],
                                      sem_row.at[s]).start(priority=k % 2)
            return carry

        lax.fori_loop(0, tmc, issue, 0, unroll=4)

    def start_gathers(s):
        for static_s in range(2):
            pl.when(s == static_s)(functools.partial(start_gathers_static, static_s))

    @pl.when(i == 0)
    def _():
        idx_copy(0, 0).start()
        idx_copy(0, 0).wait()
        start_gathers_static(0)

        @pl.when(n > 1)
        def _():
            idx_copy(1, 1).start()

    @pl.when(i + 1 < n)
    def _():
        idx_copy(i + 1, 1 - slot).wait()
        start_gathers(1 - slot)

    @pl.when(i + 2 < n)
    def _():
        idx_copy(i + 2, slot).start()

    for k in range(TOP_K):
        pltpu.make_async_copy(y_hbm.at[pl.ds(0, tmc * nch)], ybuf.at[slot, k], sem_row.at[slot]).wait()

    g = g_ref[...]
    gate2 = mod_ref[...][5:6]
    for ch in range(nch):
        cols = slice(ch * LANES, (ch + 1) * LANES)
        moe = ybuf[slot, 0, pl.ds(ch, tmc, stride=nch), :] * g[:, 0:1]
        for k in range(1, TOP_K):
            moe = moe + ybuf[slot, k, pl.ds(ch, tmc, stride=nch), :] * g[:, k:k + 1]
        o_ref[:, cols] = x1_ref[:, cols] + gate2[:, cols] * moe


def _combine(y, pos_tiles, x1, mod3, gates_t, seq):
    t, d = x1.shape
    ntt, per_tile = pos_tiles.shape
    tmc = per_tile // TOP_K
    per_b = seq // tmc
    return pl.pallas_call(
        _combine_kernel,
        grid=(ntt,),
        in_specs=[pl.BlockSpec(memory_space=pl.ANY),
                  pl.BlockSpec(memory_space=pl.ANY),
                  pl.BlockSpec((tmc, d), lambda i: (i, 0)),
                  pl.BlockSpec((None, 6, d), lambda i: (i // per_b, 0, 0)),
                  pl.BlockSpec((tmc, TOP_K), lambda i: (i, 0))],
        out_specs=pl.BlockSpec((tmc, d), lambda i: (i, 0)),
        out_shape=jax.ShapeDtypeStruct((t, d), F32),
        scratch_shapes=[pltpu.SMEM((2, per_tile), I32),
                        pltpu.VMEM((2, TOP_K, tmc * (d // LANES), LANES), F32),
                        pltpu.SemaphoreType.DMA((2,)), pltpu.SemaphoreType.DMA((2,))],
        compiler_params=_params(1),
        name="combine",
    )(pos_tiles, y, x1, mod3, gates_t)


def _moe(h2, logits_t, x1, mod3, w_gu, b_gu, w_dn, b_dn, seq):
    t, d = x1.shape
    n_exp = logits_t.shape[0]
    tm = _tile(t * TOP_K, 512)
    n_tiles = (t * TOP_K) // tm + n_exp
    idx, gates, pos, tmeta, emeta = _route(logits_t, tm, n_tiles)
    del idx
    tok_tile = _tile(seq, 256)
    pos_tiles = (pos.reshape(TOP_K, t // tok_tile, tok_tile).transpose(1, 0, 2)
                 .reshape(t // tok_tile, TOP_K * tok_tile))
    tile_exp = tmeta[0, :n_tiles]
    n_used = tmeta[1, :1]
    xs = _dispatch(h2, pos_tiles, emeta[0, :n_exp], emeta[1, :n_exp], n_used, tm, n_tiles, d // LANES)
    act = _gate_up(xs, w_gu, b_gu, tile_exp, n_used, emeta[0, :n_exp] // tm, emeta[2, :n_exp] // tm, tm)
    y = _down(act, w_dn, b_dn, tile_exp, n_used, tm)
    return _combine(y, pos_tiles, x1, mod3, gates.T, seq)


def kernel(x, c, w_ada, b_ada, norm1_g, w_in, nat_q_g, nat_k_g, nat_rpb, diff_q_g, diff_k_g, diff_lambda,
           diff_sub_g, rel_bias_table, w_out, norm2_g, router_w, router_b, w_gate_up, b_gate_up, w_down, b_down):
    bsz, seq, d = x.shape
    t = bsz * seq
    ns = (d // 2) // HEAD_DIM
    n_exp = router_w.shape[-1]
    scale = HEAD_DIM ** -0.5
    rows = seq // GRID_W
    kr = min(NAT_KR_MAX, rows)
    t5_bias = _t5_bias_by_offset(rel_bias_table, seq)
    ones = jnp.ones((HEAD_DIM,), F32)
    xf = x.reshape(t, d)
    for l in range(w_ada.shape[0]):
        lambda_init = 0.8 - 0.6 * math.exp(-0.3 * l)
        mod3 = _adaln(c, w_ada[l], b_ada[l]).reshape(bsz, 6, d)
        gains = jnp.stack([nat_q_g[l] * scale, nat_k_g[l], ones, diff_q_g[l] * (scale * LOG2_E), diff_k_g[l],
                           ones, ones, ones])
        qkv = _inproj(xf, mod3, norm1_g[l], w_in[l], gains, seq)
        nat = _nat_attention(qkv, _nat_bias_table(nat_rpb[l], kr), bsz, seq, ns)
        dif = _diff_attention(qkv, t5_bias, diff_lambda[l], diff_sub_g[l], lambda_init, bsz, seq, ns)
        rw_pad = jnp.pad(router_w[l], ((0, 0), (0, LANES - n_exp)))
        x1, h2, logits_t = _outproj(nat, dif, xf, mod3, w_out[l], norm2_g[l], rw_pad, router_b[l], seq)
        xf = _moe(h2, logits_t, x1, mod3, w_gate_up[l], b_gate_up[l], w_down[l], b_down[l], seq)
    return xf.reshape(bsz, seq, d)
```

```python
import functools
import math

import jax
import jax.numpy as jnp
from jax import lax
from jax.experimental import pallas as pl
from jax.experimental.pallas import tpu as pltpu

HEAD_DIM = 128
GRID_W = 64
NAT_KR_MAX = 8
NAT_KC = 16
N_BUCKETS = 32
MAX_DISTANCE = 128
TOP_K = 4
SWIGLU_LIMIT = 7.0
SWIGLU_ALPHA = 1.702
RMS_EPS = 1e-6
NEG_INF = -1e30
LOG2_E = 1.4426950408889634

LANES = 128
PAIR = 2 * LANES
V7X_VMEM_LIMIT_BYTES = 56 * 1024 * 1024

F32 = jnp.float32
BF16 = jnp.bfloat16
I32 = jnp.int32
U32 = jnp.uint32


def _tile(n, pref):
    t = min(n, pref)
    assert n % t == 0, (n, pref)
    return t


def _params(n_axes):
    return pltpu.CompilerParams(dimension_semantics=("arbitrary",) * n_axes,
                                vmem_limit_bytes=V7X_VMEM_LIMIT_BYTES)


def _adaln_kernel(c_ref, w_ref, b_ref, o_ref):
    c = c_ref[...]
    a = (c * jax.nn.sigmoid(c)).astype(BF16)
    o_ref[...] = jnp.dot(a, w_ref[...].astype(BF16), preferred_element_type=F32) + b_ref[...]


def _adaln(c, w, b):
    bsz, d = c.shape
    n = w.shape[1]
    tn = _tile(n, 1024)
    return pl.pallas_call(
        _adaln_kernel,
        grid=(n // tn,),
        in_specs=[pl.BlockSpec((bsz, d), lambda j: (0, 0)),
                  pl.BlockSpec((d, tn), lambda j: (0, j)),
                  pl.BlockSpec((1, tn), lambda j: (0, j))],
        out_specs=pl.BlockSpec((bsz, tn), lambda j: (0, j)),
        out_shape=jax.ShapeDtypeStruct((bsz, n), F32),
        compiler_params=_params(1),
        name="adaln",
    )(c, w, b.reshape(1, n))


def _inproj_kernel(x_ref, mod_ref, g_ref, w_ref, gain_ref, o_ref, h_scr):
    j = pl.program_id(1)

    @pl.when(j == 0)
    def _():
        x = x_ref[...]
        ms = jnp.mean(x * x, axis=-1, keepdims=True)
        y = x * lax.rsqrt(ms + RMS_EPS) * g_ref[...]
        m = mod_ref[...]
        h_scr[...] = (y * (1.0 + m[1:2]) + m[0:1]).astype(BF16)

    heads = o_ref.shape[0]
    per = min(heads, PAIR // HEAD_DIM)
    is_value = (j == 2) | (j == 5)
    g = gain_ref[pl.ds(j, 1), :]
    h = h_scr[...]
    for c0 in range(0, heads, per):
        y = jnp.dot(h, w_ref[:, c0 * HEAD_DIM:(c0 + per) * HEAD_DIM].astype(BF16), preferred_element_type=F32)
        for hh in range(per):
            yh = y[:, hh * HEAD_DIM:(hh + 1) * HEAD_DIM]
            ms = jnp.mean(yh * yh, axis=-1, keepdims=True)
            factor = jnp.where(is_value, 1.0, lax.rsqrt(ms + RMS_EPS))
            o_ref[c0 + hh] = (yh * factor * g).astype(BF16)


def _inproj(xf, mod3, g1, w, gains, seq):
    t, d = xf.shape
    seg = d // 2
    ns = seg // HEAD_DIM
    tm = _tile(seq, 1024)
    per_b = seq // tm
    return pl.pallas_call(
        _inproj_kernel,
        grid=(t // tm, 6),
        in_specs=[pl.BlockSpec((tm, d), lambda i, j: (i, 0)),
                  pl.BlockSpec((None, 6, d), lambda i, j: (i // per_b, 0, 0)),
                  pl.BlockSpec((1, d), lambda i, j: (0, 0)),
                  pl.BlockSpec((d, seg), lambda i, j: (0, j)),
                  pl.BlockSpec((8, HEAD_DIM), lambda i, j: (0, 0))],
        out_specs=pl.BlockSpec((ns, tm, HEAD_DIM), lambda i, j: (j, i, 0)),
        out_shape=jax.ShapeDtypeStruct((6 * ns, t, HEAD_DIM), BF16),
        scratch_shapes=[pltpu.VMEM((tm, d), BF16)],
        compiler_params=_params(2),
        name="inproj",
    )(xf, mod3, g1.reshape(1, d), w, gains)


def _nat_kernel(q_ref, k_ref, v_ref, w_ref, o_ref, b_ref, *, rows, kr, group):
    band = kr * GRID_W

    @pl.when(pl.program_id(1) == 0)
    def _():
        qc = lax.broadcasted_iota(I32, (GRID_W, LANES), 0)
        lane = lax.broadcasted_iota(I32, (GRID_W, LANES), 1)
        kc = lane % GRID_W
        col_start = jnp.clip(qc - NAT_KC // 2, 0, GRID_W - NAT_KC)
        visible = (kc >= col_start) & (kc < col_start + NAT_KC)
        for d in range(kr):
            for j in range(0, kr, 2):
                def tile(jj, shift):
                    ro = jj - d + (NAT_KR_MAX - 1)
                    row = jnp.broadcast_to(w_ref[ro:ro + 1, :], (GRID_W, LANES))
                    return pltpu.roll(row, shift, 1, stride=1, stride_axis=0)
                both = jnp.where(lane < GRID_W, tile(j, 0), tile(j + 1, GRID_W))
                b_ref[d, :, j * GRID_W:(j + 2) * GRID_W] = jnp.where(visible, both, NEG_INF)

    def body(g, carry):
        k0s, scores = [], []
        for i in range(group):
            r = g * group + i
            rs = jnp.clip(r - kr // 2, 0, rows - kr)
            k0 = pl.multiple_of(rs * GRID_W, GRID_W)
            q = q_ref[pl.ds(pl.multiple_of(r * GRID_W, GRID_W), GRID_W), :]
            s = lax.dot_general(q, k_ref[pl.ds(k0, band), :], (((1,), (1,)), ((), ())),
                                preferred_element_type=F32)
            k0s.append(k0)
            scores.append(s + b_ref[r - rs])
        s = jnp.concatenate(scores, axis=0)
        p = jnp.exp(s - jnp.max(s, axis=-1, keepdims=True))
        den = jnp.sum(p, axis=-1, keepdims=True)
        pb = p.astype(BF16)
        outs = [jnp.dot(pb[i * GRID_W:(i + 1) * GRID_W], v_ref[pl.ds(k0s[i], band), :],
                        preferred_element_type=F32) for i in range(group)]
        o = jnp.concatenate(outs, axis=0) / den
        o_ref[pl.ds(pl.multiple_of(g * (group * GRID_W), group * GRID_W), group * GRID_W), :] = o.astype(BF16)
        return carry

    lax.fori_loop(0, rows // group, body, 0)


def _nat_bias_by_offset(rpb):
    lane = jnp.arange(LANES)
    off = jnp.where(lane < LANES // 2, lane, lane - LANES)
    return jnp.take(rpb.astype(F32), jnp.clip(off, -(NAT_KC - 1), NAT_KC - 1) + (NAT_KC - 1), axis=-1)


def _nat_attention(qkv, rpb, bsz, seq, ns):
    t = qkv.shape[1]
    rows = seq // GRID_W
    kr = min(NAT_KR_MAX, rows)
    assert kr % 2 == 0 and 2 * GRID_W == LANES
    blk = (None, seq, HEAD_DIM)
    n_off = 2 * NAT_KR_MAX - 1
    return pl.pallas_call(
        functools.partial(_nat_kernel, rows=rows, kr=kr, group=_tile(rows, 8)),
        grid=(ns, bsz),
        in_specs=[pl.BlockSpec(blk, lambda h, b: (h, b, 0)),
                  pl.BlockSpec(blk, lambda h, b: (ns + h, b, 0)),
                  pl.BlockSpec(blk, lambda h, b: (2 * ns + h, b, 0)),
                  pl.BlockSpec((None, n_off, LANES), lambda h, b: (h, 0, 0))],
        out_specs=pl.BlockSpec(blk, lambda h, b: (h, b, 0)),
        out_shape=jax.ShapeDtypeStruct((ns, t, HEAD_DIM), BF16),
        scratch_shapes=[pltpu.VMEM((kr, GRID_W, kr * GRID_W), F32)],
        compiler_params=_params(2),
        name="nat_attn",
    )(qkv, qkv, qkv, _nat_bias_by_offset(rpb))


def _t5_bucket(rel):
    nb = N_BUCKETS // 2
    max_exact = nb // 2
    ret = jnp.where(rel > 0, nb, 0)
    n = jnp.abs(rel)
    nf = jnp.maximum(n, 1).astype(F32)
    large = max_exact + (jnp.log(nf / max_exact) / math.log(MAX_DISTANCE / max_exact)
                         * (nb - max_exact)).astype(I32)
    large = jnp.minimum(large, nb - 1)
    return ret + jnp.where(n < max_exact, n, large)


def _t5_bias_by_offset(rel_table, seq):
    rel = jnp.arange(2 * seq, dtype=I32) - seq
    u = rel_table[_t5_bucket(rel)].T.astype(F32) * LOG2_E
    return u.reshape(u.shape[0], 1, 2 * seq)


def _diff_kernel(q_ref, k_ref, v_ref, u_ref, lam_ref, sg_ref, o_ref, strip_scr, *, seq, tq, nq, n_split,
                 lambda_init):
    qi = pl.program_id(2)

    @pl.when((pl.program_id(1) == 0) & (qi == 0))
    def _():
        strip_scr[...] = pltpu.roll(jnp.broadcast_to(u_ref[...], strip_scr.shape), 0, 1, stride=1, stride_axis=0)

    bias = strip_scr[:, pl.ds(pl.multiple_of((nq - qi) * tq, tq), seq)]
    lam = lam_ref[...]
    lam_full = (jnp.exp(jnp.sum(lam[0:1] * lam[1:2], axis=-1, keepdims=True))
                - jnp.exp(jnp.sum(lam[2:3] * lam[3:4], axis=-1, keepdims=True)) + lambda_init)

    v = jnp.concatenate([v_ref[0], v_ref[1]], axis=-1)
    half = tq // n_split

    def scores(p, rows):
        s = lax.dot_general(q_ref[p, rows, :], k_ref[p], (((1,), (1,)), ((), ())), preferred_element_type=F32)
        return s + bias[rows]

    def attend(s1, s2):
        e1 = jnp.exp2(s1 - jnp.max(s1, axis=-1, keepdims=True))
        e2 = jnp.exp2(s2 - jnp.max(s2, axis=-1, keepdims=True))
        r1 = 1.0 / jnp.sum(e1, axis=-1, keepdims=True)
        r2 = lam_full / jnp.sum(e2, axis=-1, keepdims=True)
        o = jnp.dot((e1 * r1 - e2 * r2).astype(BF16), v, preferred_element_type=F32)
        ms = jnp.mean(o * o, axis=-1, keepdims=True)
        return o * lax.rsqrt(ms + RMS_EPS) * sg_ref[...] * (1.0 - lambda_init)

    blocks = [slice(j * half, (j + 1) * half) for j in range(n_split)]
    all_scores = [(scores(0, rows), scores(1, rows)) for rows in blocks]
    for rows, (s1, s2) in zip(blocks, all_scores):
        o = attend(s1, s2)
        o_ref[0, rows, :] = o[:, :HEAD_DIM].astype(BF16)
        o_ref[1, rows, :] = o[:, HEAD_DIM:].astype(BF16)


def _diff_attention(qkv, bias_by_offset, lam, sub_g, lambda_init, bsz, seq, ns):
    t = qkv.shape[1]
    hd = ns // 2
    tq = _tile(seq, 256)
    nq = seq // tq
    qb, kb, vb = 3 * ns // 2, 4 * ns // 2, 5 * ns // 2
    return pl.pallas_call(
        functools.partial(_diff_kernel, seq=seq, tq=tq, nq=nq, n_split=2, lambda_init=lambda_init),
        grid=(hd, bsz, nq),
        in_specs=[pl.BlockSpec((2, tq, HEAD_DIM), lambda h, b, i: (qb + h, b * nq + i, 0)),
                  pl.BlockSpec((2, seq, HEAD_DIM), lambda h, b, i: (kb + h, b, 0)),
                  pl.BlockSpec((2, seq, HEAD_DIM), lambda h, b, i: (vb + h, b, 0)),
                  pl.BlockSpec((None, 1, 2 * seq), lambda h, b, i: (h, 0, 0)),
                  pl.BlockSpec((4, HEAD_DIM), lambda h, b, i: (0, 0)),
                  pl.BlockSpec((1, 2 * HEAD_DIM), lambda h, b, i: (0, 0))],
        out_specs=pl.BlockSpec((2, tq, HEAD_DIM), lambda h, b, i: (h, b * nq + i, 0)),
        out_shape=jax.ShapeDtypeStruct((ns, t, HEAD_DIM), BF16),
        scratch_shapes=[pltpu.VMEM((tq, 2 * seq), F32)],
        compiler_params=_params(3),
        name="diff_attn",
    )(qkv, qkv, qkv, bias_by_offset, lam, sub_g.reshape(1, 2 * HEAD_DIM))


def _rows_shape(rows, d):
    return (rows * (d // PAIR), LANES)


def _pack_pair(lo, hi):
    lo_w = lax.bitcast_convert_type(lo.astype(BF16).astype(F32), U32) >> 16
    hi_w = lax.bitcast_convert_type(hi.astype(BF16).astype(F32), U32) & jnp.uint32(0xFFFF0000)
    return lo_w | hi_w


def _unpack_pair(w):
    return (lax.bitcast_convert_type(w << 16, F32),
            lax.bitcast_convert_type(w & jnp.uint32(0xFFFF0000), F32))


def _split_bf16(v):
    hi = v.astype(BF16)
    lo = (v - hi.astype(F32)).astype(BF16)
    return hi, lo


def _outproj_kernel(nat_ref, dif_ref, x_ref, mod_ref, w_ref, g_ref, rw_ref, rb_ref,
                    x1_ref, h2_ref, lt_ref):
    ns = nat_ref.shape[0]
    a = jnp.concatenate([nat_ref[hh] for hh in range(ns)] + [dif_ref[hh] for hh in range(ns)], axis=-1)
    mix = jnp.dot(a, w_ref[...].astype(BF16), preferred_element_type=F32)
    m = mod_ref[...]
    x1 = x_ref[...] + m[2:3] * mix
    x1_ref[...] = x1
    ms = jnp.mean(x1 * x1, axis=-1, keepdims=True)
    h2 = x1 * lax.rsqrt(ms + RMS_EPS) * g_ref[...] * (1.0 + m[4:5]) + m[3:4]
    npair = h2.shape[1] // PAIR
    for p in range(npair):
        h2_ref[pl.ds(p, h2.shape[0], stride=npair), :] = _pack_pair(
            h2[:, 2 * p * LANES:(2 * p + 1) * LANES], h2[:, (2 * p + 1) * LANES:(2 * p + 2) * LANES])
    h_hi, h_lo = _split_bf16(h2)
    w_hi, w_lo = _split_bf16(rw_ref[...])
    lg = (jnp.dot(h_hi, w_hi, preferred_element_type=F32)
          + jnp.dot(h_lo, w_hi, preferred_element_type=F32)
          + jnp.dot(h_hi, w_lo, preferred_element_type=F32))
    n_exp = lt_ref.shape[0]
    lt_ref[...] = lg.T[:n_exp] + rb_ref[...]


def _outproj(nat, dif, xf, mod3, w, g2, rw_pad, rb, seq):
    t, d = xf.shape
    ns = nat.shape[0]
    n_exp = rb.shape[0]
    tm = _tile(seq, 512)
    per_b = seq // tm
    return pl.pallas_call(
        _outproj_kernel,
        grid=(t // tm,),
        in_specs=[pl.BlockSpec((ns, tm, HEAD_DIM), lambda i: (0, i, 0)),
                  pl.BlockSpec((ns, tm, HEAD_DIM), lambda i: (0, i, 0)),
                  pl.BlockSpec((tm, d), lambda i: (i, 0)),
                  pl.BlockSpec((None, 6, d), lambda i: (i // per_b, 0, 0)),
                  pl.BlockSpec((d, d), lambda i: (0, 0), pipeline_mode=pl.Buffered(1)),
                  pl.BlockSpec((1, d), lambda i: (0, 0)),
                  pl.BlockSpec((d, LANES), lambda i: (0, 0)),
                  pl.BlockSpec((n_exp, 1), lambda i: (0, 0))],
        out_specs=[pl.BlockSpec((tm, d), lambda i: (i, 0)),
                   pl.BlockSpec((tm * (d // PAIR), LANES), lambda i: (i, 0)),
                   pl.BlockSpec((n_exp, tm), lambda i: (0, i))],
        out_shape=[jax.ShapeDtypeStruct((t, d), F32),
                   jax.ShapeDtypeStruct(_rows_shape(t, d), U32),
                   jax.ShapeDtypeStruct((n_exp, t), F32)],
        compiler_params=_params(1),
        name="outproj",
    )(nat, dif, xf, mod3, w, g2.reshape(1, d), rw_pad, rb.reshape(n_exp, 1))


def _route_kernel(lt_ref, idx_ref, gate_ref, pos_ref, tmeta_ref, emeta_ref, rank_scr, *, tm, tb):
    n_exp, t = lt_ref.shape
    ntp = tmeta_ref.shape[1]
    eidx = lax.broadcasted_iota(I32, (n_exp, tb), 0)
    tri = (lax.broadcasted_iota(I32, (tb, tb), 0) < lax.broadcasted_iota(I32, (tb, tb), 1)).astype(BF16)

    def pass1(jb, counts):
        off = pl.multiple_of(jb * tb, tb)
        l = lt_ref[:, pl.ds(off, tb)]
        vals, sels, hots = [], [], []
        for _ in range(TOP_K):
            m = jnp.max(l, axis=0, keepdims=True)
            sel = jnp.min(jnp.where(l == m, eidx, n_exp), axis=0, keepdims=True)
            hot = eidx == sel
            vals.append(m)
            sels.append(sel)
            hots.append(hot)
            l = jnp.where(hot, -jnp.inf, l)
        exps = [jnp.exp(v - vals[0]) for v in vals]
        den = exps[0]
        for e in exps[1:]:
            den = den + e
        member = hots[0].astype(F32)
        for hot in hots[1:]:
            member = member + hot.astype(F32)
        before = jnp.dot(member.astype(BF16), tri, preferred_element_type=F32) + counts
        for k in range(TOP_K):
            rank = jnp.sum(jnp.where(hots[k], before, 0.0), axis=0, keepdims=True)
            idx_ref[pl.ds(k, 1), pl.ds(off, tb)] = sels[k]
            gate_ref[pl.ds(k, 1), pl.ds(off, tb)] = exps[k] / den
            rank_scr[pl.ds(k, 1), pl.ds(off, tb)] = rank.astype(I32)
        return counts + jnp.sum(member, axis=1, keepdims=True)

    counts = lax.fori_loop(0, t // tb, pass1, jnp.zeros((n_exp, 1), F32))

    padded = jnp.ceil(counts * (1.0 / tm)) * tm
    er = lax.broadcasted_iota(I32, (n_exp, LANES), 0)
    ec = lax.broadcasted_iota(I32, (n_exp, LANES), 1)

    def to_lanes(col):
        return jnp.sum(jnp.where(er == ec, col, 0.0), axis=0, keepdims=True)

    start = jnp.sum(jnp.where(ec < er, to_lanes(padded), 0.0), axis=1, keepdims=True)
    end = start + padded

    def pass2(jb, carry):
        off = pl.multiple_of(jb * tb, tb)
        for k in range(TOP_K):
            sel = idx_ref[pl.ds(k, 1), pl.ds(off, tb)]
            st = jnp.sum(jnp.where(eidx == sel, start, 0.0), axis=0, keepdims=True)
            pos_ref[pl.ds(k, 1), pl.ds(off, tb)] = st.astype(I32) + rank_scr[pl.ds(k, 1), pl.ds(off, tb)]
        return carry

    lax.fori_loop(0, t // tb, pass2, 0)

    tile_row = (lax.broadcasted_iota(I32, (n_exp, ntp), 1) * tm).astype(F32)
    tile_exp = jnp.sum((end <= tile_row).astype(F32), axis=0, keepdims=True)
    tile_exp = jnp.minimum(tile_exp, n_exp - 1.0)
    n_used = jnp.sum(padded, axis=0, keepdims=True) * (1.0 / tm)
    tmeta_ref[...] = jnp.zeros(tmeta_ref.shape, I32)
    tmeta_ref[0:1, :] = tile_exp.astype(I32)
    tmeta_ref[1:2, :] = jnp.broadcast_to(n_used, (1, ntp)).astype(I32)
    emeta_ref[...] = jnp.zeros(emeta_ref.shape, I32)
    emeta_ref[0:1, :] = to_lanes(start).astype(I32)
    emeta_ref[1:2, :] = to_lanes(counts).astype(I32)
    emeta_ref[2:3, :] = to_lanes(padded).astype(I32)


def _route(logits_t, tm, n_tiles):
    n_exp, t = logits_t.shape
    tb = _tile(t, 512)
    ntp = pl.cdiv(n_tiles, LANES) * LANES
    full = lambda shape: pl.BlockSpec(shape, lambda: (0,) * len(shape))
    return pl.pallas_call(
        functools.partial(_route_kernel, tm=tm, tb=tb),
        in_specs=[full((n_exp, t))],
        out_specs=[full((TOP_K, t)), full((TOP_K, t)), full((TOP_K, t)), full((8, ntp)), full((8, LANES))],
        out_shape=[jax.ShapeDtypeStruct((TOP_K, t), I32),
                   jax.ShapeDtypeStruct((TOP_K, t), F32),
                   jax.ShapeDtypeStruct((TOP_K, t), I32),
                   jax.ShapeDtypeStruct((8, ntp), I32),
                   jax.ShapeDtypeStruct((8, LANES), I32)],
        scratch_shapes=[pltpu.VMEM((TOP_K, t), I32)],
        compiler_params=pltpu.CompilerParams(vmem_limit_bytes=V7X_VMEM_LIMIT_BYTES),
        name="route",
    )(logits_t)


def _dispatch_kernel(estart_ref, ecnt_ref, nused_ref, pos_hbm, h_ref, xs_hbm, pos_smem, zero_scr,
                     sem_idx, sem_row, sem_zero, *, tm, nch):
    i = pl.program_id(0)
    tmd = h_ref.shape[0] // nch
    n_exp = estart_ref.shape[0]
    n_tiles = xs_hbm.shape[0] // (tm * nch)
    slot = i % 2

    def idx_copy(step, s):
        return pltpu.make_async_copy(pos_hbm.at[step], pos_smem.at[s], sem_idx.at[s])

    @pl.when(i == 0)
    def _():
        idx_copy(0, 0).start()
        zero_scr[...] = jnp.zeros(zero_scr.shape, zero_scr.dtype)

        def fill(e, carry):
            cnt = ecnt_ref[e]
            pad = (tm - cnt % tm) % tm
            base = estart_ref[e] + cnt
            size = tm // 2
            while size >= 1:
                off = pad & ~(2 * size - 1)

                @pl.when((pad & size) != 0)
                def _(size=size, off=off):
                    dst0 = pl.multiple_of((base + off) * nch, nch)
                    cp = pltpu.make_async_copy(zero_scr.at[pl.ds(0, size * nch)],
                                               xs_hbm.at[pl.ds(dst0, size * nch)], sem_zero)
                    cp.start()
                    cp.wait()
                size //= 2
            return carry

        lax.fori_loop(0, n_exp, fill, 0)

        def tail_copy(j, part):
            dst0 = pl.multiple_of((j * tm + part * (tm // 2)) * nch, nch)
            return pltpu.make_async_copy(zero_scr, xs_hbm.at[pl.ds(dst0, tm // 2 * nch)], sem_zero)

        def tail_start(j, carry):
            tail_copy(j, 0).start()
            tail_copy(j, 1).start()
            return carry

        def tail_wait(j, carry):
            tail_copy(j, 0).wait()
            tail_copy(j, 1).wait()
            return carry

        lax.fori_loop(nused_ref[0], n_tiles, tail_start, 0)
        lax.fori_loop(nused_ref[0], n_tiles, tail_wait, 0)

    @pl.when(i + 1 < pl.num_programs(0))
    def _():
        idx_copy(i + 1, 1 - slot).start()

    idx_copy(i, slot).wait()

    def start_rows(s):
        def issue(tok, carry):
            src0 = pl.multiple_of(tok * nch, nch)
            for k in range(TOP_K):
                dst0 = pl.multiple_of(pos_smem[s, k * tmd + tok] * nch, nch)
                pltpu.make_async_copy(h_ref.at[pl.ds(src0, nch)], xs_hbm.at[pl.ds(dst0, nch)],
                                      sem_row).start(priority=k % 2)
            return carry

        lax.fori_loop(0, tmd, issue, 0, unroll=4)

    for s in range(2):
        pl.when(slot == s)(functools.partial(start_rows, s))
    for _ in range(TOP_K):
        pltpu.make_async_copy(h_ref, xs_hbm.at[pl.ds(0, tmd * nch)], sem_row).wait()


def _dispatch(h2, pos_tiles, estart, ecnt, n_used, tm, n_tiles, nch):
    ntt, per_tile = pos_tiles.shape
    tmd = per_tile // TOP_K
    grid_spec = pltpu.PrefetchScalarGridSpec(
        num_scalar_prefetch=3,
        grid=(ntt,),
        in_specs=[pl.BlockSpec(memory_space=pl.ANY),
                  pl.BlockSpec((tmd * nch, LANES), lambda i, es, ec, nu: (i, 0))],
        out_specs=pl.BlockSpec(memory_space=pl.ANY),
        scratch_shapes=[pltpu.SMEM((2, per_tile), I32),
                        pltpu.VMEM((tm // 2 * nch, LANES), h2.dtype),
                        pltpu.SemaphoreType.DMA((2,)), pltpu.SemaphoreType.DMA, pltpu.SemaphoreType.DMA],
    )
    return pl.pallas_call(
        functools.partial(_dispatch_kernel, tm=tm, nch=nch),
        grid_spec=grid_spec,
        out_shape=jax.ShapeDtypeStruct((n_tiles * tm * nch, LANES), h2.dtype),
        compiler_params=_params(1),
        name="dispatch",
    )(estart, ecnt, n_used, pos_tiles, h2)


GATE_UP_SPLIT = 2


def _gate_up_kernel(stile_ref, shalf_ref, sexp_ref, nsteps_ref, x_ref, wg_ref, wu_ref, bg_ref, bu_ref, act_ref,
                    *, chunk, npair):
    s = pl.program_id(0)

    @pl.when(s < nsteps_ref[0])
    def _():
        tm, ffh = act_ref.shape
        chunks = []
        for p in range(npair):
            chunks += _unpack_pair(x_ref[pl.ds(p, tm, stride=npair), :])
        x = jnp.concatenate(chunks, axis=-1).astype(BF16)
        for c0 in range(0, ffh, chunk):
            cols = slice(c0, c0 + chunk)
            gate = jnp.dot(x, wg_ref[:, cols].astype(BF16), preferred_element_type=F32) + bg_ref[:, cols]
            up = jnp.dot(x, wu_ref[:, cols].astype(BF16), preferred_element_type=F32) + bu_ref[:, cols]
            gate = jnp.minimum(gate, SWIGLU_LIMIT)
            up = jnp.clip(up, -SWIGLU_LIMIT, SWIGLU_LIMIT)
            act = (up + 1.0) * gate * jax.nn.sigmoid(gate * SWIGLU_ALPHA)
            act_ref[:, cols] = act.astype(BF16)

    @pl.when(s >= nsteps_ref[0])
    def _():
        act_ref[...] = jnp.zeros(act_ref.shape, BF16)


def _gate_up_schedule(tile_exp, n_used, first_tile, group_tiles, n_tiles):
    ns = GATE_UP_SPLIT
    s = jnp.arange(ns * n_tiles, dtype=I32)
    e = tile_exp[s // ns]
    first = first_tile[e]
    count = jnp.maximum(group_tiles[e], 1)
    local = s - ns * first
    tile, half = first + local % count, local // count
    n_steps = ns * n_used
    last = jnp.maximum(n_steps - 1, 0)
    live = s < n_steps
    pick = lambda v: jnp.where(live, v, v[last])
    return pick(tile), pick(half), pick(e), n_steps


def _gate_up(xs, w, b, tile_exp, n_used, first_tile, group_tiles, tm):
    n_exp, d, ff2 = w.shape
    nch = d // PAIR
    rows = xs.shape[0] // nch
    ff = ff2 // 2
    n_tiles = rows // tm
    ns = GATE_UP_SPLIT
    ffh = ff // ns
    stile, shalf, sexp, n_steps = _gate_up_schedule(tile_exp, n_used, first_tile, group_tiles, n_tiles)

    def out_index(s, st, sh, se, n):
        live = s < n[0]
        return jnp.where(live, st[s], s // ns), jnp.where(live, sh[s], s % ns)

    b3 = b.reshape(n_exp, 1, ff2)
    grid_spec = pltpu.PrefetchScalarGridSpec(
        num_scalar_prefetch=4,
        grid=(ns * n_tiles,),
        in_specs=[pl.BlockSpec((tm * nch, LANES), lambda s, st, sh, se, n: (st[s], 0)),
                  pl.BlockSpec((None, d, ffh), lambda s, st, sh, se, n: (se[s], 0, sh[s])),
                  pl.BlockSpec((None, d, ffh), lambda s, st, sh, se, n: (se[s], 0, ns + sh[s])),
                  pl.BlockSpec((None, 1, ffh), lambda s, st, sh, se, n: (se[s], 0, sh[s])),
                  pl.BlockSpec((None, 1, ffh), lambda s, st, sh, se, n: (se[s], 0, ns + sh[s]))],
        out_specs=pl.BlockSpec((tm, ffh), out_index),
    )
    return pl.pallas_call(
        functools.partial(_gate_up_kernel, chunk=_tile(ffh, 512), npair=nch),
        grid_spec=grid_spec,
        out_shape=jax.ShapeDtypeStruct((rows, ff), BF16),
        compiler_params=_params(1),
        name="gate_up",
    )(stile, shalf, sexp, n_steps, xs, w, w, b3, b3)


def _down_kernel(texp_ref, nused_ref, a_ref, w_ref, b_ref, y_ref):
    i = pl.program_id(0)

    @pl.when(i < nused_ref[0])
    def _():
        tm = a_ref.shape[0]
        npair = w_ref.shape[1] // PAIR
        a = a_ref[...]
        chunk = _tile(w_ref.shape[1], 512)
        for c0 in range(0, w_ref.shape[1], chunk):
            y = (jnp.dot(a, w_ref[:, c0:c0 + chunk].astype(BF16), preferred_element_type=F32)
                 + b_ref[:, c0:c0 + chunk])
            for q in range(chunk // PAIR):
                y_ref[pl.ds(c0 // PAIR + q, tm, stride=npair), :] = _pack_pair(
                    y[:, 2 * q * LANES:(2 * q + 1) * LANES], y[:, (2 * q + 1) * LANES:(2 * q + 2) * LANES])

    @pl.when(i >= nused_ref[0])
    def _():
        y_ref[...] = jnp.zeros(y_ref.shape, U32)


def _down(act, w, b, tile_exp, n_used, tm):
    rows, ff = act.shape
    n_exp, _, d = w.shape
    nch = d // PAIR
    n_tiles = rows // tm
    live = lambda i, te, nu: jnp.minimum(i, nu[0] - 1)
    grid_spec = pltpu.PrefetchScalarGridSpec(
        num_scalar_prefetch=2,
        grid=(n_tiles,),
        in_specs=[pl.BlockSpec((tm, ff), lambda i, te, nu: (live(i, te, nu), 0)),
                  pl.BlockSpec((None, ff, d), lambda i, te, nu: (te[i], 0, 0)),
                  pl.BlockSpec((None, 1, d), lambda i, te, nu: (te[i], 0, 0))],
        out_specs=pl.BlockSpec((tm * nch, LANES), lambda i, te, nu: (i, 0)),
    )
    return pl.pallas_call(
        _down_kernel,
        grid_spec=grid_spec,
        out_shape=jax.ShapeDtypeStruct(_rows_shape(rows, d), U32),
        compiler_params=_params(1),
        name="down",
    )(tile_exp, n_used, act, w, b.reshape(n_exp, 1, d))


def _combine_kernel(pos_hbm, y_hbm, x1_ref, mod_ref, g_ref, o_ref, pos_smem, ybuf, sem_idx, sem_row):
    i = pl.program_id(0)
    n = pl.num_programs(0)
    tmc, d = x1_ref.shape
    nch = d // PAIR
    slot = i % 2

    def idx_copy(step, s):
        return pltpu.make_async_copy(pos_hbm.at[step], pos_smem.at[s], sem_idx.at[s])

    def start_gathers_static(s):
        def issue(tok, carry):
            dst0 = pl.multiple_of(tok * nch, nch)
            for k in range(TOP_K):
                src0 = pl.multiple_of(pos_smem[s, k * tmc + tok] * nch, nch)
                pltpu.make_async_copy(y_hbm.at[pl.ds(src0, nch)], ybuf.at[s, k, pl.ds(dst0, nch)],
                                      sem_row.at[s]).start(priority=k % 2)
            return carry

        lax.fori_loop(0, tmc, issue, 0, unroll=4)

    def start_gathers(s):
        for static_s in range(2):
            pl.when(s == static_s)(functools.partial(start_gathers_static, static_s))

    @pl.when(i == 0)
    def _():
        idx_copy(0, 0).start()
        idx_copy(0, 0).wait()
        start_gathers_static(0)

        @pl.when(n > 1)
        def _():
            idx_copy(1, 1).start()

    @pl.when(i + 1 < n)
    def _():
        idx_copy(i + 1, 1 - slot).wait()
        start_gathers(1 - slot)

    @pl.when(i + 2 < n)
    def _():
        idx_copy(i + 2, slot).start()

    for k in range(TOP_K):
        pltpu.make_async_copy(y_hbm.at[pl.ds(0, tmc * nch)], ybuf.at[slot, k], sem_row.at[slot]).wait()

    g = g_ref[...]
    gate2 = mod_ref[...][5:6]
    for p in range(nch):
        lo_sum = hi_sum = None
        for k in range(TOP_K):
            lo, hi = _unpack_pair(ybuf[slot, k, pl.ds(p, tmc, stride=nch), :])
            gk = g[:, k:k + 1]
            lo_sum = lo * gk if lo_sum is None else lo_sum + lo * gk
            hi_sum = hi * gk if hi_sum is None else hi_sum + hi * gk
        for half, moe in enumerate((lo_sum, hi_sum)):
            cols = slice((2 * p + half) * LANES, (2 * p + half + 1) * LANES)
            o_ref[:, cols] = x1_ref[:, cols] + gate2[:, cols] * moe


def _combine(y, pos_tiles, x1, mod3, gates_t, seq):
    t, d = x1.shape
    ntt, per_tile = pos_tiles.shape
    tmc = per_tile // TOP_K
    per_b = seq // tmc
    return pl.pallas_call(
        _combine_kernel,
        grid=(ntt,),
        in_specs=[pl.BlockSpec(memory_space=pl.ANY),
                  pl.BlockSpec(memory_space=pl.ANY),
                  pl.BlockSpec((tmc, d), lambda i: (i, 0)),
                  pl.BlockSpec((None, 6, d), lambda i: (i // per_b, 0, 0)),
                  pl.BlockSpec((tmc, TOP_K), lambda i: (i, 0))],
        out_specs=pl.BlockSpec((tmc, d), lambda i: (i, 0)),
        out_shape=jax.ShapeDtypeStruct((t, d), F32),
        scratch_shapes=[pltpu.SMEM((2, per_tile), I32),
                        pltpu.VMEM((2, TOP_K, tmc * (d // PAIR), LANES), U32),
                        pltpu.SemaphoreType.DMA((2,)), pltpu.SemaphoreType.DMA((2,))],
        compiler_params=_params(1),
        name="combine",
    )(pos_tiles, y, x1, mod3, gates_t)


def _moe(h2, logits_t, x1, mod3, w_gu, b_gu, w_dn, b_dn, seq):
    t, d = x1.shape
    n_exp = logits_t.shape[0]
    tm = _tile(t * TOP_K, 512)
    n_tiles = (t * TOP_K) // tm + n_exp
    idx, gates, pos, tmeta, emeta = _route(logits_t, tm, n_tiles)
    del idx
    tok_tile = _tile(seq, 256)
    pos_tiles = (pos.reshape(TOP_K, t // tok_tile, tok_tile).transpose(1, 0, 2)
                 .reshape(t // tok_tile, TOP_K * tok_tile))
    tile_exp = tmeta[0, :n_tiles]
    n_used = tmeta[1, :1]
    xs = _dispatch(h2, pos_tiles, emeta[0, :n_exp], emeta[1, :n_exp], n_used, tm, n_tiles, d // PAIR)
    act = _gate_up(xs, w_gu, b_gu, tile_exp, n_used, emeta[0, :n_exp] // tm, emeta[2, :n_exp] // tm, tm)
    y = _down(act, w_dn, b_dn, tile_exp, n_used, tm)
    return _combine(y, pos_tiles, x1, mod3, gates.T, seq)


def kernel(x, c, w_ada, b_ada, norm1_g, w_in, nat_q_g, nat_k_g, nat_rpb, diff_q_g, diff_k_g, diff_lambda,
           diff_sub_g, rel_bias_table, w_out, norm2_g, router_w, router_b, w_gate_up, b_gate_up, w_down, b_down):
    bsz, seq, d = x.shape
    t = bsz * seq
    ns = (d // 2) // HEAD_DIM
    n_exp = router_w.shape[-1]
    scale = HEAD_DIM ** -0.5
    t5_bias = _t5_bias_by_offset(rel_bias_table, seq)
    ones = jnp.ones((HEAD_DIM,), F32)
    xf = x.reshape(t, d)
    for l in range(w_ada.shape[0]):
        lambda_init = 0.8 - 0.6 * math.exp(-0.3 * l)
        mod3 = _adaln(c, w_ada[l], b_ada[l]).reshape(bsz, 6, d)
        gains = jnp.stack([nat_q_g[l] * scale, nat_k_g[l], ones, diff_q_g[l] * (scale * LOG2_E), diff_k_g[l],
                           ones, ones, ones])
        qkv = _inproj(xf, mod3, norm1_g[l], w_in[l], gains, seq)
        nat = _nat_attention(qkv, nat_rpb[l], bsz, seq, ns)
        dif = _diff_attention(qkv, t5_bias, diff_lambda[l], diff_sub_g[l], lambda_init, bsz, seq, ns)
        rw_pad = jnp.pad(router_w[l], ((0, 0), (0, LANES - n_exp)))
        x1, h2, logits_t = _outproj(nat, dif, xf, mod3, w_out[l], norm2_g[l], rw_pad, router_b[l], seq)
        xf = _moe(h2, logits_t, x1, mod3, w_gate_up[l], b_gate_up[l], w_down[l], b_down[l], seq)
    return xf.reshape(bsz, seq, d)
```

```python
import functools
import math

import jax
import jax.numpy as jnp
from jax import lax
from jax.experimental import pallas as pl
from jax.experimental.pallas import tpu as pltpu

HEAD_DIM = 128
GRID_W = 64
NAT_KR_MAX = 8
NAT_KC = 16
N_BUCKETS = 32
MAX_DISTANCE = 128
TOP_K = 4
SWIGLU_LIMIT = 7.0
SWIGLU_ALPHA = 1.702
RMS_EPS = 1e-6
NEG_INF = -1e30
LOG2_E = 1.4426950408889634

LANES = 128
PAIR = 2 * LANES
V7X_VMEM_LIMIT_BYTES = 56 * 1024 * 1024

F32 = jnp.float32
BF16 = jnp.bfloat16
I32 = jnp.int32
U32 = jnp.uint32


def _tile(n, pref):
    t = min(n, pref)
    assert n % t == 0, (n, pref)
    return t


def _params(n_axes):
    return pltpu.CompilerParams(dimension_semantics=("arbitrary",) * n_axes,
                                vmem_limit_bytes=V7X_VMEM_LIMIT_BYTES)


def _adaln_kernel(c_ref, w_ref, b_ref, o_ref):
    c = c_ref[...]
    a = (c * jax.nn.sigmoid(c)).astype(BF16)
    o_ref[...] = jnp.dot(a, w_ref[...].astype(BF16), preferred_element_type=F32) + b_ref[...]


def _adaln(c, w, b):
    bsz, d = c.shape
    n = w.shape[1]
    tn = _tile(n, 1024)
    return pl.pallas_call(
        _adaln_kernel,
        grid=(n // tn,),
        in_specs=[pl.BlockSpec((bsz, d), lambda j: (0, 0)),
                  pl.BlockSpec((d, tn), lambda j: (0, j)),
                  pl.BlockSpec((1, tn), lambda j: (0, j))],
        out_specs=pl.BlockSpec((bsz, tn), lambda j: (0, j)),
        out_shape=jax.ShapeDtypeStruct((bsz, n), F32),
        compiler_params=_params(1),
        name="adaln",
    )(c, w, b.reshape(1, n))


def _inproj_kernel(x_ref, mod_ref, g_ref, w_ref, gain_ref, o_ref, h_scr):
    j = pl.program_id(1)

    @pl.when(j == 0)
    def _():
        x = x_ref[...]
        ms = jnp.mean(x * x, axis=-1, keepdims=True)
        y = x * lax.rsqrt(ms + RMS_EPS) * g_ref[...]
        m = mod_ref[...]
        h_scr[...] = (y * (1.0 + m[1:2]) + m[0:1]).astype(BF16)

    heads = o_ref.shape[0]
    per = min(heads, PAIR // HEAD_DIM)
    is_value = (j == 2) | (j == 5)
    g = gain_ref[pl.ds(j, 1), :]
    h = h_scr[...]
    for c0 in range(0, heads, per):
        y = jnp.dot(h, w_ref[:, c0 * HEAD_DIM:(c0 + per) * HEAD_DIM].astype(BF16), preferred_element_type=F32)
        for hh in range(per):
            yh = y[:, hh * HEAD_DIM:(hh + 1) * HEAD_DIM]
            ms = jnp.mean(yh * yh, axis=-1, keepdims=True)
            factor = jnp.where(is_value, 1.0, lax.rsqrt(ms + RMS_EPS))
            o_ref[c0 + hh] = (yh * factor * g).astype(BF16)


def _inproj(xf, mod3, g1, w, gains, seq):
    t, d = xf.shape
    seg = d // 2
    ns = seg // HEAD_DIM
    tm = _tile(seq, 1024)
    per_b = seq // tm
    return pl.pallas_call(
        _inproj_kernel,
        grid=(t // tm, 6),
        in_specs=[pl.BlockSpec((tm, d), lambda i, j: (i, 0)),
                  pl.BlockSpec((None, 6, d), lambda i, j: (i // per_b, 0, 0)),
                  pl.BlockSpec((1, d), lambda i, j: (0, 0)),
                  pl.BlockSpec((d, seg), lambda i, j: (0, j)),
                  pl.BlockSpec((8, HEAD_DIM), lambda i, j: (0, 0))],
        out_specs=pl.BlockSpec((ns, tm, HEAD_DIM), lambda i, j: (j, i, 0)),
        out_shape=jax.ShapeDtypeStruct((6 * ns, t, HEAD_DIM), BF16),
        scratch_shapes=[pltpu.VMEM((tm, d), BF16)],
        compiler_params=_params(2),
        name="inproj",
    )(xf, mod3, g1.reshape(1, d), w, gains)


def _nat_kernel(q_ref, k_ref, v_ref, w_ref, o_ref, b_ref, *, rows, kr, group):
    band = kr * GRID_W

    @pl.when(pl.program_id(1) == 0)
    def _():
        qc = lax.broadcasted_iota(I32, (GRID_W, LANES), 0)
        lane = lax.broadcasted_iota(I32, (GRID_W, LANES), 1)
        kc = lane % GRID_W
        col_start = jnp.clip(qc - NAT_KC // 2, 0, GRID_W - NAT_KC)
        visible = (kc >= col_start) & (kc < col_start + NAT_KC)
        for d in range(kr):
            for j in range(0, kr, 2):
                def tile(jj, shift):
                    ro = jj - d + (NAT_KR_MAX - 1)
                    row = jnp.broadcast_to(w_ref[ro:ro + 1, :], (GRID_W, LANES))
                    return pltpu.roll(row, shift, 1, stride=1, stride_axis=0)
                both = jnp.where(lane < GRID_W, tile(j, 0), tile(j + 1, GRID_W))
                b_ref[d, :, j * GRID_W:(j + 2) * GRID_W] = jnp.where(visible, both, NEG_INF)

    def body(g, carry):
        k0s, scores = [], []
        for i in range(group):
            r = g * group + i
            rs = jnp.clip(r - kr // 2, 0, rows - kr)
            k0 = pl.multiple_of(rs * GRID_W, GRID_W)
            q = q_ref[pl.ds(pl.multiple_of(r * GRID_W, GRID_W), GRID_W), :]
            s = lax.dot_general(q, k_ref[pl.ds(k0, band), :], (((1,), (1,)), ((), ())),
                                preferred_element_type=F32)
            k0s.append(k0)
            scores.append(s + b_ref[r - rs])
        s = jnp.concatenate(scores, axis=0)
        p = jnp.exp(s - jnp.max(s, axis=-1, keepdims=True))
        den = jnp.sum(p, axis=-1, keepdims=True)
        pb = p.astype(BF16)
        outs = [jnp.dot(pb[i * GRID_W:(i + 1) * GRID_W], v_ref[pl.ds(k0s[i], band), :],
                        preferred_element_type=F32) for i in range(group)]
        o = jnp.concatenate(outs, axis=0) / den
        o_ref[pl.ds(pl.multiple_of(g * (group * GRID_W), group * GRID_W), group * GRID_W), :] = o.astype(BF16)
        return carry

    lax.fori_loop(0, rows // group, body, 0)


def _nat_bias_by_offset(rpb):
    lane = jnp.arange(LANES)
    off = jnp.where(lane < LANES // 2, lane, lane - LANES)
    return jnp.take(rpb.astype(F32), jnp.clip(off, -(NAT_KC - 1), NAT_KC - 1) + (NAT_KC - 1), axis=-1)


def _nat_attention(qkv, rpb, bsz, seq, ns):
    t = qkv.shape[1]
    rows = seq // GRID_W
    kr = min(NAT_KR_MAX, rows)
    assert kr % 2 == 0 and 2 * GRID_W == LANES
    blk = (None, seq, HEAD_DIM)
    n_off = 2 * NAT_KR_MAX - 1
    return pl.pallas_call(
        functools.partial(_nat_kernel, rows=rows, kr=kr, group=_tile(rows, 16)),
        grid=(ns, bsz),
        in_specs=[pl.BlockSpec(blk, lambda h, b: (h, b, 0)),
                  pl.BlockSpec(blk, lambda h, b: (ns + h, b, 0)),
                  pl.BlockSpec(blk, lambda h, b: (2 * ns + h, b, 0)),
                  pl.BlockSpec((None, n_off, LANES), lambda h, b: (h, 0, 0))],
        out_specs=pl.BlockSpec(blk, lambda h, b: (h, b, 0)),
        out_shape=jax.ShapeDtypeStruct((ns, t, HEAD_DIM), BF16),
        scratch_shapes=[pltpu.VMEM((kr, GRID_W, kr * GRID_W), F32)],
        compiler_params=_params(2),
        name="nat_attn",
    )(qkv, qkv, qkv, _nat_bias_by_offset(rpb))


def _t5_bucket(rel):
    nb = N_BUCKETS // 2
    max_exact = nb // 2
    ret = jnp.where(rel > 0, nb, 0)
    n = jnp.abs(rel)
    nf = jnp.maximum(n, 1).astype(F32)
    large = max_exact + (jnp.log(nf / max_exact) / math.log(MAX_DISTANCE / max_exact)
                         * (nb - max_exact)).astype(I32)
    large = jnp.minimum(large, nb - 1)
    return ret + jnp.where(n < max_exact, n, large)


def _t5_bias_by_offset(rel_table, seq):
    rel = jnp.arange(2 * seq, dtype=I32) - seq
    hot = (_t5_bucket(rel)[None, :] == jnp.arange(N_BUCKETS, dtype=I32)[:, None]).astype(F32)
    u = jnp.dot(rel_table.astype(F32).T, hot, precision=lax.Precision.HIGHEST) * LOG2_E
    return u.reshape(u.shape[0], 1, 2 * seq)


def _diff_kernel(q_ref, k_ref, v_ref, u_ref, lam_ref, sg_ref, o_ref, strip_scr, *, seq, tq, nq, n_split,
                 lambda_init):
    qi = pl.program_id(2)

    @pl.when((pl.program_id(1) == 0) & (qi == 0))
    def _():
        strip_scr[...] = pltpu.roll(jnp.broadcast_to(u_ref[...], strip_scr.shape), 0, 1, stride=1, stride_axis=0)

    bias = strip_scr[:, pl.ds(pl.multiple_of((nq - qi) * tq, tq), seq)]
    lam = lam_ref[...]
    lam_full = (jnp.exp(jnp.sum(lam[0:1] * lam[1:2], axis=-1, keepdims=True))
                - jnp.exp(jnp.sum(lam[2:3] * lam[3:4], axis=-1, keepdims=True)) + lambda_init)

    v = jnp.concatenate([v_ref[0], v_ref[1]], axis=-1)
    half = tq // n_split

    def scores(p, rows):
        s = lax.dot_general(q_ref[p, rows, :], k_ref[p], (((1,), (1,)), ((), ())), preferred_element_type=F32)
        return s + bias[rows]

    def attend(s1, s2):
        e1 = jnp.exp2(s1 - jnp.max(s1, axis=-1, keepdims=True))
        e2 = jnp.exp2(s2 - jnp.max(s2, axis=-1, keepdims=True))
        r1 = 1.0 / jnp.sum(e1, axis=-1, keepdims=True)
        r2 = lam_full / jnp.sum(e2, axis=-1, keepdims=True)
        o = jnp.dot((e1 * r1 - e2 * r2).astype(BF16), v, preferred_element_type=F32)
        ms = jnp.mean(o * o, axis=-1, keepdims=True)
        return o * lax.rsqrt(ms + RMS_EPS) * sg_ref[...] * (1.0 - lambda_init)

    blocks = [slice(j * half, (j + 1) * half) for j in range(n_split)]
    all_scores = [(scores(0, rows), scores(1, rows)) for rows in blocks]
    for rows, (s1, s2) in zip(blocks, all_scores):
        o = attend(s1, s2)
        o_ref[0, rows, :] = o[:, :HEAD_DIM].astype(BF16)
        o_ref[1, rows, :] = o[:, HEAD_DIM:].astype(BF16)


def _diff_attention(qkv, bias_by_offset, lam, sub_g, lambda_init, bsz, seq, ns):
    t = qkv.shape[1]
    hd = ns // 2
    tq = _tile(seq, 512)
    nq = seq // tq
    qb, kb, vb = 3 * ns // 2, 4 * ns // 2, 5 * ns // 2
    return pl.pallas_call(
        functools.partial(_diff_kernel, seq=seq, tq=tq, nq=nq, n_split=tq // 128, lambda_init=lambda_init),
        grid=(hd, bsz, nq),
        in_specs=[pl.BlockSpec((2, tq, HEAD_DIM), lambda h, b, i: (qb + h, b * nq + i, 0)),
                  pl.BlockSpec((2, seq, HEAD_DIM), lambda h, b, i: (kb + h, b, 0)),
                  pl.BlockSpec((2, seq, HEAD_DIM), lambda h, b, i: (vb + h, b, 0)),
                  pl.BlockSpec((None, 1, 2 * seq), lambda h, b, i: (h, 0, 0)),
                  pl.BlockSpec((4, HEAD_DIM), lambda h, b, i: (0, 0)),
                  pl.BlockSpec((1, 2 * HEAD_DIM), lambda h, b, i: (0, 0))],
        out_specs=pl.BlockSpec((2, tq, HEAD_DIM), lambda h, b, i: (h, b * nq + i, 0)),
        out_shape=jax.ShapeDtypeStruct((ns, t, HEAD_DIM), BF16),
        scratch_shapes=[pltpu.VMEM((tq, 2 * seq), F32)],
        compiler_params=_params(3),
        name="diff_attn",
    )(qkv, qkv, qkv, bias_by_offset, lam, sub_g.reshape(1, 2 * HEAD_DIM))


def _rows_shape(rows, d):
    return (rows * (d // PAIR), LANES)


def _pack_pair(lo, hi):
    lo_w = lax.bitcast_convert_type(lo.astype(BF16).astype(F32), U32) >> 16
    hi_w = lax.bitcast_convert_type(hi.astype(BF16).astype(F32), U32) & jnp.uint32(0xFFFF0000)
    return lo_w | hi_w


def _unpack_pair(w):
    return (lax.bitcast_convert_type(w << 16, F32),
            lax.bitcast_convert_type(w & jnp.uint32(0xFFFF0000), F32))


def _split_bf16(v):
    hi = v.astype(BF16)
    lo = (v - hi.astype(F32)).astype(BF16)
    return hi, lo


def _outproj_kernel(nat_ref, dif_ref, x_ref, mod_ref, w_ref, g_ref, rw_ref, rb_ref,
                    x1_ref, h2_ref, lt_ref):
    ns = nat_ref.shape[0]
    a = jnp.concatenate([nat_ref[hh] for hh in range(ns)] + [dif_ref[hh] for hh in range(ns)], axis=-1)
    mix = jnp.dot(a, w_ref[...].astype(BF16), preferred_element_type=F32)
    m = mod_ref[...]
    x1 = x_ref[...] + m[2:3] * mix
    x1_ref[...] = x1
    ms = jnp.mean(x1 * x1, axis=-1, keepdims=True)
    h2 = x1 * lax.rsqrt(ms + RMS_EPS) * g_ref[...] * (1.0 + m[4:5]) + m[3:4]
    npair = h2.shape[1] // PAIR
    for p in range(npair):
        h2_ref[pl.ds(p, h2.shape[0], stride=npair), :] = _pack_pair(
            h2[:, 2 * p * LANES:(2 * p + 1) * LANES], h2[:, (2 * p + 1) * LANES:(2 * p + 2) * LANES])
    h_hi, h_lo = _split_bf16(h2)
    w_hi, w_lo = _split_bf16(rw_ref[...])
    both = jnp.dot(h_hi, jnp.concatenate([w_hi, w_lo], axis=-1), preferred_element_type=F32)
    lg = both[:, :LANES] + both[:, LANES:] + jnp.dot(h_lo, w_hi, preferred_element_type=F32)
    n_exp = lt_ref.shape[0]
    lt_ref[...] = lg.T[:n_exp] + rb_ref[...]


def _outproj(nat, dif, xf, mod3, w, g2, rw_pad, rb, seq):
    t, d = xf.shape
    ns = nat.shape[0]
    n_exp = rb.shape[0]
    tm = _tile(seq, 512)
    per_b = seq // tm
    return pl.pallas_call(
        _outproj_kernel,
        grid=(t // tm,),
        in_specs=[pl.BlockSpec((ns, tm, HEAD_DIM), lambda i: (0, i, 0)),
                  pl.BlockSpec((ns, tm, HEAD_DIM), lambda i: (0, i, 0)),
                  pl.BlockSpec((tm, d), lambda i: (i, 0)),
                  pl.BlockSpec((None, 6, d), lambda i: (i // per_b, 0, 0)),
                  pl.BlockSpec((d, d), lambda i: (0, 0), pipeline_mode=pl.Buffered(1)),
                  pl.BlockSpec((1, d), lambda i: (0, 0)),
                  pl.BlockSpec((d, LANES), lambda i: (0, 0)),
                  pl.BlockSpec((n_exp, 1), lambda i: (0, 0))],
        out_specs=[pl.BlockSpec((tm, d), lambda i: (i, 0)),
                   pl.BlockSpec((tm * (d // PAIR), LANES), lambda i: (i, 0)),
                   pl.BlockSpec((n_exp, tm), lambda i: (0, i))],
        out_shape=[jax.ShapeDtypeStruct((t, d), F32),
                   jax.ShapeDtypeStruct(_rows_shape(t, d), U32),
                   jax.ShapeDtypeStruct((n_exp, t), F32)],
        compiler_params=_params(1),
        name="outproj",
    )(nat, dif, xf, mod3, w, g2.reshape(1, d), rw_pad, rb.reshape(n_exp, 1))


def _route_kernel(lt_ref, idx_ref, gate_ref, pos_ref, tmeta_ref, emeta_ref, rank_scr, *, tm, tb):
    n_exp, t = lt_ref.shape
    ntp = tmeta_ref.shape[1]
    eidx = lax.broadcasted_iota(I32, (n_exp, tb), 0)
    tri = (lax.broadcasted_iota(I32, (tb, tb), 0) < lax.broadcasted_iota(I32, (tb, tb), 1)).astype(BF16)

    def pass1(jb, counts):
        off = pl.multiple_of(jb * tb, tb)
        l = lt_ref[:, pl.ds(off, tb)]
        vals, sels, hots = [], [], []
        for _ in range(TOP_K):
            m = jnp.max(l, axis=0, keepdims=True)
            sel = jnp.min(jnp.where(l == m, eidx, n_exp), axis=0, keepdims=True)
            hot = eidx == sel
            vals.append(m)
            sels.append(sel)
            hots.append(hot)
            l = jnp.where(hot, -jnp.inf, l)
        exps = [jnp.exp(v - vals[0]) for v in vals]
        den = exps[0]
        for e in exps[1:]:
            den = den + e
        member = hots[0].astype(F32)
        for hot in hots[1:]:
            member = member + hot.astype(F32)
        before = jnp.dot(member.astype(BF16), tri, preferred_element_type=F32) + counts
        for k in range(TOP_K):
            rank = jnp.sum(jnp.where(hots[k], before, 0.0), axis=0, keepdims=True)
            idx_ref[pl.ds(k, 1), pl.ds(off, tb)] = sels[k]
            gate_ref[pl.ds(k, 1), pl.ds(off, tb)] = exps[k] / den
            rank_scr[pl.ds(k, 1), pl.ds(off, tb)] = rank.astype(I32)
        return counts + jnp.sum(member, axis=1, keepdims=True)

    counts = lax.fori_loop(0, t // tb, pass1, jnp.zeros((n_exp, 1), F32))

    padded = jnp.ceil(counts * (1.0 / tm)) * tm
    er = lax.broadcasted_iota(I32, (n_exp, LANES), 0)
    ec = lax.broadcasted_iota(I32, (n_exp, LANES), 1)

    def to_lanes(col):
        return jnp.sum(jnp.where(er == ec, col, 0.0), axis=0, keepdims=True)

    start = jnp.sum(jnp.where(ec < er, to_lanes(padded), 0.0), axis=1, keepdims=True)
    end = start + padded

    def pass2(jb, carry):
        off = pl.multiple_of(jb * tb, tb)
        for k in range(TOP_K):
            sel = idx_ref[pl.ds(k, 1), pl.ds(off, tb)]
            st = jnp.sum(jnp.where(eidx == sel, start, 0.0), axis=0, keepdims=True)
            pos_ref[pl.ds(k, 1), pl.ds(off, tb)] = st.astype(I32) + rank_scr[pl.ds(k, 1), pl.ds(off, tb)]
        return carry

    lax.fori_loop(0, t // tb, pass2, 0)

    tile_row = (lax.broadcasted_iota(I32, (n_exp, ntp), 1) * tm).astype(F32)
    tile_exp = jnp.sum((end <= tile_row).astype(F32), axis=0, keepdims=True)
    tile_exp = jnp.minimum(tile_exp, n_exp - 1.0)
    n_used = jnp.sum(padded, axis=0, keepdims=True) * (1.0 / tm)
    tmeta_ref[...] = jnp.zeros(tmeta_ref.shape, I32)
    tmeta_ref[0:1, :] = tile_exp.astype(I32)
    tmeta_ref[1:2, :] = jnp.broadcast_to(n_used, (1, ntp)).astype(I32)
    emeta_ref[...] = jnp.zeros(emeta_ref.shape, I32)
    emeta_ref[0:1, :] = to_lanes(start).astype(I32)
    emeta_ref[1:2, :] = to_lanes(counts).astype(I32)
    emeta_ref[2:3, :] = to_lanes(padded).astype(I32)


def _route(logits_t, tm, n_tiles):
    n_exp, t = logits_t.shape
    tb = _tile(t, 512)
    ntp = pl.cdiv(n_tiles, LANES) * LANES
    full = lambda shape: pl.BlockSpec(shape, lambda: (0,) * len(shape))
    return pl.pallas_call(
        functools.partial(_route_kernel, tm=tm, tb=tb),
        in_specs=[full((n_exp, t))],
        out_specs=[full((TOP_K, t)), full((TOP_K, t)), full((TOP_K, t)), full((8, ntp)), full((8, LANES))],
        out_shape=[jax.ShapeDtypeStruct((TOP_K, t), I32),
                   jax.ShapeDtypeStruct((TOP_K, t), F32),
                   jax.ShapeDtypeStruct((TOP_K, t), I32),
                   jax.ShapeDtypeStruct((8, ntp), I32),
                   jax.ShapeDtypeStruct((8, LANES), I32)],
        scratch_shapes=[pltpu.VMEM((TOP_K, t), I32)],
        compiler_params=pltpu.CompilerParams(vmem_limit_bytes=V7X_VMEM_LIMIT_BYTES),
        name="route",
    )(logits_t)


def _dispatch_kernel(estart_ref, ecnt_ref, nused_ref, pos_hbm, h_ref, xs_hbm, pos_smem, zero_scr,
                     sem_idx, sem_row, sem_zero, *, tm, nch):
    i = pl.program_id(0)
    tmd = h_ref.shape[0] // nch
    n_exp = estart_ref.shape[0]
    n_tiles = xs_hbm.shape[0] // (tm * nch)
    slot = i % 2

    def idx_copy(step, s):
        return pltpu.make_async_copy(pos_hbm.at[step], pos_smem.at[s], sem_idx.at[s])

    @pl.when(i == 0)
    def _():
        idx_copy(0, 0).start()
        zero_scr[...] = jnp.zeros(zero_scr.shape, zero_scr.dtype)

        def fill(e, carry):
            cnt = ecnt_ref[e]
            pad = (tm - cnt % tm) % tm
            base = estart_ref[e] + cnt
            size = tm // 2
            while size >= 1:
                off = pad & ~(2 * size - 1)

                @pl.when((pad & size) != 0)
                def _(size=size, off=off):
                    dst0 = pl.multiple_of((base + off) * nch, nch)
                    cp = pltpu.make_async_copy(zero_scr.at[pl.ds(0, size * nch)],
                                               xs_hbm.at[pl.ds(dst0, size * nch)], sem_zero)
                    cp.start()
                    cp.wait()
                size //= 2
            return carry

        lax.fori_loop(0, n_exp, fill, 0)

        def tail_copy(j, part):
            dst0 = pl.multiple_of((j * tm + part * (tm // 2)) * nch, nch)
            return pltpu.make_async_copy(zero_scr, xs_hbm.at[pl.ds(dst0, tm // 2 * nch)], sem_zero)

        def tail_start(j, carry):
            tail_copy(j, 0).start()
            tail_copy(j, 1).start()
            return carry

        def tail_wait(j, carry):
            tail_copy(j, 0).wait()
            tail_copy(j, 1).wait()
            return carry

        lax.fori_loop(nused_ref[0], n_tiles, tail_start, 0)
        lax.fori_loop(nused_ref[0], n_tiles, tail_wait, 0)

    @pl.when(i + 1 < pl.num_programs(0))
    def _():
        idx_copy(i + 1, 1 - slot).start()

    idx_copy(i, slot).wait()

    def start_rows(s):
        def issue(tok, carry):
            src0 = pl.multiple_of(tok * nch, nch)
            for k in range(TOP_K):
                dst0 = pl.multiple_of(pos_smem[s, k * tmd + tok] * nch, nch)
                pltpu.make_async_copy(h_ref.at[pl.ds(src0, nch)], xs_hbm.at[pl.ds(dst0, nch)],
                                      sem_row).start(priority=k % 2)
            return carry

        lax.fori_loop(0, tmd, issue, 0, unroll=4)

    for s in range(2):
        pl.when(slot == s)(functools.partial(start_rows, s))
    for _ in range(TOP_K):
        pltpu.make_async_copy(h_ref, xs_hbm.at[pl.ds(0, tmd * nch)], sem_row).wait()


def _dispatch(h2, pos_tiles, estart, ecnt, n_used, tm, n_tiles, nch):
    ntt, per_tile = pos_tiles.shape
    tmd = per_tile // TOP_K
    grid_spec = pltpu.PrefetchScalarGridSpec(
        num_scalar_prefetch=3,
        grid=(ntt,),
        in_specs=[pl.BlockSpec(memory_space=pl.ANY),
                  pl.BlockSpec((tmd * nch, LANES), lambda i, es, ec, nu: (i, 0))],
        out_specs=pl.BlockSpec(memory_space=pl.ANY),
        scratch_shapes=[pltpu.SMEM((2, per_tile), I32),
                        pltpu.VMEM((tm // 2 * nch, LANES), h2.dtype),
                        pltpu.SemaphoreType.DMA((2,)), pltpu.SemaphoreType.DMA, pltpu.SemaphoreType.DMA],
    )
    return pl.pallas_call(
        functools.partial(_dispatch_kernel, tm=tm, nch=nch),
        grid_spec=grid_spec,
        out_shape=jax.ShapeDtypeStruct((n_tiles * tm * nch, LANES), h2.dtype),
        compiler_params=_params(1),
        name="dispatch",
    )(estart, ecnt, n_used, pos_tiles, h2)


GATE_UP_SPLIT = 2


def _gate_up_kernel(stile_ref, shalf_ref, sexp_ref, nsteps_ref, x_ref, wg_ref, wu_ref, bg_ref, bu_ref, act_ref,
                    *, chunk, npair):
    s = pl.program_id(0)

    @pl.when(s < nsteps_ref[0])
    def _():
        tm, ffh = act_ref.shape
        chunks = []
        for p in range(npair):
            chunks += _unpack_pair(x_ref[pl.ds(p, tm, stride=npair), :])
        x = jnp.concatenate(chunks, axis=-1).astype(BF16)
        for c0 in range(0, ffh, chunk):
            cols = slice(c0, c0 + chunk)
            gate = jnp.dot(x, wg_ref[:, cols].astype(BF16), preferred_element_type=F32) + bg_ref[:, cols]
            up = jnp.dot(x, wu_ref[:, cols].astype(BF16), preferred_element_type=F32) + bu_ref[:, cols]
            gate = jnp.minimum(gate, SWIGLU_LIMIT)
            up = jnp.clip(up, -SWIGLU_LIMIT, SWIGLU_LIMIT)
            act = (up + 1.0) * gate * jax.nn.sigmoid(gate * SWIGLU_ALPHA)
            act_ref[:, cols] = act.astype(BF16)

    @pl.when(s >= nsteps_ref[0])
    def _():
        act_ref[...] = jnp.zeros(act_ref.shape, BF16)


def _gate_up_schedule(n_used, first_tile, group_tiles, n_tiles):
    assert GATE_UP_SPLIT == 2
    n_steps = 2 * n_used
    s = jnp.minimum(jnp.arange(2 * n_tiles, dtype=I32), jnp.maximum(n_steps - 1, 0))
    ends = 2 * (first_tile + group_tiles)
    e = jnp.minimum(jnp.sum((s[:, None] >= ends[None, :]).astype(I32), axis=1), first_tile.shape[0] - 1)
    hot = e[:, None] == jnp.arange(first_tile.shape[0], dtype=I32)[None, :]
    first = jnp.sum(jnp.where(hot, first_tile[None, :], 0), axis=1)
    count = jnp.sum(jnp.where(hot, group_tiles[None, :], 0), axis=1)
    local = s - 2 * first
    half = (local >= count).astype(I32)
    return first + local - half * count, half, e, n_steps


def _gate_up(xs, w, b, n_used, first_tile, group_tiles, tm):
    n_exp, d, ff2 = w.shape
    nch = d // PAIR
    rows = xs.shape[0] // nch
    ff = ff2 // 2
    n_tiles = rows // tm
    ns = GATE_UP_SPLIT
    ffh = ff // ns
    stile, shalf, sexp, n_steps = _gate_up_schedule(n_used, first_tile, group_tiles, n_tiles)

    def out_index(s, st, sh, se, n):
        live = s < n[0]
        return jnp.where(live, st[s], s // ns), jnp.where(live, sh[s], s % ns)

    b3 = b.reshape(n_exp, 1, ff2)
    grid_spec = pltpu.PrefetchScalarGridSpec(
        num_scalar_prefetch=4,
        grid=(ns * n_tiles,),
        in_specs=[pl.BlockSpec((tm * nch, LANES), lambda s, st, sh, se, n: (st[s], 0)),
                  pl.BlockSpec((None, d, ffh), lambda s, st, sh, se, n: (se[s], 0, sh[s])),
                  pl.BlockSpec((None, d, ffh), lambda s, st, sh, se, n: (se[s], 0, ns + sh[s])),
                  pl.BlockSpec((None, 1, ffh), lambda s, st, sh, se, n: (se[s], 0, sh[s])),
                  pl.BlockSpec((None, 1, ffh), lambda s, st, sh, se, n: (se[s], 0, ns + sh[s]))],
        out_specs=pl.BlockSpec((tm, ffh), out_index),
    )
    return pl.pallas_call(
        functools.partial(_gate_up_kernel, chunk=_tile(ffh, 512), npair=nch),
        grid_spec=grid_spec,
        out_shape=jax.ShapeDtypeStruct((rows, ff), BF16),
        compiler_params=_params(1),
        name="gate_up",
    )(stile, shalf, sexp, n_steps, xs, w, w, b3, b3)


def _down_kernel(texp_ref, nused_ref, a_ref, w_ref, b_ref, y_ref):
    i = pl.program_id(0)

    @pl.when(i < nused_ref[0])
    def _():
        tm = a_ref.shape[0]
        npair = w_ref.shape[1] // PAIR
        a = a_ref[...]
        chunk = _tile(w_ref.shape[1], 512)
        for c0 in range(0, w_ref.shape[1], chunk):
            y = (jnp.dot(a, w_ref[:, c0:c0 + chunk].astype(BF16), preferred_element_type=F32)
                 + b_ref[:, c0:c0 + chunk])
            for q in range(chunk // PAIR):
                y_ref[pl.ds(c0 // PAIR + q, tm, stride=npair), :] = _pack_pair(
                    y[:, 2 * q * LANES:(2 * q + 1) * LANES], y[:, (2 * q + 1) * LANES:(2 * q + 2) * LANES])

    @pl.when(i >= nused_ref[0])
    def _():
        y_ref[...] = jnp.zeros(y_ref.shape, U32)


def _down(act, w, b, tile_exp, n_used, tm):
    rows, ff = act.shape
    n_exp, _, d = w.shape
    nch = d // PAIR
    n_tiles = rows // tm
    live = lambda i, te, nu: jnp.minimum(i, nu[0] - 1)
    grid_spec = pltpu.PrefetchScalarGridSpec(
        num_scalar_prefetch=2,
        grid=(n_tiles,),
        in_specs=[pl.BlockSpec((tm, ff), lambda i, te, nu: (live(i, te, nu), 0)),
                  pl.BlockSpec((None, ff, d), lambda i, te, nu: (te[i], 0, 0)),
                  pl.BlockSpec((None, 1, d), lambda i, te, nu: (te[i], 0, 0))],
        out_specs=pl.BlockSpec((tm * nch, LANES), lambda i, te, nu: (i, 0)),
    )
    return pl.pallas_call(
        _down_kernel,
        grid_spec=grid_spec,
        out_shape=jax.ShapeDtypeStruct(_rows_shape(rows, d), U32),
        compiler_params=_params(1),
        name="down",
    )(tile_exp, n_used, act, w, b.reshape(n_exp, 1, d))


def _combine_kernel(pos_hbm, y_hbm, x1_ref, mod_ref, g_ref, o_ref, pos_smem, ybuf, sem_idx, sem_row):
    i = pl.program_id(0)
    n = pl.num_programs(0)
    tmc, d = x1_ref.shape
    nch = d // PAIR
    slot = i % 2

    def idx_copy(step, s):
        return pltpu.make_async_copy(pos_hbm.at[step], pos_smem.at[s], sem_idx.at[s])

    def start_gathers_static(s):
        def issue(tok, carry):
            dst0 = pl.multiple_of(tok * nch, nch)
            for k in range(TOP_K):
                src0 = pl.multiple_of(pos_smem[s, k * tmc + tok] * nch, nch)
                pltpu.make_async_copy(y_hbm.at[pl.ds(src0, nch)], ybuf.at[s, k, pl.ds(dst0, nch)],
                                      sem_row.at[s]).start(priority=k % 2)
            return carry

        lax.fori_loop(0, tmc, issue, 0, unroll=4)

    def start_gathers(s):
        for static_s in range(2):
            pl.when(s == static_s)(functools.partial(start_gathers_static, static_s))

    @pl.when(i == 0)
    def _():
        idx_copy(0, 0).start()
        idx_copy(0, 0).wait()
        start_gathers_static(0)

        @pl.when(n > 1)
        def _():
            idx_copy(1, 1).start()

    @pl.when(i + 1 < n)
    def _():
        idx_copy(i + 1, 1 - slot).wait()
        start_gathers(1 - slot)

    @pl.when(i + 2 < n)
    def _():
        idx_copy(i + 2, slot).start()

    for k in range(TOP_K):
        pltpu.make_async_copy(y_hbm.at[pl.ds(0, tmc * nch)], ybuf.at[slot, k], sem_row.at[slot]).wait()

    g = g_ref[...]
    gate2 = mod_ref[...][5:6]
    for p in range(nch):
        lo_sum = hi_sum = None
        for k in range(TOP_K):
            lo, hi = _unpack_pair(ybuf[slot, k, pl.ds(p, tmc, stride=nch), :])
            gk = g[:, k:k + 1]
            lo_sum = lo * gk if lo_sum is None else lo_sum + lo * gk
            hi_sum = hi * gk if hi_sum is None else hi_sum + hi * gk
        for half, moe in enumerate((lo_sum, hi_sum)):
            cols = slice((2 * p + half) * LANES, (2 * p + half + 1) * LANES)
            o_ref[:, cols] = x1_ref[:, cols] + gate2[:, cols] * moe


def _combine(y, pos_tiles, x1, mod3, gates_t, seq):
    t, d = x1.shape
    ntt, per_tile = pos_tiles.shape
    tmc = per_tile // TOP_K
    per_b = seq // tmc
    return pl.pallas_call(
        _combine_kernel,
        grid=(ntt,),
        in_specs=[pl.BlockSpec(memory_space=pl.ANY),
                  pl.BlockSpec(memory_space=pl.ANY),
                  pl.BlockSpec((tmc, d), lambda i: (i, 0)),
                  pl.BlockSpec((None, 6, d), lambda i: (i // per_b, 0, 0)),
                  pl.BlockSpec((tmc, TOP_K), lambda i: (i, 0))],
        out_specs=pl.BlockSpec((tmc, d), lambda i: (i, 0)),
        out_shape=jax.ShapeDtypeStruct((t, d), F32),
        scratch_shapes=[pltpu.SMEM((2, per_tile), I32),
                        pltpu.VMEM((2, TOP_K, tmc * (d // PAIR), LANES), U32),
                        pltpu.SemaphoreType.DMA((2,)), pltpu.SemaphoreType.DMA((2,))],
        compiler_params=_params(1),
        name="combine",
    )(pos_tiles, y, x1, mod3, gates_t)


def _moe(h2, logits_t, x1, mod3, w_gu, b_gu, w_dn, b_dn, seq):
    t, d = x1.shape
    n_exp = logits_t.shape[0]
    tm = _tile(t * TOP_K, 512)
    n_tiles = (t * TOP_K) // tm + n_exp
    idx, gates, pos, tmeta, emeta = _route(logits_t, tm, n_tiles)
    del idx
    tok_tile = _tile(seq, 256)
    pos_tiles = (pos.reshape(TOP_K, t // tok_tile, tok_tile).transpose(1, 0, 2)
                 .reshape(t // tok_tile, TOP_K * tok_tile))
    tile_exp = tmeta[0, :n_tiles]
    n_used = tmeta[1, :1]
    xs = _dispatch(h2, pos_tiles, emeta[0, :n_exp], emeta[1, :n_exp], n_used, tm, n_tiles, d // PAIR)
    act = _gate_up(xs, w_gu, b_gu, n_used, emeta[0, :n_exp] // tm, emeta[2, :n_exp] // tm, tm)
    y = _down(act, w_dn, b_dn, tile_exp, n_used, tm)
    return _combine(y, pos_tiles, x1, mod3, gates.T, seq)


def kernel(x, c, w_ada, b_ada, norm1_g, w_in, nat_q_g, nat_k_g, nat_rpb, diff_q_g, diff_k_g, diff_lambda,
           diff_sub_g, rel_bias_table, w_out, norm2_g, router_w, router_b, w_gate_up, b_gate_up, w_down, b_down):
    bsz, seq, d = x.shape
    t = bsz * seq
    ns = (d // 2) // HEAD_DIM
    n_exp = router_w.shape[-1]
    scale = HEAD_DIM ** -0.5
    t5_bias = _t5_bias_by_offset(rel_bias_table, seq)
    ones = jnp.ones((HEAD_DIM,), F32)
    xf = x.reshape(t, d)
    for l in range(w_ada.shape[0]):
        lambda_init = 0.8 - 0.6 * math.exp(-0.3 * l)
        mod3 = _adaln(c, w_ada[l], b_ada[l]).reshape(bsz, 6, d)
        gains = jnp.stack([nat_q_g[l] * scale, nat_k_g[l], ones, diff_q_g[l] * (scale * LOG2_E), diff_k_g[l],
                           ones, ones, ones])
        qkv = _inproj(xf, mod3, norm1_g[l], w_in[l], gains, seq)
        nat = _nat_attention(qkv, nat_rpb[l], bsz, seq, ns)
        dif = _diff_attention(qkv, t5_bias, diff_lambda[l], diff_sub_g[l], lambda_init, bsz, seq, ns)
        rw_pad = jnp.pad(router_w[l], ((0, 0), (0, LANES - n_exp)))
        x1, h2, logits_t = _outproj(nat, dif, xf, mod3, w_out[l], norm2_g[l], rw_pad, router_b[l], seq)
        xf = _moe(h2, logits_t, x1, mod3, w_gate_up[l], b_gate_up[l], w_down[l], b_down[l], seq)
    return xf.reshape(bsz, seq, d)
```

```python
import functools
import math

import jax
import jax.numpy as jnp
from jax import lax
from jax.experimental import pallas as pl
from jax.experimental.pallas import tpu as pltpu

HEAD_DIM = 128
GRID_W = 64
NAT_KR_MAX = 8
NAT_KC = 16
N_BUCKETS = 32
MAX_DISTANCE = 128
TOP_K = 4
SWIGLU_LIMIT = 7.0
SWIGLU_ALPHA = 1.702
RMS_EPS = 1e-6
NEG_INF = -1e30
LOG2_E = 1.4426950408889634

LANES = 128
PAIR = 2 * LANES
V7X_VMEM_LIMIT_BYTES = 56 * 1024 * 1024

F32 = jnp.float32
BF16 = jnp.bfloat16
I32 = jnp.int32
U32 = jnp.uint32


def _tile(n, pref):
    t = min(n, pref)
    assert n % t == 0, (n, pref)
    return t


def _params(n_axes):
    return pltpu.CompilerParams(dimension_semantics=("arbitrary",) * n_axes,
                                vmem_limit_bytes=V7X_VMEM_LIMIT_BYTES)


def _adaln_kernel(c_ref, w_ref, b_ref, o_ref):
    c = c_ref[...]
    a = (c * jax.nn.sigmoid(c)).astype(BF16)
    o_ref[...] = jnp.dot(a, w_ref[...].astype(BF16), preferred_element_type=F32) + b_ref[...]


def _adaln(c, w, b):
    bsz, d = c.shape
    n = w.shape[1]
    tn = _tile(n, 1024)
    return pl.pallas_call(
        _adaln_kernel,
        grid=(n // tn,),
        in_specs=[pl.BlockSpec((bsz, d), lambda j: (0, 0)),
                  pl.BlockSpec((d, tn), lambda j: (0, j)),
                  pl.BlockSpec((1, tn), lambda j: (0, j))],
        out_specs=pl.BlockSpec((bsz, tn), lambda j: (0, j)),
        out_shape=jax.ShapeDtypeStruct((bsz, n), F32),
        compiler_params=_params(1),
        name="adaln",
    )(c, w, b.reshape(1, n))


def _inproj_kernel(x_ref, mod_ref, g_ref, w_ref, gain_ref, o_ref, h_scr):
    j = pl.program_id(1)

    @pl.when(j == 0)
    def _():
        x = x_ref[...]
        ms = jnp.mean(x * x, axis=-1, keepdims=True)
        y = x * lax.rsqrt(ms + RMS_EPS) * g_ref[...]
        m = mod_ref[...]
        h_scr[...] = (y * (1.0 + m[1:2]) + m[0:1]).astype(BF16)

    heads = o_ref.shape[0]
    per = min(heads, PAIR // HEAD_DIM)
    is_value = (j == 2) | (j == 5)
    g = gain_ref[pl.ds(j, 1), :]
    h = h_scr[...]
    for c0 in range(0, heads, per):
        y = jnp.dot(h, w_ref[:, c0 * HEAD_DIM:(c0 + per) * HEAD_DIM].astype(BF16), preferred_element_type=F32)
        for hh in range(per):
            yh = y[:, hh * HEAD_DIM:(hh + 1) * HEAD_DIM]
            ms = jnp.mean(yh * yh, axis=-1, keepdims=True)
            factor = jnp.where(is_value, 1.0, lax.rsqrt(ms + RMS_EPS))
            o_ref[c0 + hh] = (yh * factor * g).astype(BF16)


def _inproj(xf, mod3, g1, w, gains, seq):
    t, d = xf.shape
    seg = d // 2
    ns = seg // HEAD_DIM
    tm = _tile(seq, 1024)
    per_b = seq // tm
    return pl.pallas_call(
        _inproj_kernel,
        grid=(t // tm, 6),
        in_specs=[pl.BlockSpec((tm, d), lambda i, j: (i, 0)),
                  pl.BlockSpec((None, 6, d), lambda i, j: (i // per_b, 0, 0)),
                  pl.BlockSpec((1, d), lambda i, j: (0, 0)),
                  pl.BlockSpec((d, seg), lambda i, j: (0, j)),
                  pl.BlockSpec((8, HEAD_DIM), lambda i, j: (0, 0))],
        out_specs=pl.BlockSpec((ns, tm, HEAD_DIM), lambda i, j: (j, i, 0)),
        out_shape=jax.ShapeDtypeStruct((6 * ns, t, HEAD_DIM), BF16),
        scratch_shapes=[pltpu.VMEM((tm, d), BF16)],
        compiler_params=_params(2),
        name="inproj",
    )(xf, mod3, g1.reshape(1, d), w, gains)


def _nat_kernel(q_ref, k_ref, v_ref, w_ref, o_ref, b_ref, *, rows, kr, group):
    band = kr * GRID_W

    @pl.when(pl.program_id(1) == 0)
    def _():
        qc = lax.broadcasted_iota(I32, (GRID_W, LANES), 0)
        lane = lax.broadcasted_iota(I32, (GRID_W, LANES), 1)
        kc = lane % GRID_W
        col_start = jnp.clip(qc - NAT_KC // 2, 0, GRID_W - NAT_KC)
        visible = (kc >= col_start) & (kc < col_start + NAT_KC)
        for d in range(kr):
            for j in range(0, kr, 2):
                def tile(jj, shift):
                    ro = jj - d + (NAT_KR_MAX - 1)
                    row = jnp.broadcast_to(w_ref[ro:ro + 1, :], (GRID_W, LANES))
                    return pltpu.roll(row, shift, 1, stride=1, stride_axis=0)
                both = jnp.where(lane < GRID_W, tile(j, 0), tile(j + 1, GRID_W))
                b_ref[d, :, j * GRID_W:(j + 2) * GRID_W] = jnp.where(visible, both, NEG_INF)

    def body(g, carry):
        k0s, scores = [], []
        for i in range(group):
            r = g * group + i
            rs = jnp.clip(r - kr // 2, 0, rows - kr)
            k0 = pl.multiple_of(rs * GRID_W, GRID_W)
            q = q_ref[pl.ds(pl.multiple_of(r * GRID_W, GRID_W), GRID_W), :]
            s = lax.dot_general(q, k_ref[pl.ds(k0, band), :], (((1,), (1,)), ((), ())),
                                preferred_element_type=F32)
            k0s.append(k0)
            scores.append(s + b_ref[r - rs])
        s = jnp.concatenate(scores, axis=0)
        p = jnp.exp(s - jnp.max(s, axis=-1, keepdims=True))
        den = jnp.sum(p, axis=-1, keepdims=True)
        pb = p.astype(BF16)
        outs = [jnp.dot(pb[i * GRID_W:(i + 1) * GRID_W], v_ref[pl.ds(k0s[i], band), :],
                        preferred_element_type=F32) for i in range(group)]
        o = jnp.concatenate(outs, axis=0) / den
        o_ref[pl.ds(pl.multiple_of(g * (group * GRID_W), group * GRID_W), group * GRID_W), :] = o.astype(BF16)
        return carry

    lax.fori_loop(0, rows // group, body, 0)


def _nat_bias_by_offset(rpb):
    lane = jnp.arange(LANES)
    off = jnp.where(lane < LANES // 2, lane, lane - LANES)
    return jnp.take(rpb.astype(F32), jnp.clip(off, -(NAT_KC - 1), NAT_KC - 1) + (NAT_KC - 1), axis=-1)


def _nat_attention(qkv, rpb, bsz, seq, ns):
    t = qkv.shape[1]
    rows = seq // GRID_W
    kr = min(NAT_KR_MAX, rows)
    assert kr % 2 == 0 and 2 * GRID_W == LANES
    blk = (None, seq, HEAD_DIM)
    n_off = 2 * NAT_KR_MAX - 1
    return pl.pallas_call(
        functools.partial(_nat_kernel, rows=rows, kr=kr, group=_tile(rows, 32)),
        grid=(ns, bsz),
        in_specs=[pl.BlockSpec(blk, lambda h, b: (h, b, 0)),
                  pl.BlockSpec(blk, lambda h, b: (ns + h, b, 0)),
                  pl.BlockSpec(blk, lambda h, b: (2 * ns + h, b, 0)),
                  pl.BlockSpec((None, n_off, LANES), lambda h, b: (h, 0, 0))],
        out_specs=pl.BlockSpec(blk, lambda h, b: (h, b, 0)),
        out_shape=jax.ShapeDtypeStruct((ns, t, HEAD_DIM), BF16),
        scratch_shapes=[pltpu.VMEM((kr, GRID_W, kr * GRID_W), F32)],
        compiler_params=_params(2),
        name="nat_attn",
    )(qkv, qkv, qkv, _nat_bias_by_offset(rpb))


def _t5_bucket(rel):
    nb = N_BUCKETS // 2
    max_exact = nb // 2
    ret = jnp.where(rel > 0, nb, 0)
    n = jnp.abs(rel)
    nf = jnp.maximum(n, 1).astype(F32)
    large = max_exact + (jnp.log(nf / max_exact) / math.log(MAX_DISTANCE / max_exact)
                         * (nb - max_exact)).astype(I32)
    large = jnp.minimum(large, nb - 1)
    return ret + jnp.where(n < max_exact, n, large)


def _t5_bias_by_offset(rel_table, seq):
    rel = jnp.arange(2 * seq, dtype=I32) - seq
    hot = (_t5_bucket(rel)[None, :] == jnp.arange(N_BUCKETS, dtype=I32)[:, None]).astype(F32)
    u = jnp.dot(rel_table.astype(F32).T, hot, precision=lax.Precision.HIGHEST) * LOG2_E
    return u.reshape(u.shape[0], 1, 2 * seq)


def _diff_kernel(q_ref, k_ref, v_ref, u_ref, lam_ref, sg_ref, o_ref, strip_scr, *, seq, tq, nq, n_split,
                 lambda_init):
    qi = pl.program_id(2)

    @pl.when((pl.program_id(1) == 0) & (qi == 0))
    def _():
        strip_scr[...] = pltpu.roll(jnp.broadcast_to(u_ref[...], strip_scr.shape), 0, 1, stride=1, stride_axis=0)

    bias = strip_scr[:, pl.ds(pl.multiple_of((nq - qi) * tq, tq), seq)]
    lam = lam_ref[...]
    lam_full = (jnp.exp(jnp.sum(lam[0:1] * lam[1:2], axis=-1, keepdims=True))
                - jnp.exp(jnp.sum(lam[2:3] * lam[3:4], axis=-1, keepdims=True)) + lambda_init)

    v = jnp.concatenate([v_ref[0], v_ref[1]], axis=-1)
    half = tq // n_split

    def scores(p, rows):
        s = lax.dot_general(q_ref[p, rows, :], k_ref[p], (((1,), (1,)), ((), ())), preferred_element_type=F32)
        return s + bias[rows]

    def attend(s1, s2):
        e1 = jnp.exp2(s1 - jnp.max(s1, axis=-1, keepdims=True))
        e2 = jnp.exp2(s2 - jnp.max(s2, axis=-1, keepdims=True))
        r1 = 1.0 / jnp.sum(e1, axis=-1, keepdims=True)
        r2 = lam_full / jnp.sum(e2, axis=-1, keepdims=True)
        o = jnp.dot((e1 * r1 - e2 * r2).astype(BF16), v, preferred_element_type=F32)
        ms = jnp.mean(o * o, axis=-1, keepdims=True)
        return o * lax.rsqrt(ms + RMS_EPS) * sg_ref[...] * (1.0 - lambda_init)

    blocks = [slice(j * half, (j + 1) * half) for j in range(n_split)]
    all_scores = [(scores(0, rows), scores(1, rows)) for rows in blocks]
    for rows, (s1, s2) in zip(blocks, all_scores):
        o = attend(s1, s2)
        o_ref[0, rows, :] = o[:, :HEAD_DIM].astype(BF16)
        o_ref[1, rows, :] = o[:, HEAD_DIM:].astype(BF16)


def _diff_attention(qkv, bias_by_offset, lam, sub_g, lambda_init, bsz, seq, ns):
    t = qkv.shape[1]
    hd = ns // 2
    tq = _tile(seq, 512)
    nq = seq // tq
    qb, kb, vb = 3 * ns // 2, 4 * ns // 2, 5 * ns // 2
    return pl.pallas_call(
        functools.partial(_diff_kernel, seq=seq, tq=tq, nq=nq, n_split=tq // 128, lambda_init=lambda_init),
        grid=(hd, bsz, nq),
        in_specs=[pl.BlockSpec((2, tq, HEAD_DIM), lambda h, b, i: (qb + h, b * nq + i, 0)),
                  pl.BlockSpec((2, seq, HEAD_DIM), lambda h, b, i: (kb + h, b, 0)),
                  pl.BlockSpec((2, seq, HEAD_DIM), lambda h, b, i: (vb + h, b, 0)),
                  pl.BlockSpec((None, 1, 2 * seq), lambda h, b, i: (h, 0, 0)),
                  pl.BlockSpec((4, HEAD_DIM), lambda h, b, i: (0, 0)),
                  pl.BlockSpec((1, 2 * HEAD_DIM), lambda h, b, i: (0, 0))],
        out_specs=pl.BlockSpec((2, tq, HEAD_DIM), lambda h, b, i: (h, b * nq + i, 0)),
        out_shape=jax.ShapeDtypeStruct((ns, t, HEAD_DIM), BF16),
        scratch_shapes=[pltpu.VMEM((tq, 2 * seq), F32)],
        compiler_params=_params(3),
        name="diff_attn",
    )(qkv, qkv, qkv, bias_by_offset, lam, sub_g.reshape(1, 2 * HEAD_DIM))


def _rows_shape(rows, d):
    return (rows * (d // PAIR), LANES)


def _pack_pair(lo, hi):
    lo_w = lax.bitcast_convert_type(lo.astype(BF16).astype(F32), U32) >> 16
    hi_w = lax.bitcast_convert_type(hi.astype(BF16).astype(F32), U32) & jnp.uint32(0xFFFF0000)
    return lo_w | hi_w


def _unpack_pair(w):
    return (lax.bitcast_convert_type(w << 16, F32),
            lax.bitcast_convert_type(w & jnp.uint32(0xFFFF0000), F32))


def _split_bf16(v):
    hi = v.astype(BF16)
    lo = (v - hi.astype(F32)).astype(BF16)
    return hi, lo


def _outproj_kernel(nat_ref, dif_ref, x_ref, mod_ref, w_ref, g_ref, rw_ref, rb_ref,
                    x1_ref, h2_ref, lt_ref):
    ns = nat_ref.shape[0]
    a = jnp.concatenate([nat_ref[hh] for hh in range(ns)] + [dif_ref[hh] for hh in range(ns)], axis=-1)
    mix = jnp.dot(a, w_ref[...].astype(BF16), preferred_element_type=F32)
    m = mod_ref[...]
    x1 = x_ref[...] + m[2:3] * mix
    x1_ref[...] = x1
    ms = jnp.mean(x1 * x1, axis=-1, keepdims=True)
    h2 = x1 * lax.rsqrt(ms + RMS_EPS) * g_ref[...] * (1.0 + m[4:5]) + m[3:4]
    npair = h2.shape[1] // PAIR
    for p in range(npair):
        h2_ref[pl.ds(p, h2.shape[0], stride=npair), :] = _pack_pair(
            h2[:, 2 * p * LANES:(2 * p + 1) * LANES], h2[:, (2 * p + 1) * LANES:(2 * p + 2) * LANES])
    h_hi, h_lo = _split_bf16(h2)
    w_hi, w_lo = _split_bf16(rw_ref[...])
    both = jnp.dot(h_hi, jnp.concatenate([w_hi, w_lo], axis=-1), preferred_element_type=F32)
    lg = both[:, :LANES] + both[:, LANES:] + jnp.dot(h_lo, w_hi, preferred_element_type=F32)
    n_exp = lt_ref.shape[0]
    lt_ref[...] = lg.T[:n_exp] + rb_ref[...]


def _outproj(nat, dif, xf, mod3, w, g2, rw_pad, rb, seq):
    t, d = xf.shape
    ns = nat.shape[0]
    n_exp = rb.shape[0]
    tm = _tile(seq, 512)
    per_b = seq // tm
    return pl.pallas_call(
        _outproj_kernel,
        grid=(t // tm,),
        in_specs=[pl.BlockSpec((ns, tm, HEAD_DIM), lambda i: (0, i, 0)),
                  pl.BlockSpec((ns, tm, HEAD_DIM), lambda i: (0, i, 0)),
                  pl.BlockSpec((tm, d), lambda i: (i, 0)),
                  pl.BlockSpec((None, 6, d), lambda i: (i // per_b, 0, 0)),
                  pl.BlockSpec((d, d), lambda i: (0, 0), pipeline_mode=pl.Buffered(1)),
                  pl.BlockSpec((1, d), lambda i: (0, 0)),
                  pl.BlockSpec((d, LANES), lambda i: (0, 0)),
                  pl.BlockSpec((n_exp, 1), lambda i: (0, 0))],
        out_specs=[pl.BlockSpec((tm, d), lambda i: (i, 0)),
                   pl.BlockSpec((tm * (d // PAIR), LANES), lambda i: (i, 0)),
                   pl.BlockSpec((n_exp, tm), lambda i: (0, i))],
        out_shape=[jax.ShapeDtypeStruct((t, d), F32),
                   jax.ShapeDtypeStruct(_rows_shape(t, d), U32),
                   jax.ShapeDtypeStruct((n_exp, t), F32)],
        compiler_params=_params(1),
        name="outproj",
    )(nat, dif, xf, mod3, w, g2.reshape(1, d), rw_pad, rb.reshape(n_exp, 1))


def _route_kernel(lt_ref, gate_ref, pos_ref, emeta_ref, idx_ref, rank_scr, *, tm, tb):
    n_exp, t = lt_ref.shape
    eidx = lax.broadcasted_iota(I32, (n_exp, tb), 0)
    tri = (lax.broadcasted_iota(I32, (tb, tb), 0) < lax.broadcasted_iota(I32, (tb, tb), 1)).astype(BF16)

    def pass1(jb, counts):
        off = pl.multiple_of(jb * tb, tb)
        l = lt_ref[:, pl.ds(off, tb)]
        vals, sels, hots = [], [], []
        for _ in range(TOP_K):
            m = jnp.max(l, axis=0, keepdims=True)
            sel = jnp.min(jnp.where(l == m, eidx, n_exp), axis=0, keepdims=True)
            hot = eidx == sel
            vals.append(m)
            sels.append(sel)
            hots.append(hot)
            l = jnp.where(hot, -jnp.inf, l)
        exps = [jnp.exp(v - vals[0]) for v in vals]
        den = exps[0]
        for e in exps[1:]:
            den = den + e
        member = hots[0].astype(F32)
        for hot in hots[1:]:
            member = member + hot.astype(F32)
        before = jnp.dot(member.astype(BF16), tri, preferred_element_type=F32) + counts
        for k in range(TOP_K):
            rank = jnp.sum(jnp.where(hots[k], before, 0.0), axis=0, keepdims=True)
            idx_ref[pl.ds(k, 1), pl.ds(off, tb)] = sels[k]
            gate_ref[pl.ds(k, 1), pl.ds(off, tb)] = exps[k] / den
            rank_scr[pl.ds(k, 1), pl.ds(off, tb)] = rank.astype(I32)
        return counts + jnp.sum(member, axis=1, keepdims=True)

    counts = lax.fori_loop(0, t // tb, pass1, jnp.zeros((n_exp, 1), F32))

    padded = jnp.ceil(counts * (1.0 / tm)) * tm
    er = lax.broadcasted_iota(I32, (n_exp, LANES), 0)
    ec = lax.broadcasted_iota(I32, (n_exp, LANES), 1)

    def to_lanes(col):
        return jnp.sum(jnp.where(er == ec, col, 0.0), axis=0, keepdims=True)

    start = jnp.sum(jnp.where(ec < er, to_lanes(padded), 0.0), axis=1, keepdims=True)

    def pass2(jb, carry):
        off = pl.multiple_of(jb * tb, tb)
        for k in range(TOP_K):
            sel = idx_ref[pl.ds(k, 1), pl.ds(off, tb)]
            st = jnp.sum(jnp.where(eidx == sel, start, 0.0), axis=0, keepdims=True)
            pos_ref[pl.ds(k, 1), pl.ds(off, tb)] = st.astype(I32) + rank_scr[pl.ds(k, 1), pl.ds(off, tb)]
        return carry

    lax.fori_loop(0, t // tb, pass2, 0)

    n_used = jnp.sum(padded, axis=0, keepdims=True) * (1.0 / tm)
    emeta_ref[...] = jnp.zeros(emeta_ref.shape, I32)
    emeta_ref[0:1, :] = to_lanes(start).astype(I32)
    emeta_ref[1:2, :] = to_lanes(counts).astype(I32)
    emeta_ref[2:3, :] = to_lanes(padded).astype(I32)
    emeta_ref[3:4, :] = jnp.broadcast_to(n_used, (1, LANES)).astype(I32)


def _route(logits_t, tm):
    n_exp, t = logits_t.shape
    tb = _tile(t, 512)
    full = lambda shape: pl.BlockSpec(shape, lambda: (0,) * len(shape))
    return pl.pallas_call(
        functools.partial(_route_kernel, tm=tm, tb=tb),
        in_specs=[full((n_exp, t))],
        out_specs=[full((TOP_K, t)), full((TOP_K, t)), full((8, LANES))],
        out_shape=[jax.ShapeDtypeStruct((TOP_K, t), F32),
                   jax.ShapeDtypeStruct((TOP_K, t), I32),
                   jax.ShapeDtypeStruct((8, LANES), I32)],
        scratch_shapes=[pltpu.VMEM((TOP_K, t), I32), pltpu.VMEM((TOP_K, t), I32)],
        compiler_params=pltpu.CompilerParams(vmem_limit_bytes=V7X_VMEM_LIMIT_BYTES),
        name="route",
    )(logits_t)


def _dispatch_kernel(estart_ref, ecnt_ref, nused_ref, pos_hbm, h_ref, xs_hbm, pos_smem, zero_scr,
                     sem_idx, sem_row, sem_zero, *, tm, nch):
    i = pl.program_id(0)
    tmd = h_ref.shape[0] // nch
    n_exp = estart_ref.shape[0]
    n_tiles = xs_hbm.shape[0] // (tm * nch)
    slot = i % 2

    def idx_copy(step, s):
        return pltpu.make_async_copy(pos_hbm.at[step], pos_smem.at[s], sem_idx.at[s])

    @pl.when(i == 0)
    def _():
        idx_copy(0, 0).start()
        zero_scr[...] = jnp.zeros(zero_scr.shape, zero_scr.dtype)

        def fill(e, carry):
            cnt = ecnt_ref[e]
            pad = (tm - cnt % tm) % tm
            base = estart_ref[e] + cnt
            size = tm // 2
            while size >= 1:
                off = pad & ~(2 * size - 1)

                @pl.when((pad & size) != 0)
                def _(size=size, off=off):
                    dst0 = pl.multiple_of((base + off) * nch, nch)
                    cp = pltpu.make_async_copy(zero_scr.at[pl.ds(0, size * nch)],
                                               xs_hbm.at[pl.ds(dst0, size * nch)], sem_zero)
                    cp.start()
                    cp.wait()
                size //= 2
            return carry

        lax.fori_loop(0, n_exp, fill, 0)

        def tail_copy(j, part):
            dst0 = pl.multiple_of((j * tm + part * (tm // 2)) * nch, nch)
            return pltpu.make_async_copy(zero_scr, xs_hbm.at[pl.ds(dst0, tm // 2 * nch)], sem_zero)

        def tail_start(j, carry):
            tail_copy(j, 0).start()
            tail_copy(j, 1).start()
            return carry

        def tail_wait(j, carry):
            tail_copy(j, 0).wait()
            tail_copy(j, 1).wait()
            return carry

        lax.fori_loop(nused_ref[0], n_tiles, tail_start, 0)
        lax.fori_loop(nused_ref[0], n_tiles, tail_wait, 0)

    @pl.when(i + 1 < pl.num_programs(0))
    def _():
        idx_copy(i + 1, 1 - slot).start()

    idx_copy(i, slot).wait()

    def start_rows(s):
        def issue(tok, carry):
            src0 = pl.multiple_of(tok * nch, nch)
            for k in range(TOP_K):
                dst0 = pl.multiple_of(pos_smem[s, k * tmd + tok] * nch, nch)
                pltpu.make_async_copy(h_ref.at[pl.ds(src0, nch)], xs_hbm.at[pl.ds(dst0, nch)],
                                      sem_row).start(priority=k % 2)
            return carry

        lax.fori_loop(0, tmd, issue, 0, unroll=4)

    for s in range(2):
        pl.when(slot == s)(functools.partial(start_rows, s))
    for _ in range(TOP_K):
        pltpu.make_async_copy(h_ref, xs_hbm.at[pl.ds(0, tmd * nch)], sem_row).wait()


def _dispatch(h2, pos_tiles, estart, ecnt, n_used, tm, n_tiles, nch):
    ntt, per_tile = pos_tiles.shape
    tmd = per_tile // TOP_K
    grid_spec = pltpu.PrefetchScalarGridSpec(
        num_scalar_prefetch=3,
        grid=(ntt,),
        in_specs=[pl.BlockSpec(memory_space=pl.ANY),
                  pl.BlockSpec((tmd * nch, LANES), lambda i, es, ec, nu: (i, 0))],
        out_specs=pl.BlockSpec(memory_space=pl.ANY),
        scratch_shapes=[pltpu.SMEM((2, per_tile), I32),
                        pltpu.VMEM((tm // 2 * nch, LANES), h2.dtype),
                        pltpu.SemaphoreType.DMA((2,)), pltpu.SemaphoreType.DMA, pltpu.SemaphoreType.DMA],
    )
    return pl.pallas_call(
        functools.partial(_dispatch_kernel, tm=tm, nch=nch),
        grid_spec=grid_spec,
        out_shape=jax.ShapeDtypeStruct((n_tiles * tm * nch, LANES), h2.dtype),
        compiler_params=_params(1),
        name="dispatch",
    )(estart, ecnt, n_used, pos_tiles, h2)


GATE_UP_SPLIT = 2


def _gate_up_kernel(stile_ref, shalf_ref, sexp_ref, sfull_ref, nsteps_ref, x_ref, wg_ref, wu_ref, bg_ref, bu_ref,
                    act_ref, *, chunk, npair):
    s = pl.program_id(0)
    tm, ffh = act_ref.shape

    def rows_pass(nrows):
        chunks = []
        for p in range(npair):
            chunks += _unpack_pair(x_ref[pl.ds(p, nrows, stride=npair), :])
        x = jnp.concatenate(chunks, axis=-1).astype(BF16)
        for c0 in range(0, ffh, chunk):
            cols = slice(c0, c0 + chunk)
            gate = jnp.dot(x, wg_ref[:, cols].astype(BF16), preferred_element_type=F32) + bg_ref[:, cols]
            up = jnp.dot(x, wu_ref[:, cols].astype(BF16), preferred_element_type=F32) + bu_ref[:, cols]
            gate = jnp.minimum(gate, SWIGLU_LIMIT)
            up = jnp.clip(up, -SWIGLU_LIMIT, SWIGLU_LIMIT)
            act = (up + 1.0) * gate * jax.nn.sigmoid(gate * SWIGLU_ALPHA)
            act_ref[0:nrows, cols] = act.astype(BF16)

    live = s < nsteps_ref[0]
    full = sfull_ref[s] != 0

    @pl.when(live & full)
    def _():
        rows_pass(tm)

    @pl.when(live & jnp.logical_not(full))
    def _():
        rows_pass(tm // 2)
        act_ref[tm // 2:, :] = jnp.zeros((tm - tm // 2, ffh), BF16)

    @pl.when(jnp.logical_not(live))
    def _():
        act_ref[...] = jnp.zeros(act_ref.shape, BF16)


def _expert_of(index, ends):
    n_exp = ends.shape[0]
    e = jnp.minimum(jnp.sum((index[:, None] >= ends[None, :]).astype(I32), axis=1), n_exp - 1)
    hot = e[:, None] == jnp.arange(n_exp, dtype=I32)[None, :]
    return e, lambda table: jnp.sum(jnp.where(hot, table[None, :], 0), axis=1)


def _tile_is_full(tile, first, rows_in_group, tm):
    return (rows_in_group - (tile - first) * tm > tm // 2).astype(I32)


def _gate_up_schedule(n_used, first_tile, group_tiles, group_rows, n_tiles, tm):
    assert GATE_UP_SPLIT == 2
    n_steps = 2 * n_used
    s = jnp.minimum(jnp.arange(2 * n_tiles, dtype=I32), jnp.maximum(n_steps - 1, 0))
    e, lookup = _expert_of(s, 2 * (first_tile + group_tiles))
    first, count = lookup(first_tile), lookup(group_tiles)
    local = s - 2 * first
    half = (local >= count).astype(I32)
    tile = first + (count - 1) - (local - half * count)
    return tile, half, e, _tile_is_full(tile, first, lookup(group_rows), tm), n_steps


def _gate_up(xs, w, b, n_used, first_tile, group_tiles, group_rows, tm):
    n_exp, d, ff2 = w.shape
    nch = d // PAIR
    rows = xs.shape[0] // nch
    ff = ff2 // 2
    n_tiles = rows // tm
    ns = GATE_UP_SPLIT
    ffh = ff // ns
    stile, shalf, sexp, sfull, n_steps = _gate_up_schedule(n_used, first_tile, group_tiles, group_rows, n_tiles, tm)

    def out_index(s, st, sh, se, sf, n):
        live = s < n[0]
        return jnp.where(live, st[s], s // ns), jnp.where(live, sh[s], s % ns)

    b3 = b.reshape(n_exp, 1, ff2)
    grid_spec = pltpu.PrefetchScalarGridSpec(
        num_scalar_prefetch=5,
        grid=(ns * n_tiles,),
        in_specs=[pl.BlockSpec((tm * nch, LANES), lambda s, st, sh, se, sf, n: (st[s], 0)),
                  pl.BlockSpec((None, d, ffh), lambda s, st, sh, se, sf, n: (se[s], 0, sh[s])),
                  pl.BlockSpec((None, d, ffh), lambda s, st, sh, se, sf, n: (se[s], 0, ns + sh[s])),
                  pl.BlockSpec((None, 1, ffh), lambda s, st, sh, se, sf, n: (se[s], 0, sh[s])),
                  pl.BlockSpec((None, 1, ffh), lambda s, st, sh, se, sf, n: (se[s], 0, ns + sh[s]))],
        out_specs=pl.BlockSpec((tm, ffh), out_index),
    )
    return pl.pallas_call(
        functools.partial(_gate_up_kernel, chunk=_tile(ffh, 512), npair=nch),
        grid_spec=grid_spec,
        out_shape=jax.ShapeDtypeStruct((rows, ff), BF16),
        compiler_params=_params(1),
        name="gate_up",
    )(stile, shalf, sexp, sfull, n_steps, xs, w, w, b3, b3)


def _down_kernel(torder_ref, texp_ref, tfull_ref, nused_ref, a_ref, w_ref, b_ref, y_ref):
    i = pl.program_id(0)
    tm = a_ref.shape[0]
    npair = w_ref.shape[1] // PAIR
    chunk = _tile(w_ref.shape[1], 512)

    def rows_pass(nrows):
        a = a_ref[0:nrows, :]
        for c0 in range(0, w_ref.shape[1], chunk):
            y = (jnp.dot(a, w_ref[:, c0:c0 + chunk].astype(BF16), preferred_element_type=F32)
                 + b_ref[:, c0:c0 + chunk])
            for q in range(chunk // PAIR):
                y_ref[pl.ds(c0 // PAIR + q, nrows, stride=npair), :] = _pack_pair(
                    y[:, 2 * q * LANES:(2 * q + 1) * LANES], y[:, (2 * q + 1) * LANES:(2 * q + 2) * LANES])

    live = i < nused_ref[0]
    full = tfull_ref[i] != 0

    @pl.when(live & full)
    def _():
        rows_pass(tm)

    @pl.when(live & jnp.logical_not(full))
    def _():
        rows_pass(tm // 2)
        y_ref[tm // 2 * npair:, :] = jnp.zeros(((tm - tm // 2) * npair, LANES), U32)

    @pl.when(jnp.logical_not(live))
    def _():
        y_ref[...] = jnp.zeros(y_ref.shape, U32)


def _down(act, w, b, tile_order, tile_exp, tile_full, n_used, tm):
    rows, ff = act.shape
    n_exp, _, d = w.shape
    nch = d // PAIR
    n_tiles = rows // tm
    grid_spec = pltpu.PrefetchScalarGridSpec(
        num_scalar_prefetch=4,
        grid=(n_tiles,),
        in_specs=[pl.BlockSpec((tm, ff), lambda i, to, te, tf, nu: (to[jnp.minimum(i, nu[0] - 1)], 0)),
                  pl.BlockSpec((None, ff, d), lambda i, to, te, tf, nu: (te[i], 0, 0)),
                  pl.BlockSpec((None, 1, d), lambda i, to, te, tf, nu: (te[i], 0, 0))],
        out_specs=pl.BlockSpec((tm * nch, LANES), lambda i, to, te, tf, nu: (to[i], 0)),
    )
    return pl.pallas_call(
        _down_kernel,
        grid_spec=grid_spec,
        out_shape=jax.ShapeDtypeStruct(_rows_shape(rows, d), U32),
        compiler_params=_params(1),
        name="down",
    )(tile_order, tile_exp, tile_full, n_used, act, w, b.reshape(n_exp, 1, d))


def _combine_kernel(pos_hbm, y_hbm, x1_ref, mod_ref, g_ref, o_ref, pos_smem, ybuf, sem_idx, sem_row):
    i = pl.program_id(0)
    n = pl.num_programs(0)
    tmc, d = x1_ref.shape
    nch = d // PAIR
    slot = i % 2

    def idx_copy(step, s):
        return pltpu.make_async_copy(pos_hbm.at[step], pos_smem.at[s], sem_idx.at[s])

    def start_gathers_static(s):
        def issue(tok, carry):
            dst0 = pl.multiple_of(tok * nch, nch)
            for k in range(TOP_K):
                src0 = pl.multiple_of(pos_smem[s, k * tmc + tok] * nch, nch)
                pltpu.make_async_copy(y_hbm.at[pl.ds(src0, nch)], ybuf.at[s, k, pl.ds(dst0, nch)],
                                      sem_row.at[s]).start(priority=k % 2)
            return carry

        lax.fori_loop(0, tmc, issue, 0, unroll=4)

    def start_gathers(s):
        for static_s in range(2):
            pl.when(s == static_s)(functools.partial(start_gathers_static, static_s))

    @pl.when(i == 0)
    def _():
        idx_copy(0, 0).start()
        idx_copy(0, 0).wait()
        start_gathers_static(0)

        @pl.when(n > 1)
        def _():
            idx_copy(1, 1).start()

    @pl.when(i + 1 < n)
    def _():
        idx_copy(i + 1, 1 - slot).wait()
        start_gathers(1 - slot)

    @pl.when(i + 2 < n)
    def _():
        idx_copy(i + 2, slot).start()

    for k in range(TOP_K):
        pltpu.make_async_copy(y_hbm.at[pl.ds(0, tmc * nch)], ybuf.at[slot, k], sem_row.at[slot]).wait()

    g = g_ref[...]
    gate2 = mod_ref[...][5:6]
    for p in range(nch):
        lo_sum = hi_sum = None
        for k in range(TOP_K):
            lo, hi = _unpack_pair(ybuf[slot, k, pl.ds(p, tmc, stride=nch), :])
            gk = g[:, k:k + 1]
            lo_sum = lo * gk if lo_sum is None else lo_sum + lo * gk
            hi_sum = hi * gk if hi_sum is None else hi_sum + hi * gk
        for half, moe in enumerate((lo_sum, hi_sum)):
            cols = slice((2 * p + half) * LANES, (2 * p + half + 1) * LANES)
            o_ref[:, cols] = x1_ref[:, cols] + gate2[:, cols] * moe


def _combine(y, pos_tiles, x1, mod3, gates_t, seq):
    t, d = x1.shape
    ntt, per_tile = pos_tiles.shape
    tmc = per_tile // TOP_K
    per_b = seq // tmc
    return pl.pallas_call(
        _combine_kernel,
        grid=(ntt,),
        in_specs=[pl.BlockSpec(memory_space=pl.ANY),
                  pl.BlockSpec(memory_space=pl.ANY),
                  pl.BlockSpec((tmc, d), lambda i: (i, 0)),
                  pl.BlockSpec((None, 6, d), lambda i: (i // per_b, 0, 0)),
                  pl.BlockSpec((tmc, TOP_K), lambda i: (i, 0))],
        out_specs=pl.BlockSpec((tmc, d), lambda i: (i, 0)),
        out_shape=jax.ShapeDtypeStruct((t, d), F32),
        scratch_shapes=[pltpu.SMEM((2, per_tile), I32),
                        pltpu.VMEM((2, TOP_K, tmc * (d // PAIR), LANES), U32),
                        pltpu.SemaphoreType.DMA((2,)), pltpu.SemaphoreType.DMA((2,))],
        compiler_params=_params(1),
        name="combine",
    )(pos_tiles, y, x1, mod3, gates_t)


def _moe(h2, logits_t, x1, mod3, w_gu, b_gu, w_dn, b_dn, seq):
    t, d = x1.shape
    n_exp = logits_t.shape[0]
    tm = _tile(t * TOP_K, 512)
    n_tiles = (t * TOP_K) // tm + n_exp
    gates, pos, emeta = _route(logits_t, tm)
    tok_tile = _tile(seq, 256)
    pos_tiles = (pos.reshape(TOP_K, t // tok_tile, tok_tile).transpose(1, 0, 2)
                 .reshape(t // tok_tile, TOP_K * tok_tile))
    n_used = emeta[3, :1]
    group_start, group_rows = emeta[0, :n_exp], emeta[1, :n_exp]
    first_tile, group_tiles = group_start // tm, emeta[2, :n_exp] // tm
    steps = jnp.arange(n_tiles, dtype=I32)
    tile_exp, lookup = _expert_of(steps, first_tile + group_tiles)
    first = lookup(first_tile)
    tile_order = jnp.where(steps < n_used, first + (lookup(group_tiles) - 1) - (steps - first), steps)
    tile_full = _tile_is_full(tile_order, first, lookup(group_rows), tm)
    xs = _dispatch(h2, pos_tiles, group_start, group_rows, n_used, tm, n_tiles, d // PAIR)
    act = _gate_up(xs, w_gu, b_gu, n_used, first_tile, group_tiles, group_rows, tm)
    y = _down(act, w_dn, b_dn, tile_order, tile_exp, tile_full, n_used, tm)
    return _combine(y, pos_tiles, x1, mod3, gates.T, seq)


def kernel(x, c, w_ada, b_ada, norm1_g, w_in, nat_q_g, nat_k_g, nat_rpb, diff_q_g, diff_k_g, diff_lambda,
           diff_sub_g, rel_bias_table, w_out, norm2_g, router_w, router_b, w_gate_up, b_gate_up, w_down, b_down):
    bsz, seq, d = x.shape
    t = bsz * seq
    ns = (d // 2) // HEAD_DIM
    n_exp = router_w.shape[-1]
    scale = HEAD_DIM ** -0.5
    t5_bias = _t5_bias_by_offset(rel_bias_table, seq)
    ones = jnp.ones((HEAD_DIM,), F32)
    xf = x.reshape(t, d)
    for l in range(w_ada.shape[0]):
        lambda_init = 0.8 - 0.6 * math.exp(-0.3 * l)
        mod3 = _adaln(c, w_ada[l], b_ada[l]).reshape(bsz, 6, d)
        gains = jnp.stack([nat_q_g[l] * scale, nat_k_g[l], ones, diff_q_g[l] * (scale * LOG2_E), diff_k_g[l],
                           ones, ones, ones])
        qkv = _inproj(xf, mod3, norm1_g[l], w_in[l], gains, seq)
        nat = _nat_attention(qkv, nat_rpb[l], bsz, seq, ns)
        dif = _diff_attention(qkv, t5_bias, diff_lambda[l], diff_sub_g[l], lambda_init, bsz, seq, ns)
        rw_pad = jnp.pad(router_w[l], ((0, 0), (0, LANES - n_exp)))
        x1, h2, logits_t = _outproj(nat, dif, xf, mod3, w_out[l], norm2_g[l], rw_pad, router_b[l], seq)
        xf = _moe(h2, logits_t, x1, mod3, w_gate_up[l], b_gate_up[l], w_down[l], b_down[l], seq)
    return xf.reshape(bsz, seq, d)
```

```python
import functools
import math

import jax
import jax.numpy as jnp
from jax import lax
from jax.experimental import pallas as pl
from jax.experimental.pallas import tpu as pltpu

HEAD_DIM = 128
GRID_W = 64
NAT_KR_MAX = 8
NAT_KC = 16
N_BUCKETS = 32
MAX_DISTANCE = 128
TOP_K = 4
SWIGLU_LIMIT = 7.0
SWIGLU_ALPHA = 1.702
RMS_EPS = 1e-6
NEG_INF = -1e30
LOG2_E = 1.4426950408889634

LANES = 128
PAIR = 2 * LANES
V7X_VMEM_LIMIT_BYTES = 56 * 1024 * 1024

F32 = jnp.float32
BF16 = jnp.bfloat16
I32 = jnp.int32
U32 = jnp.uint32


def _tile(n, pref):
    t = min(n, pref)
    assert n % t == 0, (n, pref)
    return t


def _params(n_axes):
    return pltpu.CompilerParams(dimension_semantics=("arbitrary",) * n_axes,
                                vmem_limit_bytes=V7X_VMEM_LIMIT_BYTES)


def _adaln_kernel(c_ref, w_ref, b_ref, o_ref):
    c = c_ref[...]
    a = (c * jax.nn.sigmoid(c)).astype(BF16)
    o_ref[...] = jnp.dot(a, w_ref[...].astype(BF16), preferred_element_type=F32) + b_ref[...]


def _adaln(c, w, b):
    bsz, d = c.shape
    n = w.shape[1]
    tn = _tile(n, 1024)
    return pl.pallas_call(
        _adaln_kernel,
        grid=(n // tn,),
        in_specs=[pl.BlockSpec((bsz, d), lambda j: (0, 0)),
                  pl.BlockSpec((d, tn), lambda j: (0, j)),
                  pl.BlockSpec((1, tn), lambda j: (0, j))],
        out_specs=pl.BlockSpec((bsz, tn), lambda j: (0, j)),
        out_shape=jax.ShapeDtypeStruct((bsz, n), F32),
        compiler_params=_params(1),
        name="adaln",
    )(c, w, b.reshape(1, n))


def _inproj_kernel(x_ref, mod_ref, g_ref, w_ref, gain_ref, o_ref, h_scr):
    j = pl.program_id(1)

    @pl.when(j == 0)
    def _():
        x = x_ref[...]
        ms = jnp.mean(x * x, axis=-1, keepdims=True)
        y = x * lax.rsqrt(ms + RMS_EPS) * g_ref[...]
        m = mod_ref[...]
        h_scr[...] = (y * (1.0 + m[1:2]) + m[0:1]).astype(BF16)

    heads = o_ref.shape[0]
    per = min(heads, PAIR // HEAD_DIM)
    is_value = (j == 2) | (j == 5)
    g = gain_ref[pl.ds(j, 1), :]
    h = h_scr[...]
    for c0 in range(0, heads, per):
        y = jnp.dot(h, w_ref[:, c0 * HEAD_DIM:(c0 + per) * HEAD_DIM].astype(BF16), preferred_element_type=F32)
        for hh in range(per):
            yh = y[:, hh * HEAD_DIM:(hh + 1) * HEAD_DIM]
            ms = jnp.mean(yh * yh, axis=-1, keepdims=True)
            factor = jnp.where(is_value, 1.0, lax.rsqrt(ms + RMS_EPS))
            o_ref[c0 + hh] = (yh * factor * g).astype(BF16)


def _inproj(xf, mod3, g1, w, gains, seq):
    t, d = xf.shape
    seg = d // 2
    ns = seg // HEAD_DIM
    tm = _tile(seq, 1024)
    per_b = seq // tm
    return pl.pallas_call(
        _inproj_kernel,
        grid=(t // tm, 6),
        in_specs=[pl.BlockSpec((tm, d), lambda i, j: (i, 0)),
                  pl.BlockSpec((None, 6, d), lambda i, j: (i // per_b, 0, 0)),
                  pl.BlockSpec((1, d), lambda i, j: (0, 0)),
                  pl.BlockSpec((d, seg), lambda i, j: (0, j)),
                  pl.BlockSpec((8, HEAD_DIM), lambda i, j: (0, 0))],
        out_specs=pl.BlockSpec((ns, tm, HEAD_DIM), lambda i, j: (j, i, 0)),
        out_shape=jax.ShapeDtypeStruct((6 * ns, t, HEAD_DIM), BF16),
        scratch_shapes=[pltpu.VMEM((tm, d), BF16)],
        compiler_params=_params(2),
        name="inproj",
    )(xf, mod3, g1.reshape(1, d), w, gains)


def _nat_kernel(q_ref, k_ref, v_ref, w_ref, o_ref, b_ref, *, rows, kr, group):
    band = kr * GRID_W

    @pl.when(pl.program_id(1) == 0)
    def _():
        qc = lax.broadcasted_iota(I32, (GRID_W, LANES), 0)
        lane = lax.broadcasted_iota(I32, (GRID_W, LANES), 1)
        kc = lane % GRID_W
        col_start = jnp.clip(qc - NAT_KC // 2, 0, GRID_W - NAT_KC)
        visible = (kc >= col_start) & (kc < col_start + NAT_KC)
        for d in range(kr):
            for j in range(0, kr, 2):
                def tile(jj, shift):
                    ro = jj - d + (NAT_KR_MAX - 1)
                    row = jnp.broadcast_to(w_ref[ro:ro + 1, :], (GRID_W, LANES))
                    return pltpu.roll(row, shift, 1, stride=1, stride_axis=0)
                both = jnp.where(lane < GRID_W, tile(j, 0), tile(j + 1, GRID_W))
                b_ref[d, :, j * GRID_W:(j + 2) * GRID_W] = jnp.where(visible, both, NEG_INF)

    def body(g, carry):
        k0s, scores = [], []
        for i in range(group):
            r = g * group + i
            rs = jnp.clip(r - kr // 2, 0, rows - kr)
            k0 = pl.multiple_of(rs * GRID_W, GRID_W)
            q = q_ref[pl.ds(pl.multiple_of(r * GRID_W, GRID_W), GRID_W), :]
            s = lax.dot_general(q, k_ref[pl.ds(k0, band), :], (((1,), (1,)), ((), ())),
                                preferred_element_type=F32)
            k0s.append(k0)
            scores.append(s + b_ref[r - rs])
        s = jnp.concatenate(scores, axis=0)
        p = jnp.exp(s - jnp.max(s, axis=-1, keepdims=True))
        den = jnp.sum(p, axis=-1, keepdims=True)
        pb = p.astype(BF16)
        outs = [jnp.dot(pb[i * GRID_W:(i + 1) * GRID_W], v_ref[pl.ds(k0s[i], band), :],
                        preferred_element_type=F32) for i in range(group)]
        o = jnp.concatenate(outs, axis=0) / den
        o_ref[pl.ds(pl.multiple_of(g * (group * GRID_W), group * GRID_W), group * GRID_W), :] = o.astype(BF16)
        return carry

    lax.fori_loop(0, rows // group, body, 0)


def _nat_bias_by_offset(rpb):
    lane = jnp.arange(LANES)
    off = jnp.where(lane < LANES // 2, lane, lane - LANES)
    return jnp.take(rpb.astype(F32), jnp.clip(off, -(NAT_KC - 1), NAT_KC - 1) + (NAT_KC - 1), axis=-1)


def _nat_attention(qkv, rpb, bsz, seq, ns):
    t = qkv.shape[1]
    rows = seq // GRID_W
    kr = min(NAT_KR_MAX, rows)
    assert kr % 2 == 0 and 2 * GRID_W == LANES
    blk = (None, seq, HEAD_DIM)
    n_off = 2 * NAT_KR_MAX - 1
    return pl.pallas_call(
        functools.partial(_nat_kernel, rows=rows, kr=kr, group=_tile(rows, 32)),
        grid=(ns, bsz),
        in_specs=[pl.BlockSpec(blk, lambda h, b: (h, b, 0)),
                  pl.BlockSpec(blk, lambda h, b: (ns + h, b, 0)),
                  pl.BlockSpec(blk, lambda h, b: (2 * ns + h, b, 0)),
                  pl.BlockSpec((None, n_off, LANES), lambda h, b: (h, 0, 0))],
        out_specs=pl.BlockSpec(blk, lambda h, b: (h, b, 0)),
        out_shape=jax.ShapeDtypeStruct((ns, t, HEAD_DIM), BF16),
        scratch_shapes=[pltpu.VMEM((kr, GRID_W, kr * GRID_W), F32)],
        compiler_params=_params(2),
        name="nat_attn",
    )(qkv, qkv, qkv, _nat_bias_by_offset(rpb))


def _t5_bucket(rel):
    nb = N_BUCKETS // 2
    max_exact = nb // 2
    ret = jnp.where(rel > 0, nb, 0)
    n = jnp.abs(rel)
    nf = jnp.maximum(n, 1).astype(F32)
    large = max_exact + (jnp.log(nf / max_exact) / math.log(MAX_DISTANCE / max_exact)
                         * (nb - max_exact)).astype(I32)
    large = jnp.minimum(large, nb - 1)
    return ret + jnp.where(n < max_exact, n, large)


def _t5_bias_by_offset(rel_table, seq):
    rel = jnp.arange(2 * seq, dtype=I32) - seq
    hot = (_t5_bucket(rel)[None, :] == jnp.arange(N_BUCKETS, dtype=I32)[:, None]).astype(F32)
    u = jnp.dot(rel_table.astype(F32).T, hot, precision=lax.Precision.HIGHEST) * LOG2_E
    return u.reshape(u.shape[0], 1, 2 * seq)


def _diff_kernel(q_ref, k_ref, v_ref, u_ref, lam_ref, sg_ref, o_ref, strip_scr, *, seq, tq, nq, n_split,
                 lambda_init):
    qi = pl.program_id(2)

    @pl.when((pl.program_id(1) == 0) & (qi == 0))
    def _():
        strip_scr[...] = pltpu.roll(jnp.broadcast_to(u_ref[...], strip_scr.shape), 0, 1, stride=1, stride_axis=0)

    bias = strip_scr[:, pl.ds(pl.multiple_of((nq - qi) * tq, tq), seq)]
    lam = lam_ref[...]
    lam_full = (jnp.exp(jnp.sum(lam[0:1] * lam[1:2], axis=-1, keepdims=True))
                - jnp.exp(jnp.sum(lam[2:3] * lam[3:4], axis=-1, keepdims=True)) + lambda_init)

    v = jnp.concatenate([v_ref[0], v_ref[1]], axis=-1)
    half = tq // n_split

    def scores(p, rows):
        s = lax.dot_general(q_ref[p, rows, :], k_ref[p], (((1,), (1,)), ((), ())), preferred_element_type=F32)
        return s + bias[rows]

    def attend(s1, s2):
        e1 = jnp.exp2(s1 - jnp.max(s1, axis=-1, keepdims=True))
        e2 = jnp.exp2(s2 - jnp.max(s2, axis=-1, keepdims=True))
        r1 = 1.0 / jnp.sum(e1, axis=-1, keepdims=True)
        r2 = lam_full / jnp.sum(e2, axis=-1, keepdims=True)
        o = jnp.dot((e1 * r1 - e2 * r2).astype(BF16), v, preferred_element_type=F32)
        ms = jnp.mean(o * o, axis=-1, keepdims=True)
        return o * lax.rsqrt(ms + RMS_EPS) * sg_ref[...] * (1.0 - lambda_init)

    blocks = [slice(j * half, (j + 1) * half) for j in range(n_split)]
    all_scores = [(scores(0, rows), scores(1, rows)) for rows in blocks]
    for rows, (s1, s2) in zip(blocks, all_scores):
        o = attend(s1, s2)
        o_ref[0, rows, :] = o[:, :HEAD_DIM].astype(BF16)
        o_ref[1, rows, :] = o[:, HEAD_DIM:].astype(BF16)


def _diff_attention(qkv, bias_by_offset, lam, sub_g, lambda_init, bsz, seq, ns):
    t = qkv.shape[1]
    hd = ns // 2
    tq = _tile(seq, 512)
    nq = seq // tq
    qb, kb, vb = 3 * ns // 2, 4 * ns // 2, 5 * ns // 2
    return pl.pallas_call(
        functools.partial(_diff_kernel, seq=seq, tq=tq, nq=nq, n_split=tq // 128, lambda_init=lambda_init),
        grid=(hd, bsz, nq),
        in_specs=[pl.BlockSpec((2, tq, HEAD_DIM), lambda h, b, i: (qb + h, b * nq + i, 0)),
                  pl.BlockSpec((2, seq, HEAD_DIM), lambda h, b, i: (kb + h, b, 0)),
                  pl.BlockSpec((2, seq, HEAD_DIM), lambda h, b, i: (vb + h, b, 0)),
                  pl.BlockSpec((None, 1, 2 * seq), lambda h, b, i: (h, 0, 0)),
                  pl.BlockSpec((4, HEAD_DIM), lambda h, b, i: (0, 0)),
                  pl.BlockSpec((1, 2 * HEAD_DIM), lambda h, b, i: (0, 0))],
        out_specs=pl.BlockSpec((2, tq, HEAD_DIM), lambda h, b, i: (h, b * nq + i, 0)),
        out_shape=jax.ShapeDtypeStruct((ns, t, HEAD_DIM), BF16),
        scratch_shapes=[pltpu.VMEM((tq, 2 * seq), F32)],
        compiler_params=_params(3),
        name="diff_attn",
    )(qkv, qkv, qkv, bias_by_offset, lam, sub_g.reshape(1, 2 * HEAD_DIM))


def _rows_shape(rows, d):
    return (rows * (d // PAIR), LANES)


def _pack_pair(lo, hi):
    lo_w = lax.bitcast_convert_type(lo.astype(BF16).astype(F32), U32) >> 16
    hi_w = lax.bitcast_convert_type(hi.astype(BF16).astype(F32), U32) & jnp.uint32(0xFFFF0000)
    return lo_w | hi_w


def _unpack_pair(w):
    return (lax.bitcast_convert_type(w << 16, F32),
            lax.bitcast_convert_type(w & jnp.uint32(0xFFFF0000), F32))


def _split_bf16(v):
    hi = v.astype(BF16)
    lo = (v - hi.astype(F32)).astype(BF16)
    return hi, lo


def _outproj_kernel(nat_ref, dif_ref, x_ref, mod_ref, w_ref, g_ref, rw_ref, rb_ref,
                    x1_ref, h2_ref, lt_ref):
    ns = nat_ref.shape[0]
    a = jnp.concatenate([nat_ref[hh] for hh in range(ns)] + [dif_ref[hh] for hh in range(ns)], axis=-1)
    mix = jnp.dot(a, w_ref[...].astype(BF16), preferred_element_type=F32)
    m = mod_ref[...]
    x1 = x_ref[...] + m[2:3] * mix
    x1_ref[...] = x1
    ms = jnp.mean(x1 * x1, axis=-1, keepdims=True)
    h2 = x1 * lax.rsqrt(ms + RMS_EPS) * g_ref[...] * (1.0 + m[4:5]) + m[3:4]
    npair = h2.shape[1] // PAIR
    for p in range(npair):
        h2_ref[pl.ds(p, h2.shape[0], stride=npair), :] = _pack_pair(
            h2[:, 2 * p * LANES:(2 * p + 1) * LANES], h2[:, (2 * p + 1) * LANES:(2 * p + 2) * LANES])
    h_hi, h_lo = _split_bf16(h2)
    w_hi, w_lo = _split_bf16(rw_ref[...])
    both = jnp.dot(h_hi, jnp.concatenate([w_hi, w_lo], axis=-1), preferred_element_type=F32)
    lg = both[:, :LANES] + both[:, LANES:] + jnp.dot(h_lo, w_hi, preferred_element_type=F32)
    n_exp = lt_ref.shape[0]
    lt_ref[...] = lg.T[:n_exp] + rb_ref[...]


def _outproj(nat, dif, xf, mod3, w, g2, rw_pad, rb, seq):
    t, d = xf.shape
    ns = nat.shape[0]
    n_exp = rb.shape[0]
    tm = _tile(seq, 512)
    per_b = seq // tm
    return pl.pallas_call(
        _outproj_kernel,
        grid=(t // tm,),
        in_specs=[pl.BlockSpec((ns, tm, HEAD_DIM), lambda i: (0, i, 0)),
                  pl.BlockSpec((ns, tm, HEAD_DIM), lambda i: (0, i, 0)),
                  pl.BlockSpec((tm, d), lambda i: (i, 0)),
                  pl.BlockSpec((None, 6, d), lambda i: (i // per_b, 0, 0)),
                  pl.BlockSpec((d, d), lambda i: (0, 0), pipeline_mode=pl.Buffered(1)),
                  pl.BlockSpec((1, d), lambda i: (0, 0)),
                  pl.BlockSpec((d, LANES), lambda i: (0, 0)),
                  pl.BlockSpec((n_exp, 1), lambda i: (0, 0))],
        out_specs=[pl.BlockSpec((tm, d), lambda i: (i, 0)),
                   pl.BlockSpec((tm * (d // PAIR), LANES), lambda i: (i, 0)),
                   pl.BlockSpec((n_exp, tm), lambda i: (0, i))],
        out_shape=[jax.ShapeDtypeStruct((t, d), F32),
                   jax.ShapeDtypeStruct(_rows_shape(t, d), U32),
                   jax.ShapeDtypeStruct((n_exp, t), F32)],
        compiler_params=_params(1),
        name="outproj",
    )(nat, dif, xf, mod3, w, g2.reshape(1, d), rw_pad, rb.reshape(n_exp, 1))


def _route_kernel(lt_ref, gate_ref, pos_ref, emeta_ref, idx_ref, rank_scr, *, tm, tb):
    n_exp, t = lt_ref.shape
    eidx = lax.broadcasted_iota(I32, (n_exp, tb), 0)
    tri = (lax.broadcasted_iota(I32, (tb, tb), 0) < lax.broadcasted_iota(I32, (tb, tb), 1)).astype(BF16)

    def pass1(jb, counts):
        off = pl.multiple_of(jb * tb, tb)
        l = lt_ref[:, pl.ds(off, tb)]
        vals, sels, hots = [], [], []
        for _ in range(TOP_K):
            m = jnp.max(l, axis=0, keepdims=True)
            sel = jnp.min(jnp.where(l == m, eidx, n_exp), axis=0, keepdims=True)
            hot = eidx == sel
            vals.append(m)
            sels.append(sel)
            hots.append(hot)
            l = jnp.where(hot, -jnp.inf, l)
        exps = [jnp.exp(v - vals[0]) for v in vals]
        den = exps[0]
        for e in exps[1:]:
            den = den + e
        member = hots[0].astype(F32)
        for hot in hots[1:]:
            member = member + hot.astype(F32)
        before = jnp.dot(member.astype(BF16), tri, preferred_element_type=F32) + counts
        for k in range(TOP_K):
            rank = jnp.sum(jnp.where(hots[k], before, 0.0), axis=0, keepdims=True)
            idx_ref[pl.ds(k, 1), pl.ds(off, tb)] = sels[k]
            gate_ref[pl.ds(k, 1), pl.ds(off, tb)] = exps[k] / den
            rank_scr[pl.ds(k, 1), pl.ds(off, tb)] = rank.astype(I32)
        return counts + jnp.sum(member, axis=1, keepdims=True)

    counts = lax.fori_loop(0, t // tb, pass1, jnp.zeros((n_exp, 1), F32))

    padded = jnp.ceil(counts * (1.0 / tm)) * tm
    er = lax.broadcasted_iota(I32, (n_exp, LANES), 0)
    ec = lax.broadcasted_iota(I32, (n_exp, LANES), 1)

    def to_lanes(col):
        return jnp.sum(jnp.where(er == ec, col, 0.0), axis=0, keepdims=True)

    start = jnp.sum(jnp.where(ec < er, to_lanes(padded), 0.0), axis=1, keepdims=True)

    def pass2(jb, carry):
        off = pl.multiple_of(jb * tb, tb)
        for k in range(TOP_K):
            sel = idx_ref[pl.ds(k, 1), pl.ds(off, tb)]
            st = jnp.sum(jnp.where(eidx == sel, start, 0.0), axis=0, keepdims=True)
            pos_ref[pl.ds(k, 1), pl.ds(off, tb)] = st.astype(I32) + rank_scr[pl.ds(k, 1), pl.ds(off, tb)]
        return carry

    lax.fori_loop(0, t // tb, pass2, 0)

    n_used = jnp.sum(padded, axis=0, keepdims=True) * (1.0 / tm)
    emeta_ref[...] = jnp.zeros(emeta_ref.shape, I32)
    emeta_ref[0:1, :] = to_lanes(start).astype(I32)
    emeta_ref[1:2, :] = to_lanes(counts).astype(I32)
    emeta_ref[2:3, :] = to_lanes(padded).astype(I32)
    emeta_ref[3:4, :] = jnp.broadcast_to(n_used, (1, LANES)).astype(I32)


def _route(logits_t, tm):
    n_exp, t = logits_t.shape
    tb = _tile(t, 512)
    full = lambda shape: pl.BlockSpec(shape, lambda: (0,) * len(shape))
    return pl.pallas_call(
        functools.partial(_route_kernel, tm=tm, tb=tb),
        in_specs=[full((n_exp, t))],
        out_specs=[full((TOP_K, t)), full((TOP_K, t)), full((8, LANES))],
        out_shape=[jax.ShapeDtypeStruct((TOP_K, t), F32),
                   jax.ShapeDtypeStruct((TOP_K, t), I32),
                   jax.ShapeDtypeStruct((8, LANES), I32)],
        scratch_shapes=[pltpu.VMEM((TOP_K, t), I32), pltpu.VMEM((TOP_K, t), I32)],
        compiler_params=pltpu.CompilerParams(vmem_limit_bytes=V7X_VMEM_LIMIT_BYTES),
        name="route",
    )(logits_t)


def _dispatch_kernel(estart_ref, ecnt_ref, nused_ref, pos_hbm, h_ref, xs_hbm, pos_smem, zero_scr,
                     sem_idx, sem_row, sem_zero, *, tm, nch):
    i = pl.program_id(0)
    tmd = h_ref.shape[0] // nch
    n_exp = estart_ref.shape[0]
    n_tiles = xs_hbm.shape[0] // (tm * nch)
    slot = i % 2

    def idx_copy(step, s):
        return pltpu.make_async_copy(pos_hbm.at[step], pos_smem.at[s], sem_idx.at[s])

    @pl.when(i == 0)
    def _():
        idx_copy(0, 0).start()
        zero_scr[...] = jnp.zeros(zero_scr.shape, zero_scr.dtype)

        def fill(e, carry):
            cnt = ecnt_ref[e]
            pad = (tm - cnt % tm) % tm
            base = estart_ref[e] + cnt
            size = tm // 2
            while size >= 1:
                off = pad & ~(2 * size - 1)

                @pl.when((pad & size) != 0)
                def _(size=size, off=off):
                    dst0 = pl.multiple_of((base + off) * nch, nch)
                    cp = pltpu.make_async_copy(zero_scr.at[pl.ds(0, size * nch)],
                                               xs_hbm.at[pl.ds(dst0, size * nch)], sem_zero)
                    cp.start()
                    cp.wait()
                size //= 2
            return carry

        lax.fori_loop(0, n_exp, fill, 0)

        def tail_copy(j, part):
            dst0 = pl.multiple_of((j * tm + part * (tm // 2)) * nch, nch)
            return pltpu.make_async_copy(zero_scr, xs_hbm.at[pl.ds(dst0, tm // 2 * nch)], sem_zero)

        def tail_start(j, carry):
            tail_copy(j, 0).start()
            tail_copy(j, 1).start()
            return carry

        def tail_wait(j, carry):
            tail_copy(j, 0).wait()
            tail_copy(j, 1).wait()
            return carry

        lax.fori_loop(nused_ref[0], n_tiles, tail_start, 0)
        lax.fori_loop(nused_ref[0], n_tiles, tail_wait, 0)

    @pl.when(i + 1 < pl.num_programs(0))
    def _():
        idx_copy(i + 1, 1 - slot).start()

    idx_copy(i, slot).wait()

    def start_rows(s):
        def issue(tok, carry):
            src0 = pl.multiple_of(tok * nch, nch)
            for k in range(TOP_K):
                dst0 = pl.multiple_of(pos_smem[s, k * tmd + tok] * nch, nch)
                pltpu.make_async_copy(h_ref.at[pl.ds(src0, nch)], xs_hbm.at[pl.ds(dst0, nch)],
                                      sem_row).start(priority=k % 2)
            return carry

        lax.fori_loop(0, tmd, issue, 0, unroll=4)

    for s in range(2):
        pl.when(slot == s)(functools.partial(start_rows, s))
    for _ in range(TOP_K):
        pltpu.make_async_copy(h_ref, xs_hbm.at[pl.ds(0, tmd * nch)], sem_row).wait()


def _dispatch(h2, pos_tiles, estart, ecnt, n_used, tm, n_tiles, nch):
    ntt, per_tile = pos_tiles.shape
    tmd = per_tile // TOP_K
    grid_spec = pltpu.PrefetchScalarGridSpec(
        num_scalar_prefetch=3,
        grid=(ntt,),
        in_specs=[pl.BlockSpec(memory_space=pl.ANY),
                  pl.BlockSpec((tmd * nch, LANES), lambda i, es, ec, nu: (i, 0))],
        out_specs=pl.BlockSpec(memory_space=pl.ANY),
        scratch_shapes=[pltpu.SMEM((2, per_tile), I32),
                        pltpu.VMEM((tm // 2 * nch, LANES), h2.dtype),
                        pltpu.SemaphoreType.DMA((2,)), pltpu.SemaphoreType.DMA, pltpu.SemaphoreType.DMA],
    )
    return pl.pallas_call(
        functools.partial(_dispatch_kernel, tm=tm, nch=nch),
        grid_spec=grid_spec,
        out_shape=jax.ShapeDtypeStruct((n_tiles * tm * nch, LANES), h2.dtype),
        compiler_params=_params(1),
        name="dispatch",
    )(estart, ecnt, n_used, pos_tiles, h2)


GATE_UP_SPLIT = 2


(M_TILE, M_HALF, M_EXP, M_FULL, M_FIRST, M_SLOT, M_NEXT, M_HAS_NEXT) = range(8)


def _stream_group_weights(meta_ref, step, live, copies):
    @pl.when(live & (step == 0))
    def _():
        for cp in copies(meta_ref[M_EXP, 0], meta_ref[M_SLOT, 0]):
            cp.start()

    @pl.when(live & (meta_ref[M_FIRST, step] != 0))
    def _():
        slot = meta_ref[M_SLOT, step]
        for cp in copies(meta_ref[M_EXP, step], slot):
            cp.wait()

        @pl.when(meta_ref[M_HAS_NEXT, step] != 0)
        def _():
            for cp in copies(meta_ref[M_NEXT, step], 1 - slot):
                cp.start()


def _gate_up_kernel(meta_ref, nsteps_ref, x_ref, w_hbm, bg_ref, bu_ref, act_ref, wbuf, sem_w, *, chunk, npair):
    s = pl.program_id(0)
    tm, ffh = act_ref.shape
    ff = w_hbm.shape[2] // 2
    live = s < nsteps_ref[0]
    slot = meta_ref[M_SLOT, s]

    def weight_copies(e, half):
        c0 = pl.multiple_of(half * ffh, ffh)
        return (pltpu.make_async_copy(w_hbm.at[e, :, pl.ds(c0, ffh)], wbuf.at[half, 0], sem_w.at[half]),
                pltpu.make_async_copy(w_hbm.at[e, :, pl.ds(pl.multiple_of(ff + c0, ffh), ffh)], wbuf.at[half, 1],
                                      sem_w.at[half]))

    _stream_group_weights(meta_ref, s, live, weight_copies)

    def rows_pass(nrows):
        chunks = []
        for p in range(npair):
            chunks += _unpack_pair(x_ref[pl.ds(p, nrows, stride=npair), :])
        x = jnp.concatenate(chunks, axis=-1).astype(BF16)
        for c0 in range(0, ffh, chunk):
            cols = slice(c0, c0 + chunk)
            gate = jnp.dot(x, wbuf[slot, 0, :, cols].astype(BF16), preferred_element_type=F32) + bg_ref[:, cols]
            up = jnp.dot(x, wbuf[slot, 1, :, cols].astype(BF16), preferred_element_type=F32) + bu_ref[:, cols]
            gate = jnp.minimum(gate, SWIGLU_LIMIT)
            up = jnp.clip(up, -SWIGLU_LIMIT, SWIGLU_LIMIT)
            act = (up + 1.0) * gate * jax.nn.sigmoid(gate * SWIGLU_ALPHA)
            act_ref[0:nrows, cols] = act.astype(BF16)

    full = meta_ref[M_FULL, s] != 0

    @pl.when(live & full)
    def _():
        rows_pass(tm)

    @pl.when(live & jnp.logical_not(full))
    def _():
        rows_pass(tm // 2)
        act_ref[tm // 2:, :] = jnp.zeros((tm - tm // 2, ffh), BF16)

    @pl.when(jnp.logical_not(live))
    def _():
        act_ref[...] = jnp.zeros(act_ref.shape, BF16)


def _expert_of(index, ends):
    n_exp = ends.shape[0]
    e = jnp.minimum(jnp.sum((index[:, None] >= ends[None, :]).astype(I32), axis=1), n_exp - 1)
    hot = e[:, None] == jnp.arange(n_exp, dtype=I32)[None, :]
    return e, lambda table: jnp.sum(jnp.where(hot, table[None, :], 0), axis=1)


def _tile_is_full(tile, first, rows_in_group, tm):
    return (rows_in_group - (tile - first) * tm > tm // 2).astype(I32)


def _live_expert_walk(group_tiles):
    n_exp = group_tiles.shape[0]
    idx = jnp.arange(n_exp, dtype=I32)
    owns = (group_tiles > 0)[None, :]
    rank = jnp.sum((idx[None, :] < idx[:, None]) & owns, axis=1).astype(I32)
    nxt = jnp.min(jnp.where((idx[None, :] > idx[:, None]) & owns, idx[None, :], n_exp), axis=1).astype(I32)
    return rank, nxt


def _gate_up_schedule(n_used, first_tile, group_tiles, group_rows, n_tiles, tm):
    assert GATE_UP_SPLIT == 2
    n_exp = first_tile.shape[0]
    n_steps = 2 * n_used
    s = jnp.minimum(jnp.arange(2 * n_tiles, dtype=I32), jnp.maximum(n_steps - 1, 0))
    e, lookup = _expert_of(s, 2 * (first_tile + group_tiles))
    first, count = lookup(first_tile), lookup(group_tiles)
    local = s - 2 * first
    half = (local >= count).astype(I32)
    inner = local - half * count
    tile = first + (count - 1) - inner
    _, next_live = _live_expert_walk(group_tiles)
    nxt = lookup(next_live)
    rows = [None] * 8
    rows[M_TILE], rows[M_HALF], rows[M_EXP] = tile, half, e
    rows[M_FULL] = _tile_is_full(tile, first, lookup(group_rows), tm)
    rows[M_FIRST] = (inner == 0).astype(I32)
    rows[M_SLOT] = half
    rows[M_NEXT] = jnp.where(half == 0, e, jnp.minimum(nxt, n_exp - 1))
    rows[M_HAS_NEXT] = ((half == 0) | (nxt < n_exp)).astype(I32)
    return jnp.stack(rows), n_steps


def _gate_up(xs, w, b, n_used, first_tile, group_tiles, group_rows, tm):
    n_exp, d, ff2 = w.shape
    nch = d // PAIR
    rows = xs.shape[0] // nch
    ff = ff2 // 2
    n_tiles = rows // tm
    ns = GATE_UP_SPLIT
    ffh = ff // ns
    meta, n_steps = _gate_up_schedule(n_used, first_tile, group_tiles, group_rows, n_tiles, tm)

    def out_index(s, m, n):
        live = s < n[0]
        return jnp.where(live, m[M_TILE, s], s // ns), jnp.where(live, m[M_HALF, s], s % ns)

    b3 = b.reshape(n_exp, 1, ff2)
    grid_spec = pltpu.PrefetchScalarGridSpec(
        num_scalar_prefetch=2,
        grid=(ns * n_tiles,),
        in_specs=[pl.BlockSpec((tm * nch, LANES), lambda s, m, n: (m[M_TILE, s], 0)),
                  pl.BlockSpec(memory_space=pl.ANY),
                  pl.BlockSpec((None, 1, ffh), lambda s, m, n: (m[M_EXP, s], 0, m[M_HALF, s])),
                  pl.BlockSpec((None, 1, ffh), lambda s, m, n: (m[M_EXP, s], 0, ns + m[M_HALF, s]))],
        out_specs=pl.BlockSpec((tm, ffh), out_index),
        scratch_shapes=[pltpu.VMEM((2, 2, d, ffh), F32), pltpu.SemaphoreType.DMA((2,))],
    )
    return pl.pallas_call(
        functools.partial(_gate_up_kernel, chunk=_tile(ffh, 512), npair=nch),
        grid_spec=grid_spec,
        out_shape=jax.ShapeDtypeStruct((rows, ff), BF16),
        compiler_params=_params(1),
        name="gate_up",
    )(meta, n_steps, xs, w, b3, b3)


def _down_kernel(meta_ref, nused_ref, a_ref, w_hbm, b_ref, y_ref, wbuf, sem_w):
    i = pl.program_id(0)
    tm = a_ref.shape[0]
    d = w_hbm.shape[2]
    npair = d // PAIR
    chunk = _tile(d, 512)
    live = i < nused_ref[0]
    slot = meta_ref[M_SLOT, i]

    def weight_copies(e, slot_):
        return (pltpu.make_async_copy(w_hbm.at[e], wbuf.at[slot_], sem_w.at[slot_]),)

    _stream_group_weights(meta_ref, i, live, weight_copies)

    def rows_pass(nrows):
        a = a_ref[0:nrows, :]
        for c0 in range(0, d, chunk):
            y = (jnp.dot(a, wbuf[slot, :, c0:c0 + chunk].astype(BF16), preferred_element_type=F32)
                 + b_ref[:, c0:c0 + chunk])
            for q in range(chunk // PAIR):
                y_ref[pl.ds(c0 // PAIR + q, nrows, stride=npair), :] = _pack_pair(
                    y[:, 2 * q * LANES:(2 * q + 1) * LANES], y[:, (2 * q + 1) * LANES:(2 * q + 2) * LANES])

    full = meta_ref[M_FULL, i] != 0

    @pl.when(live & full)
    def _():
        rows_pass(tm)

    @pl.when(live & jnp.logical_not(full))
    def _():
        rows_pass(tm // 2)
        y_ref[tm // 2 * npair:, :] = jnp.zeros(((tm - tm // 2) * npair, LANES), U32)

    @pl.when(jnp.logical_not(live))
    def _():
        y_ref[...] = jnp.zeros(y_ref.shape, U32)


def _down_schedule(n_used, first_tile, group_tiles, group_rows, n_tiles, tm):
    n_exp = first_tile.shape[0]
    steps = jnp.arange(n_tiles, dtype=I32)
    e, lookup = _expert_of(steps, first_tile + group_tiles)
    first = lookup(first_tile)
    inner = steps - first
    tile = jnp.where(steps < n_used, first + (lookup(group_tiles) - 1) - inner, steps)
    rank, next_live = _live_expert_walk(group_tiles)
    nxt = lookup(next_live)
    rows = [None] * 8
    rows[M_TILE], rows[M_HALF], rows[M_EXP] = tile, jnp.zeros_like(steps), e
    rows[M_FULL] = _tile_is_full(tile, first, lookup(group_rows), tm)
    rows[M_FIRST] = (inner == 0).astype(I32)
    rows[M_SLOT] = lookup(rank) % 2
    rows[M_NEXT] = jnp.minimum(nxt, n_exp - 1)
    rows[M_HAS_NEXT] = (nxt < n_exp).astype(I32)
    return jnp.stack(rows)


def _down(act, w, b, n_used, first_tile, group_tiles, group_rows, tm):
    rows, ff = act.shape
    n_exp, _, d = w.shape
    nch = d // PAIR
    n_tiles = rows // tm
    meta = _down_schedule(n_used, first_tile, group_tiles, group_rows, n_tiles, tm)
    grid_spec = pltpu.PrefetchScalarGridSpec(
        num_scalar_prefetch=2,
        grid=(n_tiles,),
        in_specs=[pl.BlockSpec((tm, ff), lambda i, m, nu: (m[M_TILE, jnp.minimum(i, nu[0] - 1)], 0)),
                  pl.BlockSpec(memory_space=pl.ANY),
                  pl.BlockSpec((None, 1, d), lambda i, m, nu: (m[M_EXP, i], 0, 0))],
        out_specs=pl.BlockSpec((tm * nch, LANES), lambda i, m, nu: (m[M_TILE, i], 0)),
        scratch_shapes=[pltpu.VMEM((2, ff, d), F32), pltpu.SemaphoreType.DMA((2,))],
    )
    return pl.pallas_call(
        _down_kernel,
        grid_spec=grid_spec,
        out_shape=jax.ShapeDtypeStruct(_rows_shape(rows, d), U32),
        compiler_params=_params(1),
        name="down",
    )(meta, n_used, act, w, b.reshape(n_exp, 1, d))


def _combine_kernel(pos_hbm, y_hbm, x1_ref, mod_ref, g_ref, o_ref, pos_smem, ybuf, sem_idx, sem_row):
    i = pl.program_id(0)
    n = pl.num_programs(0)
    tmc, d = x1_ref.shape
    nch = d // PAIR
    slot = i % 2

    def idx_copy(step, s):
        return pltpu.make_async_copy(pos_hbm.at[step], pos_smem.at[s], sem_idx.at[s])

    def start_gathers_static(s):
        def issue(tok, carry):
            dst0 = pl.multiple_of(tok * nch, nch)
            for k in range(TOP_K):
                src0 = pl.multiple_of(pos_smem[s, k * tmc + tok] * nch, nch)
                pltpu.make_async_copy(y_hbm.at[pl.ds(src0, nch)], ybuf.at[s, k, pl.ds(dst0, nch)],
                                      sem_row.at[s]).start(priority=k % 2)
            return carry

        lax.fori_loop(0, tmc, issue, 0, unroll=4)

    def start_gathers(s):
        for static_s in range(2):
            pl.when(s == static_s)(functools.partial(start_gathers_static, static_s))

    @pl.when(i == 0)
    def _():
        idx_copy(0, 0).start()
        idx_copy(0, 0).wait()
        start_gathers_static(0)

        @pl.when(n > 1)
        def _():
            idx_copy(1, 1).start()

    @pl.when(i + 1 < n)
    def _():
        idx_copy(i + 1, 1 - slot).wait()
        start_gathers(1 - slot)

    @pl.when(i + 2 < n)
    def _():
        idx_copy(i + 2, slot).start()

    for k in range(TOP_K):
        pltpu.make_async_copy(y_hbm.at[pl.ds(0, tmc * nch)], ybuf.at[slot, k], sem_row.at[slot]).wait()

    g = g_ref[...]
    gate2 = mod_ref[...][5:6]
    for p in range(nch):
        lo_sum = hi_sum = None
        for k in range(TOP_K):
            lo, hi = _unpack_pair(ybuf[slot, k, pl.ds(p, tmc, stride=nch), :])
            gk = g[:, k:k + 1]
            lo_sum = lo * gk if lo_sum is None else lo_sum + lo * gk
            hi_sum = hi * gk if hi_sum is None else hi_sum + hi * gk
        for half, moe in enumerate((lo_sum, hi_sum)):
            cols = slice((2 * p + half) * LANES, (2 * p + half + 1) * LANES)
            o_ref[:, cols] = x1_ref[:, cols] + gate2[:, cols] * moe


def _combine(y, pos_tiles, x1, mod3, gates_t, seq):
    t, d = x1.shape
    ntt, per_tile = pos_tiles.shape
    tmc = per_tile // TOP_K
    per_b = seq // tmc
    return pl.pallas_call(
        _combine_kernel,
        grid=(ntt,),
        in_specs=[pl.BlockSpec(memory_space=pl.ANY),
                  pl.BlockSpec(memory_space=pl.ANY),
                  pl.BlockSpec((tmc, d), lambda i: (i, 0)),
                  pl.BlockSpec((None, 6, d), lambda i: (i // per_b, 0, 0)),
                  pl.BlockSpec((tmc, TOP_K), lambda i: (i, 0))],
        out_specs=pl.BlockSpec((tmc, d), lambda i: (i, 0)),
        out_shape=jax.ShapeDtypeStruct((t, d), F32),
        scratch_shapes=[pltpu.SMEM((2, per_tile), I32),
                        pltpu.VMEM((2, TOP_K, tmc * (d // PAIR), LANES), U32),
                        pltpu.SemaphoreType.DMA((2,)), pltpu.SemaphoreType.DMA((2,))],
        compiler_params=_params(1),
        name="combine",
    )(pos_tiles, y, x1, mod3, gates_t)


def _moe(h2, logits_t, x1, mod3, w_gu, b_gu, w_dn, b_dn, seq):
    t, d = x1.shape
    n_exp = logits_t.shape[0]
    tm = _tile(t * TOP_K, 512)
    n_tiles = (t * TOP_K) // tm + n_exp
    gates, pos, emeta = _route(logits_t, tm)
    tok_tile = _tile(seq, 256)
    pos_tiles = (pos.reshape(TOP_K, t // tok_tile, tok_tile).transpose(1, 0, 2)
                 .reshape(t // tok_tile, TOP_K * tok_tile))
    n_used = emeta[3, :1]
    group_start, group_rows = emeta[0, :n_exp], emeta[1, :n_exp]
    first_tile, group_tiles = group_start // tm, emeta[2, :n_exp] // tm
    xs = _dispatch(h2, pos_tiles, group_start, group_rows, n_used, tm, n_tiles, d // PAIR)
    act = _gate_up(xs, w_gu, b_gu, n_used, first_tile, group_tiles, group_rows, tm)
    y = _down(act, w_dn, b_dn, n_used, first_tile, group_tiles, group_rows, tm)
    return _combine(y, pos_tiles, x1, mod3, gates.T, seq)


def kernel(x, c, w_ada, b_ada, norm1_g, w_in, nat_q_g, nat_k_g, nat_rpb, diff_q_g, diff_k_g, diff_lambda,
           diff_sub_g, rel_bias_table, w_out, norm2_g, router_w, router_b, w_gate_up, b_gate_up, w_down, b_down):
    bsz, seq, d = x.shape
    t = bsz * seq
    ns = (d // 2) // HEAD_DIM
    n_exp = router_w.shape[-1]
    scale = HEAD_DIM ** -0.5
    t5_bias = _t5_bias_by_offset(rel_bias_table, seq)
    ones = jnp.ones((HEAD_DIM,), F32)
    xf = x.reshape(t, d)
    for l in range(w_ada.shape[0]):
        lambda_init = 0.8 - 0.6 * math.exp(-0.3 * l)
        mod3 = _adaln(c, w_ada[l], b_ada[l]).reshape(bsz, 6, d)
        gains = jnp.stack([nat_q_g[l] * scale, nat_k_g[l], ones, diff_q_g[l] * (scale * LOG2_E), diff_k_g[l],
                           ones, ones, ones])
        qkv = _inproj(xf, mod3, norm1_g[l], w_in[l], gains, seq)
        nat = _nat_attention(qkv, nat_rpb[l], bsz, seq, ns)
        dif = _diff_attention(qkv, t5_bias, diff_lambda[l], diff_sub_g[l], lambda_init, bsz, seq, ns)
        rw_pad = jnp.pad(router_w[l], ((0, 0), (0, LANES - n_exp)))
        x1, h2, logits_t = _outproj(nat, dif, xf, mod3, w_out[l], norm2_g[l], rw_pad, router_b[l], seq)
        xf = _moe(h2, logits_t, x1, mod3, w_gate_up[l], b_gate_up[l], w_down[l], b_down[l], seq)
    return xf.reshape(bsz, seq, d)
```

```python
import functools
import math

import jax
import jax.numpy as jnp
from jax import lax
from jax.experimental import pallas as pl
from jax.experimental.pallas import tpu as pltpu

HEAD_DIM = 128
GRID_W = 64
NAT_KR_MAX = 8
NAT_KC = 16
N_BUCKETS = 32
MAX_DISTANCE = 128
TOP_K = 4
SWIGLU_LIMIT = 7.0
SWIGLU_ALPHA = 1.702
RMS_EPS = 1e-6
NEG_INF = -1e30
LOG2_E = 1.4426950408889634

LANES = 128
PAIR = 2 * LANES
V7X_VMEM_LIMIT_BYTES = 56 * 1024 * 1024

F32 = jnp.float32
BF16 = jnp.bfloat16
I32 = jnp.int32
U32 = jnp.uint32


def _tile(n, pref):
    t = min(n, pref)
    assert n % t == 0, (n, pref)
    return t


def _params(n_axes):
    return pltpu.CompilerParams(dimension_semantics=("arbitrary",) * n_axes,
                                vmem_limit_bytes=V7X_VMEM_LIMIT_BYTES)


def _adaln_kernel(c_ref, w_ref, b_ref, o_ref):
    c = c_ref[...]
    a = (c * jax.nn.sigmoid(c)).astype(BF16)
    o_ref[...] = jnp.dot(a, w_ref[...].astype(BF16), preferred_element_type=F32) + b_ref[...]


def _adaln(c, w, b):
    bsz, d = c.shape
    n = w.shape[1]
    tn = _tile(n, 1024)
    return pl.pallas_call(
        _adaln_kernel,
        grid=(n // tn,),
        in_specs=[pl.BlockSpec((bsz, d), lambda j: (0, 0)),
                  pl.BlockSpec((d, tn), lambda j: (0, j)),
                  pl.BlockSpec((1, tn), lambda j: (0, j))],
        out_specs=pl.BlockSpec((bsz, tn), lambda j: (0, j)),
        out_shape=jax.ShapeDtypeStruct((bsz, n), F32),
        compiler_params=_params(1),
        name="adaln",
    )(c, w, b.reshape(1, n))


def _inproj_kernel(x_ref, mod_ref, g_ref, w_ref, gain_ref, o_ref, h_scr):
    j = pl.program_id(1)

    @pl.when(j == 0)
    def _():
        x = x_ref[...]
        ms = jnp.mean(x * x, axis=-1, keepdims=True)
        y = x * lax.rsqrt(ms + RMS_EPS) * g_ref[...]
        m = mod_ref[...]
        h_scr[...] = (y * (1.0 + m[1:2]) + m[0:1]).astype(BF16)

    heads = o_ref.shape[0]
    per = min(heads, PAIR // HEAD_DIM)
    is_value = (j == 2) | (j == 5)
    g = gain_ref[pl.ds(j, 1), :]
    h = h_scr[...]
    for c0 in range(0, heads, per):
        y = jnp.dot(h, w_ref[:, c0 * HEAD_DIM:(c0 + per) * HEAD_DIM].astype(BF16), preferred_element_type=F32)
        for hh in range(per):
            yh = y[:, hh * HEAD_DIM:(hh + 1) * HEAD_DIM]
            ms = jnp.mean(yh * yh, axis=-1, keepdims=True)
            factor = jnp.where(is_value, 1.0, lax.rsqrt(ms + RMS_EPS))
            o_ref[c0 + hh] = (yh * factor * g).astype(BF16)


def _inproj(xf, mod3, g1, w, gains, seq):
    t, d = xf.shape
    seg = d // 2
    ns = seg // HEAD_DIM
    tm = _tile(seq, 1024)
    per_b = seq // tm
    return pl.pallas_call(
        _inproj_kernel,
        grid=(t // tm, 6),
        in_specs=[pl.BlockSpec((tm, d), lambda i, j: (i, 0)),
                  pl.BlockSpec((None, 6, d), lambda i, j: (i // per_b, 0, 0)),
                  pl.BlockSpec((1, d), lambda i, j: (0, 0)),
                  pl.BlockSpec((d, seg), lambda i, j: (0, j)),
                  pl.BlockSpec((8, HEAD_DIM), lambda i, j: (0, 0))],
        out_specs=pl.BlockSpec((ns, tm, HEAD_DIM), lambda i, j: (j, i, 0)),
        out_shape=jax.ShapeDtypeStruct((6 * ns, t, HEAD_DIM), BF16),
        scratch_shapes=[pltpu.VMEM((tm, d), BF16)],
        compiler_params=_params(2),
        name="inproj",
    )(xf, mod3, g1.reshape(1, d), w, gains)


def _nat_kernel(q_ref, k_ref, v_ref, w_ref, o_ref, b_ref, *, rows, kr, group):
    band = kr * GRID_W

    @pl.when(pl.program_id(1) == 0)
    def _():
        qc = lax.broadcasted_iota(I32, (GRID_W, LANES), 0)
        lane = lax.broadcasted_iota(I32, (GRID_W, LANES), 1)
        kc = lane % GRID_W
        col_start = jnp.clip(qc - NAT_KC // 2, 0, GRID_W - NAT_KC)
        visible = (kc >= col_start) & (kc < col_start + NAT_KC)
        for d in range(kr):
            for j in range(0, kr, 2):
                def tile(jj, shift):
                    ro = jj - d + (NAT_KR_MAX - 1)
                    row = jnp.broadcast_to(w_ref[ro:ro + 1, :], (GRID_W, LANES))
                    return pltpu.roll(row, shift, 1, stride=1, stride_axis=0)
                both = jnp.where(lane < GRID_W, tile(j, 0), tile(j + 1, GRID_W))
                b_ref[d, :, j * GRID_W:(j + 2) * GRID_W] = jnp.where(visible, both, NEG_INF)

    def body(g, carry):
        k0s, scores = [], []
        for i in range(group):
            r = g * group + i
            rs = jnp.clip(r - kr // 2, 0, rows - kr)
            k0 = pl.multiple_of(rs * GRID_W, GRID_W)
            q = q_ref[pl.ds(pl.multiple_of(r * GRID_W, GRID_W), GRID_W), :]
            s = lax.dot_general(q, k_ref[pl.ds(k0, band), :], (((1,), (1,)), ((), ())),
                                preferred_element_type=F32)
            k0s.append(k0)
            scores.append(s + b_ref[r - rs])
        s = jnp.concatenate(scores, axis=0)
        p = jnp.exp(s - jnp.max(s, axis=-1, keepdims=True))
        den = jnp.sum(p, axis=-1, keepdims=True)
        pb = p.astype(BF16)
        outs = [jnp.dot(pb[i * GRID_W:(i + 1) * GRID_W], v_ref[pl.ds(k0s[i], band), :],
                        preferred_element_type=F32) for i in range(group)]
        o = jnp.concatenate(outs, axis=0) / den
        o_ref[pl.ds(pl.multiple_of(g * (group * GRID_W), group * GRID_W), group * GRID_W), :] = o.astype(BF16)
        return carry

    lax.fori_loop(0, rows // group, body, 0)


def _nat_bias_by_offset(rpb):
    lane = jnp.arange(LANES)
    off = jnp.where(lane < LANES // 2, lane, lane - LANES)
    return jnp.take(rpb.astype(F32), jnp.clip(off, -(NAT_KC - 1), NAT_KC - 1) + (NAT_KC - 1), axis=-1)


def _nat_attention(qkv, rpb, bsz, seq, ns):
    t = qkv.shape[1]
    rows = seq // GRID_W
    kr = min(NAT_KR_MAX, rows)
    assert kr % 2 == 0 and 2 * GRID_W == LANES
    blk = (None, seq, HEAD_DIM)
    n_off = 2 * NAT_KR_MAX - 1
    return pl.pallas_call(
        functools.partial(_nat_kernel, rows=rows, kr=kr, group=_tile(rows, 32)),
        grid=(ns, bsz),
        in_specs=[pl.BlockSpec(blk, lambda h, b: (h, b, 0)),
                  pl.BlockSpec(blk, lambda h, b: (ns + h, b, 0)),
                  pl.BlockSpec(blk, lambda h, b: (2 * ns + h, b, 0)),
                  pl.BlockSpec((None, n_off, LANES), lambda h, b: (h, 0, 0))],
        out_specs=pl.BlockSpec(blk, lambda h, b: (h, b, 0)),
        out_shape=jax.ShapeDtypeStruct((ns, t, HEAD_DIM), BF16),
        scratch_shapes=[pltpu.VMEM((kr, GRID_W, kr * GRID_W), F32)],
        compiler_params=_params(2),
        name="nat_attn",
    )(qkv, qkv, qkv, _nat_bias_by_offset(rpb))


def _t5_bucket(rel):
    nb = N_BUCKETS // 2
    max_exact = nb // 2
    ret = jnp.where(rel > 0, nb, 0)
    n = jnp.abs(rel)
    nf = jnp.maximum(n, 1).astype(F32)
    large = max_exact + (jnp.log(nf / max_exact) / math.log(MAX_DISTANCE / max_exact)
                         * (nb - max_exact)).astype(I32)
    large = jnp.minimum(large, nb - 1)
    return ret + jnp.where(n < max_exact, n, large)


def _t5_bias_by_offset(rel_table, seq):
    rel = jnp.arange(2 * seq, dtype=I32) - seq
    hot = (_t5_bucket(rel)[None, :] == jnp.arange(N_BUCKETS, dtype=I32)[:, None]).astype(F32)
    u = jnp.dot(rel_table.astype(F32).T, hot, precision=lax.Precision.HIGHEST) * LOG2_E
    return u.reshape(u.shape[0], 1, 2 * seq)


def _diff_kernel(q_ref, k_ref, v_ref, u_ref, lam_ref, sg_ref, o_ref, strip_scr, *, seq, tq, nq, n_split,
                 lambda_init):
    qi = pl.program_id(2)

    @pl.when((pl.program_id(1) == 0) & (qi == 0))
    def _():
        strip_scr[...] = pltpu.roll(jnp.broadcast_to(u_ref[...], strip_scr.shape), 0, 1, stride=1, stride_axis=0)

    bias = strip_scr[:, pl.ds(pl.multiple_of((nq - qi) * tq, tq), seq)]
    lam = lam_ref[...]
    lam_full = (jnp.exp(jnp.sum(lam[0:1] * lam[1:2], axis=-1, keepdims=True))
                - jnp.exp(jnp.sum(lam[2:3] * lam[3:4], axis=-1, keepdims=True)) + lambda_init)

    v = jnp.concatenate([v_ref[0], v_ref[1]], axis=-1)
    half = tq // n_split

    def scores(p, rows):
        s = lax.dot_general(q_ref[p, rows, :], k_ref[p], (((1,), (1,)), ((), ())), preferred_element_type=F32)
        return s + bias[rows]

    def attend(s1, s2):
        e1 = jnp.exp2(s1 - jnp.max(s1, axis=-1, keepdims=True))
        e2 = jnp.exp2(s2 - jnp.max(s2, axis=-1, keepdims=True))
        r1 = 1.0 / jnp.sum(e1, axis=-1, keepdims=True)
        r2 = lam_full / jnp.sum(e2, axis=-1, keepdims=True)
        o = jnp.dot((e1 * r1 - e2 * r2).astype(BF16), v, preferred_element_type=F32)
        ms = jnp.mean(o * o, axis=-1, keepdims=True)
        return o * lax.rsqrt(ms + RMS_EPS) * sg_ref[...] * (1.0 - lambda_init)

    blocks = [slice(j * half, (j + 1) * half) for j in range(n_split)]
    all_scores = [(scores(0, rows), scores(1, rows)) for rows in blocks]
    for rows, (s1, s2) in zip(blocks, all_scores):
        o = attend(s1, s2)
        o_ref[0, rows, :] = o[:, :HEAD_DIM].astype(BF16)
        o_ref[1, rows, :] = o[:, HEAD_DIM:].astype(BF16)


def _diff_attention(qkv, bias_by_offset, lam, sub_g, lambda_init, bsz, seq, ns):
    t = qkv.shape[1]
    hd = ns // 2
    tq = _tile(seq, 1024)
    nq = seq // tq
    qb, kb, vb = 3 * ns // 2, 4 * ns // 2, 5 * ns // 2
    return pl.pallas_call(
        functools.partial(_diff_kernel, seq=seq, tq=tq, nq=nq, n_split=tq // 128, lambda_init=lambda_init),
        grid=(hd, bsz, nq),
        in_specs=[pl.BlockSpec((2, tq, HEAD_DIM), lambda h, b, i: (qb + h, b * nq + i, 0)),
                  pl.BlockSpec((2, seq, HEAD_DIM), lambda h, b, i: (kb + h, b, 0)),
                  pl.BlockSpec((2, seq, HEAD_DIM), lambda h, b, i: (vb + h, b, 0)),
                  pl.BlockSpec((None, 1, 2 * seq), lambda h, b, i: (h, 0, 0)),
                  pl.BlockSpec((4, HEAD_DIM), lambda h, b, i: (0, 0)),
                  pl.BlockSpec((1, 2 * HEAD_DIM), lambda h, b, i: (0, 0))],
        out_specs=pl.BlockSpec((2, tq, HEAD_DIM), lambda h, b, i: (h, b * nq + i, 0)),
        out_shape=jax.ShapeDtypeStruct((ns, t, HEAD_DIM), BF16),
        scratch_shapes=[pltpu.VMEM((tq, 2 * seq), F32)],
        compiler_params=_params(3),
        name="diff_attn",
    )(qkv, qkv, qkv, bias_by_offset, lam, sub_g.reshape(1, 2 * HEAD_DIM))


def _rows_shape(rows, d):
    return (rows * (d // PAIR), LANES)


def _pack_pair(lo, hi):
    lo_w = lax.bitcast_convert_type(lo.astype(BF16).astype(F32), U32) >> 16
    hi_w = lax.bitcast_convert_type(hi.astype(BF16).astype(F32), U32) & jnp.uint32(0xFFFF0000)
    return lo_w | hi_w


def _unpack_pair(w):
    return (lax.bitcast_convert_type(w << 16, F32),
            lax.bitcast_convert_type(w & jnp.uint32(0xFFFF0000), F32))


def _split_bf16(v):
    hi = v.astype(BF16)
    lo = (v - hi.astype(F32)).astype(BF16)
    return hi, lo


def _outproj_kernel(nat_ref, dif_ref, x_ref, mod_ref, w_ref, g_ref, rw_ref, rb_ref,
                    x1_ref, h2_ref, lt_ref):
    ns = nat_ref.shape[0]
    a = jnp.concatenate([nat_ref[hh] for hh in range(ns)] + [dif_ref[hh] for hh in range(ns)], axis=-1)
    mix = jnp.dot(a, w_ref[...].astype(BF16), preferred_element_type=F32)
    m = mod_ref[...]
    x1 = x_ref[...] + m[2:3] * mix
    x1_ref[...] = x1
    ms = jnp.mean(x1 * x1, axis=-1, keepdims=True)
    h2 = x1 * lax.rsqrt(ms + RMS_EPS) * g_ref[...] * (1.0 + m[4:5]) + m[3:4]
    npair = h2.shape[1] // PAIR
    for p in range(npair):
        h2_ref[pl.ds(p, h2.shape[0], stride=npair), :] = _pack_pair(
            h2[:, 2 * p * LANES:(2 * p + 1) * LANES], h2[:, (2 * p + 1) * LANES:(2 * p + 2) * LANES])
    h_hi, h_lo = _split_bf16(h2)
    w_hi, w_lo = _split_bf16(rw_ref[...])
    both = jnp.dot(h_hi, jnp.concatenate([w_hi, w_lo], axis=-1), preferred_element_type=F32)
    lg = both[:, :LANES] + both[:, LANES:] + jnp.dot(h_lo, w_hi, preferred_element_type=F32)
    n_exp = lt_ref.shape[0]
    lt_ref[...] = lg.T[:n_exp] + rb_ref[...]


def _outproj(nat, dif, xf, mod3, w, g2, rw_pad, rb, seq):
    t, d = xf.shape
    ns = nat.shape[0]
    n_exp = rb.shape[0]
    tm = _tile(seq, 512)
    per_b = seq // tm
    return pl.pallas_call(
        _outproj_kernel,
        grid=(t // tm,),
        in_specs=[pl.BlockSpec((ns, tm, HEAD_DIM), lambda i: (0, i, 0)),
                  pl.BlockSpec((ns, tm, HEAD_DIM), lambda i: (0, i, 0)),
                  pl.BlockSpec((tm, d), lambda i: (i, 0)),
                  pl.BlockSpec((None, 6, d), lambda i: (i // per_b, 0, 0)),
                  pl.BlockSpec((d, d), lambda i: (0, 0), pipeline_mode=pl.Buffered(1)),
                  pl.BlockSpec((1, d), lambda i: (0, 0)),
                  pl.BlockSpec((d, LANES), lambda i: (0, 0)),
                  pl.BlockSpec((n_exp, 1), lambda i: (0, 0))],
        out_specs=[pl.BlockSpec((tm, d), lambda i: (i, 0)),
                   pl.BlockSpec((tm * (d // PAIR), LANES), lambda i: (i, 0)),
                   pl.BlockSpec((n_exp, tm), lambda i: (0, i))],
        out_shape=[jax.ShapeDtypeStruct((t, d), F32),
                   jax.ShapeDtypeStruct(_rows_shape(t, d), U32),
                   jax.ShapeDtypeStruct((n_exp, t), F32)],
        compiler_params=_params(1),
        name="outproj",
    )(nat, dif, xf, mod3, w, g2.reshape(1, d), rw_pad, rb.reshape(n_exp, 1))


def _route_kernel(lt_ref, gate_ref, pos_ref, emeta_ref, idx_ref, rank_scr, *, tm, tb):
    n_exp, t = lt_ref.shape
    eidx = lax.broadcasted_iota(I32, (n_exp, tb), 0)
    tri = (lax.broadcasted_iota(I32, (tb, tb), 0) < lax.broadcasted_iota(I32, (tb, tb), 1)).astype(BF16)

    def pass1(jb, counts):
        off = pl.multiple_of(jb * tb, tb)
        l = lt_ref[:, pl.ds(off, tb)]
        vals, sels, hots = [], [], []
        for _ in range(TOP_K):
            m = jnp.max(l, axis=0, keepdims=True)
            sel = jnp.min(jnp.where(l == m, eidx, n_exp), axis=0, keepdims=True)
            hot = eidx == sel
            vals.append(m)
            sels.append(sel)
            hots.append(hot)
            l = jnp.where(hot, -jnp.inf, l)
        exps = [jnp.exp(v - vals[0]) for v in vals]
        den = exps[0]
        for e in exps[1:]:
            den = den + e
        member = hots[0].astype(F32)
        for hot in hots[1:]:
            member = member + hot.astype(F32)
        before = jnp.dot(member.astype(BF16), tri, preferred_element_type=F32) + counts
        for k in range(TOP_K):
            rank = jnp.sum(jnp.where(hots[k], before, 0.0), axis=0, keepdims=True)
            idx_ref[pl.ds(k, 1), pl.ds(off, tb)] = sels[k]
            gate_ref[pl.ds(k, 1), pl.ds(off, tb)] = exps[k] / den
            rank_scr[pl.ds(k, 1), pl.ds(off, tb)] = rank.astype(I32)
        return counts + jnp.sum(member, axis=1, keepdims=True)

    counts = lax.fori_loop(0, t // tb, pass1, jnp.zeros((n_exp, 1), F32))

    padded = jnp.ceil(counts * (1.0 / tm)) * tm
    er = lax.broadcasted_iota(I32, (n_exp, LANES), 0)
    ec = lax.broadcasted_iota(I32, (n_exp, LANES), 1)

    def to_lanes(col):
        return jnp.sum(jnp.where(er == ec, col, 0.0), axis=0, keepdims=True)

    start = jnp.sum(jnp.where(ec < er, to_lanes(padded), 0.0), axis=1, keepdims=True)

    def pass2(jb, carry):
        off = pl.multiple_of(jb * tb, tb)
        for k in range(TOP_K):
            sel = idx_ref[pl.ds(k, 1), pl.ds(off, tb)]
            st = jnp.sum(jnp.where(eidx == sel, start, 0.0), axis=0, keepdims=True)
            pos_ref[pl.ds(k, 1), pl.ds(off, tb)] = st.astype(I32) + rank_scr[pl.ds(k, 1), pl.ds(off, tb)]
        return carry

    lax.fori_loop(0, t // tb, pass2, 0)

    n_used = jnp.sum(padded, axis=0, keepdims=True) * (1.0 / tm)
    emeta_ref[...] = jnp.zeros(emeta_ref.shape, I32)
    emeta_ref[0:1, :] = to_lanes(start).astype(I32)
    emeta_ref[1:2, :] = to_lanes(counts).astype(I32)
    emeta_ref[2:3, :] = to_lanes(padded).astype(I32)
    emeta_ref[3:4, :] = jnp.broadcast_to(n_used, (1, LANES)).astype(I32)


def _route(logits_t, tm):
    n_exp, t = logits_t.shape
    tb = _tile(t, 512)
    full = lambda shape: pl.BlockSpec(shape, lambda: (0,) * len(shape))
    return pl.pallas_call(
        functools.partial(_route_kernel, tm=tm, tb=tb),
        in_specs=[full((n_exp, t))],
        out_specs=[full((TOP_K, t)), full((TOP_K, t)), full((8, LANES))],
        out_shape=[jax.ShapeDtypeStruct((TOP_K, t), F32),
                   jax.ShapeDtypeStruct((TOP_K, t), I32),
                   jax.ShapeDtypeStruct((8, LANES), I32)],
        scratch_shapes=[pltpu.VMEM((TOP_K, t), I32), pltpu.VMEM((TOP_K, t), I32)],
        compiler_params=pltpu.CompilerParams(vmem_limit_bytes=V7X_VMEM_LIMIT_BYTES),
        name="route",
    )(logits_t)


def _dispatch_kernel(estart_ref, ecnt_ref, nused_ref, pos_hbm, h_ref, xs_hbm, pos_smem, zero_scr,
                     sem_idx, sem_row, sem_zero, *, tm, nch):
    i = pl.program_id(0)
    tmd = h_ref.shape[0] // nch
    n_exp = estart_ref.shape[0]
    n_tiles = xs_hbm.shape[0] // (tm * nch)
    slot = i % 2

    def idx_copy(step, s):
        return pltpu.make_async_copy(pos_hbm.at[step], pos_smem.at[s], sem_idx.at[s])

    @pl.when(i == 0)
    def _():
        idx_copy(0, 0).start()
        zero_scr[...] = jnp.zeros(zero_scr.shape, zero_scr.dtype)

        def fill(e, carry):
            cnt = ecnt_ref[e]
            pad = (tm - cnt % tm) % tm
            base = estart_ref[e] + cnt
            size = tm // 2
            while size >= 1:
                off = pad & ~(2 * size - 1)

                @pl.when((pad & size) != 0)
                def _(size=size, off=off):
                    dst0 = pl.multiple_of((base + off) * nch, nch)
                    cp = pltpu.make_async_copy(zero_scr.at[pl.ds(0, size * nch)],
                                               xs_hbm.at[pl.ds(dst0, size * nch)], sem_zero)
                    cp.start()
                    cp.wait()
                size //= 2
            return carry

        lax.fori_loop(0, n_exp, fill, 0)

        def tail_copy(j, part):
            dst0 = pl.multiple_of((j * tm + part * (tm // 2)) * nch, nch)
            return pltpu.make_async_copy(zero_scr, xs_hbm.at[pl.ds(dst0, tm // 2 * nch)], sem_zero)

        def tail_start(j, carry):
            tail_copy(j, 0).start()
            tail_copy(j, 1).start()
            return carry

        def tail_wait(j, carry):
            tail_copy(j, 0).wait()
            tail_copy(j, 1).wait()
            return carry

        lax.fori_loop(nused_ref[0], n_tiles, tail_start, 0)
        lax.fori_loop(nused_ref[0], n_tiles, tail_wait, 0)

    @pl.when(i + 1 < pl.num_programs(0))
    def _():
        idx_copy(i + 1, 1 - slot).start()

    idx_copy(i, slot).wait()

    def start_rows(s):
        def issue(tok, carry):
            src0 = pl.multiple_of(tok * nch, nch)
            for k in range(TOP_K):
                dst0 = pl.multiple_of(pos_smem[s, tok * TOP_K + k], nch)
                pltpu.make_async_copy(h_ref.at[pl.ds(src0, nch)], xs_hbm.at[pl.ds(dst0, nch)],
                                      sem_row).start(priority=k % 2)
            return carry

        lax.fori_loop(0, tmd, issue, 0, unroll=4)

    for s in range(2):
        pl.when(slot == s)(functools.partial(start_rows, s))
    for _ in range(TOP_K):
        pltpu.make_async_copy(h_ref, xs_hbm.at[pl.ds(0, tmd * nch)], sem_row).wait()


def _dispatch(h2, pos_tiles, estart, ecnt, n_used, tm, n_tiles, nch):
    ntt, per_tile = pos_tiles.shape
    tmd = per_tile // TOP_K
    grid_spec = pltpu.PrefetchScalarGridSpec(
        num_scalar_prefetch=3,
        grid=(ntt,),
        in_specs=[pl.BlockSpec(memory_space=pl.ANY),
                  pl.BlockSpec((tmd * nch, LANES), lambda i, es, ec, nu: (i, 0))],
        out_specs=pl.BlockSpec(memory_space=pl.ANY),
        scratch_shapes=[pltpu.SMEM((2, per_tile), I32),
                        pltpu.VMEM((tm // 2 * nch, LANES), h2.dtype),
                        pltpu.SemaphoreType.DMA((2,)), pltpu.SemaphoreType.DMA, pltpu.SemaphoreType.DMA],
    )
    return pl.pallas_call(
        functools.partial(_dispatch_kernel, tm=tm, nch=nch),
        grid_spec=grid_spec,
        out_shape=jax.ShapeDtypeStruct((n_tiles * tm * nch, LANES), h2.dtype),
        compiler_params=_params(1),
        name="dispatch",
    )(estart, ecnt, n_used, pos_tiles, h2)


GATE_UP_SPLIT = 2


(M_TILE, M_HALF, M_EXP, M_FULL, M_FIRST, M_SLOT, M_NEXT, M_HAS_NEXT) = range(8)


def _stream_group_weights(meta_ref, step, live, copies):
    @pl.when(live & (step == 0))
    def _():
        for cp in copies(meta_ref[M_EXP, 0], meta_ref[M_SLOT, 0]):
            cp.start()

    @pl.when(live & (meta_ref[M_FIRST, step] != 0))
    def _():
        slot = meta_ref[M_SLOT, step]
        for cp in copies(meta_ref[M_EXP, step], slot):
            cp.wait()

        @pl.when(meta_ref[M_HAS_NEXT, step] != 0)
        def _():
            for cp in copies(meta_ref[M_NEXT, step], 1 - slot):
                cp.start()


def _gate_up_kernel(meta_ref, nsteps_ref, x_ref, w_hbm, bg_ref, bu_ref, act_ref, wbuf, sem_w, *, chunk, npair):
    s = pl.program_id(0)
    tm, ffh = act_ref.shape
    ff = w_hbm.shape[2] // 2
    live = s < nsteps_ref[0]
    slot = meta_ref[M_SLOT, s]

    def weight_copies(e, half):
        c0 = pl.multiple_of(half * ffh, ffh)
        return (pltpu.make_async_copy(w_hbm.at[e, :, pl.ds(c0, ffh)], wbuf.at[half, 0], sem_w.at[half]),
                pltpu.make_async_copy(w_hbm.at[e, :, pl.ds(pl.multiple_of(ff + c0, ffh), ffh)], wbuf.at[half, 1],
                                      sem_w.at[half]))

    _stream_group_weights(meta_ref, s, live, weight_copies)

    def rows_pass(nrows):
        chunks = []
        for p in range(npair):
            chunks += _unpack_pair(x_ref[pl.ds(p, nrows, stride=npair), :])
        x = jnp.concatenate(chunks, axis=-1).astype(BF16)
        for c0 in range(0, ffh, chunk):
            cols = slice(c0, c0 + chunk)
            gate = jnp.dot(x, wbuf[slot, 0, :, cols].astype(BF16), preferred_element_type=F32) + bg_ref[:, cols]
            up = jnp.dot(x, wbuf[slot, 1, :, cols].astype(BF16), preferred_element_type=F32) + bu_ref[:, cols]
            gate = jnp.minimum(gate, SWIGLU_LIMIT)
            up = jnp.clip(up, -SWIGLU_LIMIT, SWIGLU_LIMIT)
            act = (up + 1.0) * gate * jax.nn.sigmoid(gate * SWIGLU_ALPHA)
            act_ref[0:nrows, cols] = act.astype(BF16)

    full = meta_ref[M_FULL, s] != 0

    @pl.when(live & full)
    def _():
        rows_pass(tm)

    @pl.when(live & jnp.logical_not(full))
    def _():
        rows_pass(tm // 2)
        act_ref[tm // 2:, :] = jnp.zeros((tm - tm // 2, ffh), BF16)

    @pl.when(jnp.logical_not(live))
    def _():
        act_ref[...] = jnp.zeros(act_ref.shape, BF16)


def _expert_of(index, ends):
    n_exp = ends.shape[0]
    e = jnp.minimum(jnp.sum((index[:, None] >= ends[None, :]).astype(I32), axis=1), n_exp - 1)
    hot = e[:, None] == jnp.arange(n_exp, dtype=I32)[None, :]
    return e, lambda table: jnp.sum(jnp.where(hot, table[None, :], 0), axis=1)


def _tile_is_full(tile, first, rows_in_group, tm):
    return (rows_in_group - (tile - first) * tm > tm // 2).astype(I32)


def _live_expert_walk(group_tiles):
    n_exp = group_tiles.shape[0]
    idx = jnp.arange(n_exp, dtype=I32)
    owns = (group_tiles > 0)[None, :]
    rank = jnp.sum((idx[None, :] < idx[:, None]) & owns, axis=1).astype(I32)
    nxt = jnp.min(jnp.where((idx[None, :] > idx[:, None]) & owns, idx[None, :], n_exp), axis=1).astype(I32)
    return rank, nxt


def _gate_up_schedule(n_used, first_tile, group_tiles, group_rows, n_tiles, tm):
    assert GATE_UP_SPLIT == 2
    n_exp = first_tile.shape[0]
    n_steps = 2 * n_used
    s = jnp.minimum(jnp.arange(2 * n_tiles, dtype=I32), jnp.maximum(n_steps - 1, 0))
    e, lookup = _expert_of(s, 2 * (first_tile + group_tiles))
    first, count = lookup(first_tile), lookup(group_tiles)
    local = s - 2 * first
    half = (local >= count).astype(I32)
    inner = local - half * count
    tile = first + (count - 1) - inner
    _, next_live = _live_expert_walk(group_tiles)
    nxt = lookup(next_live)
    rows = [None] * 8
    rows[M_TILE], rows[M_HALF], rows[M_EXP] = tile, half, e
    rows[M_FULL] = _tile_is_full(tile, first, lookup(group_rows), tm)
    rows[M_FIRST] = (inner == 0).astype(I32)
    rows[M_SLOT] = half
    rows[M_NEXT] = jnp.where(half == 0, e, jnp.minimum(nxt, n_exp - 1))
    rows[M_HAS_NEXT] = ((half == 0) | (nxt < n_exp)).astype(I32)
    return jnp.stack(rows), n_steps


def _gate_up(xs, w, b, n_used, first_tile, group_tiles, group_rows, tm):
    n_exp, d, ff2 = w.shape
    nch = d // PAIR
    rows = xs.shape[0] // nch
    ff = ff2 // 2
    n_tiles = rows // tm
    ns = GATE_UP_SPLIT
    ffh = ff // ns
    meta, n_steps = _gate_up_schedule(n_used, first_tile, group_tiles, group_rows, n_tiles, tm)

    def out_index(s, m, n):
        live = s < n[0]
        return jnp.where(live, m[M_TILE, s], s // ns), jnp.where(live, m[M_HALF, s], s % ns)

    b3 = b.reshape(n_exp, 1, ff2)
    grid_spec = pltpu.PrefetchScalarGridSpec(
        num_scalar_prefetch=2,
        grid=(ns * n_tiles,),
        in_specs=[pl.BlockSpec((tm * nch, LANES), lambda s, m, n: (m[M_TILE, s], 0)),
                  pl.BlockSpec(memory_space=pl.ANY),
                  pl.BlockSpec((None, 1, ffh), lambda s, m, n: (m[M_EXP, s], 0, m[M_HALF, s])),
                  pl.BlockSpec((None, 1, ffh), lambda s, m, n: (m[M_EXP, s], 0, ns + m[M_HALF, s]))],
        out_specs=pl.BlockSpec((tm, ffh), out_index),
        scratch_shapes=[pltpu.VMEM((2, 2, d, ffh), F32), pltpu.SemaphoreType.DMA((2,))],
    )
    return pl.pallas_call(
        functools.partial(_gate_up_kernel, chunk=_tile(ffh, 512), npair=nch),
        grid_spec=grid_spec,
        out_shape=jax.ShapeDtypeStruct((rows, ff), BF16),
        compiler_params=_params(1),
        name="gate_up",
    )(meta, n_steps, xs, w, b3, b3)


def _down_kernel(meta_ref, nused_ref, a_ref, w_hbm, b_ref, y_ref, wbuf, sem_w):
    i = pl.program_id(0)
    tm = a_ref.shape[0]
    d = w_hbm.shape[2]
    npair = d // PAIR
    chunk = _tile(d, 512)
    live = i < nused_ref[0]
    slot = meta_ref[M_SLOT, i]

    def weight_copies(e, slot_):
        return (pltpu.make_async_copy(w_hbm.at[e], wbuf.at[slot_], sem_w.at[slot_]),)

    _stream_group_weights(meta_ref, i, live, weight_copies)

    def rows_pass(nrows):
        a = a_ref[0:nrows, :]
        for c0 in range(0, d, chunk):
            y = (jnp.dot(a, wbuf[slot, :, c0:c0 + chunk].astype(BF16), preferred_element_type=F32)
                 + b_ref[:, c0:c0 + chunk])
            for q in range(chunk // PAIR):
                y_ref[pl.ds(c0 // PAIR + q, nrows, stride=npair), :] = _pack_pair(
                    y[:, 2 * q * LANES:(2 * q + 1) * LANES], y[:, (2 * q + 1) * LANES:(2 * q + 2) * LANES])

    full = meta_ref[M_FULL, i] != 0

    @pl.when(live & full)
    def _():
        rows_pass(tm)

    @pl.when(live & jnp.logical_not(full))
    def _():
        rows_pass(tm // 2)
        y_ref[tm // 2 * npair:, :] = jnp.zeros(((tm - tm // 2) * npair, LANES), U32)

    @pl.when(jnp.logical_not(live))
    def _():
        y_ref[...] = jnp.zeros(y_ref.shape, U32)


def _down_schedule(n_used, first_tile, group_tiles, group_rows, n_tiles, tm):
    n_exp = first_tile.shape[0]
    steps = jnp.arange(n_tiles, dtype=I32)
    e, lookup = _expert_of(steps, first_tile + group_tiles)
    first = lookup(first_tile)
    inner = steps - first
    tile = jnp.where(steps < n_used, first + (lookup(group_tiles) - 1) - inner, steps)
    rank, next_live = _live_expert_walk(group_tiles)
    nxt = lookup(next_live)
    rows = [None] * 8
    rows[M_TILE], rows[M_HALF], rows[M_EXP] = tile, jnp.zeros_like(steps), e
    rows[M_FULL] = _tile_is_full(tile, first, lookup(group_rows), tm)
    rows[M_FIRST] = (inner == 0).astype(I32)
    rows[M_SLOT] = lookup(rank) % 2
    rows[M_NEXT] = jnp.minimum(nxt, n_exp - 1)
    rows[M_HAS_NEXT] = (nxt < n_exp).astype(I32)
    return jnp.stack(rows)


def _down(act, w, b, n_used, first_tile, group_tiles, group_rows, tm):
    rows, ff = act.shape
    n_exp, _, d = w.shape
    nch = d // PAIR
    n_tiles = rows // tm
    meta = _down_schedule(n_used, first_tile, group_tiles, group_rows, n_tiles, tm)
    grid_spec = pltpu.PrefetchScalarGridSpec(
        num_scalar_prefetch=2,
        grid=(n_tiles,),
        in_specs=[pl.BlockSpec((tm, ff), lambda i, m, nu: (m[M_TILE, jnp.minimum(i, nu[0] - 1)], 0)),
                  pl.BlockSpec(memory_space=pl.ANY),
                  pl.BlockSpec((None, 1, d), lambda i, m, nu: (m[M_EXP, i], 0, 0))],
        out_specs=pl.BlockSpec((tm * nch, LANES), lambda i, m, nu: (m[M_TILE, i], 0)),
        scratch_shapes=[pltpu.VMEM((2, ff, d), F32), pltpu.SemaphoreType.DMA((2,))],
    )
    return pl.pallas_call(
        _down_kernel,
        grid_spec=grid_spec,
        out_shape=jax.ShapeDtypeStruct(_rows_shape(rows, d), U32),
        compiler_params=_params(1),
        name="down",
    )(meta, n_used, act, w, b.reshape(n_exp, 1, d))


def _combine_kernel(pos_hbm, y_hbm, x1_ref, mod_ref, g_ref, o_ref, pos_smem, ybuf, sem_idx, sem_row):
    i = pl.program_id(0)
    n = pl.num_programs(0)
    tmc, d = x1_ref.shape
    nch = d // PAIR
    slot = i % 2

    def idx_copy(step, s):
        return pltpu.make_async_copy(pos_hbm.at[step], pos_smem.at[s], sem_idx.at[s])

    def start_gathers_static(s):
        def issue(tok, carry):
            dst0 = pl.multiple_of(tok * nch, nch)
            for k in range(TOP_K):
                src0 = pl.multiple_of(pos_smem[s, tok * TOP_K + k], nch)
                pltpu.make_async_copy(y_hbm.at[pl.ds(src0, nch)], ybuf.at[s, k, pl.ds(dst0, nch)],
                                      sem_row.at[s]).start(priority=k % 2)
            return carry

        lax.fori_loop(0, tmc, issue, 0, unroll=4)

    def start_gathers(s):
        for static_s in range(2):
            pl.when(s == static_s)(functools.partial(start_gathers_static, static_s))

    @pl.when(i == 0)
    def _():
        idx_copy(0, 0).start()
        idx_copy(0, 0).wait()
        start_gathers_static(0)

        @pl.when(n > 1)
        def _():
            idx_copy(1, 1).start()

    @pl.when(i + 1 < n)
    def _():
        idx_copy(i + 1, 1 - slot).wait()
        start_gathers(1 - slot)

    @pl.when(i + 2 < n)
    def _():
        idx_copy(i + 2, slot).start()

    for k in range(TOP_K):
        pltpu.make_async_copy(y_hbm.at[pl.ds(0, tmc * nch)], ybuf.at[slot, k], sem_row.at[slot]).wait()

    g = g_ref[...]
    gate2 = mod_ref[...][5:6]
    for p in range(nch):
        lo_sum = hi_sum = None
        for k in range(TOP_K):
            lo, hi = _unpack_pair(ybuf[slot, k, pl.ds(p, tmc, stride=nch), :])
            gk = g[:, k:k + 1]
            lo_sum = lo * gk if lo_sum is None else lo_sum + lo * gk
            hi_sum = hi * gk if hi_sum is None else hi_sum + hi * gk
        for half, moe in enumerate((lo_sum, hi_sum)):
            cols = slice((2 * p + half) * LANES, (2 * p + half + 1) * LANES)
            o_ref[:, cols] = x1_ref[:, cols] + gate2[:, cols] * moe


def _combine(y, pos_tiles, x1, mod3, gates_t, seq):
    t, d = x1.shape
    ntt, per_tile = pos_tiles.shape
    tmc = per_tile // TOP_K
    per_b = seq // tmc
    return pl.pallas_call(
        _combine_kernel,
        grid=(ntt,),
        in_specs=[pl.BlockSpec(memory_space=pl.ANY),
                  pl.BlockSpec(memory_space=pl.ANY),
                  pl.BlockSpec((tmc, d), lambda i: (i, 0)),
                  pl.BlockSpec((None, 6, d), lambda i: (i // per_b, 0, 0)),
                  pl.BlockSpec((tmc, TOP_K), lambda i: (i, 0))],
        out_specs=pl.BlockSpec((tmc, d), lambda i: (i, 0)),
        out_shape=jax.ShapeDtypeStruct((t, d), F32),
        scratch_shapes=[pltpu.SMEM((2, per_tile), I32),
                        pltpu.VMEM((2, TOP_K, tmc * (d // PAIR), LANES), U32),
                        pltpu.SemaphoreType.DMA((2,)), pltpu.SemaphoreType.DMA((2,))],
        compiler_params=_params(1),
        name="combine",
    )(pos_tiles, y, x1, mod3, gates_t)


def _moe(h2, logits_t, x1, mod3, w_gu, b_gu, w_dn, b_dn, seq):
    t, d = x1.shape
    n_exp = logits_t.shape[0]
    tm = _tile(t * TOP_K, 512)
    n_tiles = (t * TOP_K) // tm + n_exp
    gates, pos, emeta = _route(logits_t, tm)
    tok_tile = _tile(seq, 256)
    pos_tiles = (pos.T * (d // PAIR)).reshape(t // tok_tile, tok_tile * TOP_K)
    n_used = emeta[3, :1]
    group_start, group_rows = emeta[0, :n_exp], emeta[1, :n_exp]
    first_tile, group_tiles = group_start // tm, emeta[2, :n_exp] // tm
    xs = _dispatch(h2, pos_tiles, group_start, group_rows, n_used, tm, n_tiles, d // PAIR)
    act = _gate_up(xs, w_gu, b_gu, n_used, first_tile, group_tiles, group_rows, tm)
    y = _down(act, w_dn, b_dn, n_used, first_tile, group_tiles, group_rows, tm)
    return _combine(y, pos_tiles, x1, mod3, gates.T, seq)


def kernel(x, c, w_ada, b_ada, norm1_g, w_in, nat_q_g, nat_k_g, nat_rpb, diff_q_g, diff_k_g, diff_lambda,
           diff_sub_g, rel_bias_table, w_out, norm2_g, router_w, router_b, w_gate_up, b_gate_up, w_down, b_down):
    bsz, seq, d = x.shape
    t = bsz * seq
    ns = (d // 2) // HEAD_DIM
    n_exp = router_w.shape[-1]
    scale = HEAD_DIM ** -0.5
    t5_bias = _t5_bias_by_offset(rel_bias_table, seq)
    ones = jnp.ones((HEAD_DIM,), F32)
    xf = x.reshape(t, d)
    for l in range(w_ada.shape[0]):
        lambda_init = 0.8 - 0.6 * math.exp(-0.3 * l)
        mod3 = _adaln(c, w_ada[l], b_ada[l]).reshape(bsz, 6, d)
        gains = jnp.stack([nat_q_g[l] * scale, nat_k_g[l], ones, diff_q_g[l] * (scale * LOG2_E), diff_k_g[l],
                           ones, ones, ones])
        qkv = _inproj(xf, mod3, norm1_g[l], w_in[l], gains, seq)
        nat = _nat_attention(qkv, nat_rpb[l], bsz, seq, ns)
        dif = _diff_attention(qkv, t5_bias, diff_lambda[l], diff_sub_g[l], lambda_init, bsz, seq, ns)
        rw_pad = jnp.pad(router_w[l], ((0, 0), (0, LANES - n_exp)))
        x1, h2, logits_t = _outproj(nat, dif, xf, mod3, w_out[l], norm2_g[l], rw_pad, router_b[l], seq)
        xf = _moe(h2, logits_t, x1, mod3, w_gate_up[l], b_gate_up[l], w_down[l], b_down[l], seq)
    return xf.reshape(bsz, seq, d)
```

```python
import functools
import math

import jax
import jax.numpy as jnp
from jax import lax
from jax.experimental import pallas as pl
from jax.experimental.pallas import tpu as pltpu

HEAD_DIM = 128
GRID_W = 64
NAT_KR_MAX = 8
NAT_KC = 16
N_BUCKETS = 32
MAX_DISTANCE = 128
TOP_K = 4
SWIGLU_LIMIT = 7.0
SWIGLU_ALPHA = 1.702
RMS_EPS = 1e-6
NEG_INF = -1e30
LOG2_E = 1.4426950408889634

LANES = 128
PAIR = 2 * LANES
V7X_VMEM_LIMIT_BYTES = 56 * 1024 * 1024

F32 = jnp.float32
BF16 = jnp.bfloat16
I32 = jnp.int32
U32 = jnp.uint32


def _tile(n, pref):
    t = min(n, pref)
    assert n % t == 0, (n, pref)
    return t


def _params(n_axes):
    return pltpu.CompilerParams(dimension_semantics=("arbitrary",) * n_axes,
                                vmem_limit_bytes=V7X_VMEM_LIMIT_BYTES)


def _adaln_kernel(c_ref, w_ref, b_ref, o_ref):
    c = c_ref[...]
    a = (c * jax.nn.sigmoid(c)).astype(BF16)
    o_ref[...] = jnp.dot(a, w_ref[...].astype(BF16), preferred_element_type=F32) + b_ref[...]


def _adaln(c, w, b):
    bsz, d = c.shape
    n = w.shape[1]
    tn = _tile(n, 1024)
    return pl.pallas_call(
        _adaln_kernel,
        grid=(n // tn,),
        in_specs=[pl.BlockSpec((bsz, d), lambda j: (0, 0)),
                  pl.BlockSpec((d, tn), lambda j: (0, j)),
                  pl.BlockSpec((1, tn), lambda j: (0, j))],
        out_specs=pl.BlockSpec((bsz, tn), lambda j: (0, j)),
        out_shape=jax.ShapeDtypeStruct((bsz, n), F32),
        compiler_params=_params(1),
        name="adaln",
    )(c, w, b.reshape(1, n))


def _inproj_kernel(x_ref, mod_ref, g_ref, w_ref, gain_ref, o_ref, h_scr):
    j = pl.program_id(1)

    @pl.when(j == 0)
    def _():
        x = x_ref[...]
        ms = jnp.mean(x * x, axis=-1, keepdims=True)
        y = x * lax.rsqrt(ms + RMS_EPS) * g_ref[...]
        m = mod_ref[...]
        h_scr[...] = (y * (1.0 + m[1:2]) + m[0:1]).astype(BF16)

    heads = o_ref.shape[0]
    per = min(heads, PAIR // HEAD_DIM)
    is_value = (j == 2) | (j == 5)
    g = gain_ref[pl.ds(j, 1), :]
    h = h_scr[...]
    for c0 in range(0, heads, per):
        y = jnp.dot(h, w_ref[:, c0 * HEAD_DIM:(c0 + per) * HEAD_DIM].astype(BF16), preferred_element_type=F32)
        for hh in range(per):
            yh = y[:, hh * HEAD_DIM:(hh + 1) * HEAD_DIM]
            ms = jnp.mean(yh * yh, axis=-1, keepdims=True)
            factor = jnp.where(is_value, 1.0, lax.rsqrt(ms + RMS_EPS))
            o_ref[c0 + hh] = (yh * factor * g).astype(BF16)


def _inproj(xf, mod3, g1, w, gains, seq):
    t, d = xf.shape
    seg = d // 2
    ns = seg // HEAD_DIM
    tm = _tile(seq, 1024)
    per_b = seq // tm
    return pl.pallas_call(
        _inproj_kernel,
        grid=(t // tm, 6),
        in_specs=[pl.BlockSpec((tm, d), lambda i, j: (i, 0)),
                  pl.BlockSpec((None, 6, d), lambda i, j: (i // per_b, 0, 0)),
                  pl.BlockSpec((1, d), lambda i, j: (0, 0)),
                  pl.BlockSpec((d, seg), lambda i, j: (0, j)),
                  pl.BlockSpec((8, HEAD_DIM), lambda i, j: (0, 0))],
        out_specs=pl.BlockSpec((ns, tm, HEAD_DIM), lambda i, j: (j, i, 0)),
        out_shape=jax.ShapeDtypeStruct((6 * ns, t, HEAD_DIM), BF16),
        scratch_shapes=[pltpu.VMEM((tm, d), BF16)],
        compiler_params=_params(2),
        name="inproj",
    )(xf, mod3, g1.reshape(1, d), w, gains)


def _nat_kernel(q_ref, k_ref, v_ref, w_ref, o_ref, b_ref, *, rows, kr, group):
    band = kr * GRID_W

    @pl.when(pl.program_id(1) == 0)
    def _():
        qc = lax.broadcasted_iota(I32, (GRID_W, LANES), 0)
        lane = lax.broadcasted_iota(I32, (GRID_W, LANES), 1)
        kc = lane % GRID_W
        col_start = jnp.clip(qc - NAT_KC // 2, 0, GRID_W - NAT_KC)
        visible = (kc >= col_start) & (kc < col_start + NAT_KC)
        for d in range(kr):
            for j in range(0, kr, 2):
                def tile(jj, shift):
                    ro = jj - d + (NAT_KR_MAX - 1)
                    row = jnp.broadcast_to(w_ref[ro:ro + 1, :], (GRID_W, LANES))
                    return pltpu.roll(row, shift, 1, stride=1, stride_axis=0)
                both = jnp.where(lane < GRID_W, tile(j, 0), tile(j + 1, GRID_W))
                b_ref[d, :, j * GRID_W:(j + 2) * GRID_W] = jnp.where(visible, both, NEG_INF)

    def body(g, carry):
        k0s, scores = [], []
        for i in range(group):
            r = g * group + i
            rs = jnp.clip(r - kr // 2, 0, rows - kr)
            k0 = pl.multiple_of(rs * GRID_W, GRID_W)
            q = q_ref[pl.ds(pl.multiple_of(r * GRID_W, GRID_W), GRID_W), :]
            s = lax.dot_general(q, k_ref[pl.ds(k0, band), :], (((1,), (1,)), ((), ())),
                                preferred_element_type=F32)
            k0s.append(k0)
            scores.append(s + b_ref[r - rs])
        s = jnp.concatenate(scores, axis=0)
        p = jnp.exp(s - jnp.max(s, axis=-1, keepdims=True))
        den = jnp.sum(p, axis=-1, keepdims=True)
        pb = p.astype(BF16)
        outs = [jnp.dot(pb[i * GRID_W:(i + 1) * GRID_W], v_ref[pl.ds(k0s[i], band), :],
                        preferred_element_type=F32) for i in range(group)]
        o = jnp.concatenate(outs, axis=0) / den
        o_ref[pl.ds(pl.multiple_of(g * (group * GRID_W), group * GRID_W), group * GRID_W), :] = o.astype(BF16)
        return carry

    lax.fori_loop(0, rows // group, body, 0)


def _nat_bias_by_offset(rpb):
    lane = jnp.arange(LANES)
    off = jnp.where(lane < LANES // 2, lane, lane - LANES)
    return jnp.take(rpb.astype(F32), jnp.clip(off, -(NAT_KC - 1), NAT_KC - 1) + (NAT_KC - 1), axis=-1)


def _nat_attention(qkv, rpb, bsz, seq, ns):
    t = qkv.shape[1]
    rows = seq // GRID_W
    kr = min(NAT_KR_MAX, rows)
    assert kr % 2 == 0 and 2 * GRID_W == LANES
    blk = (None, seq, HEAD_DIM)
    n_off = 2 * NAT_KR_MAX - 1
    return pl.pallas_call(
        functools.partial(_nat_kernel, rows=rows, kr=kr, group=_tile(rows, 32)),
        grid=(ns, bsz),
        in_specs=[pl.BlockSpec(blk, lambda h, b: (h, b, 0)),
                  pl.BlockSpec(blk, lambda h, b: (ns + h, b, 0)),
                  pl.BlockSpec(blk, lambda h, b: (2 * ns + h, b, 0)),
                  pl.BlockSpec((None, n_off, LANES), lambda h, b: (h, 0, 0))],
        out_specs=pl.BlockSpec(blk, lambda h, b: (h, b, 0)),
        out_shape=jax.ShapeDtypeStruct((ns, t, HEAD_DIM), BF16),
        scratch_shapes=[pltpu.VMEM((kr, GRID_W, kr * GRID_W), F32)],
        compiler_params=_params(2),
        name="nat_attn",
    )(qkv, qkv, qkv, _nat_bias_by_offset(rpb))


def _t5_bucket(rel):
    nb = N_BUCKETS // 2
    max_exact = nb // 2
    ret = jnp.where(rel > 0, nb, 0)
    n = jnp.abs(rel)
    nf = jnp.maximum(n, 1).astype(F32)
    large = max_exact + (jnp.log(nf / max_exact) / math.log(MAX_DISTANCE / max_exact)
                         * (nb - max_exact)).astype(I32)
    large = jnp.minimum(large, nb - 1)
    return ret + jnp.where(n < max_exact, n, large)


def _t5_bias_by_offset(rel_table, seq):
    rel = jnp.arange(2 * seq, dtype=I32) - seq
    hot = (_t5_bucket(rel)[None, :] == jnp.arange(N_BUCKETS, dtype=I32)[:, None]).astype(F32)
    u = jnp.dot(rel_table.astype(F32).T, hot, precision=lax.Precision.HIGHEST) * LOG2_E
    return u.reshape(u.shape[0], 1, 2 * seq)


def _diff_kernel(q_ref, k_ref, v_ref, u_ref, lam_ref, sg_ref, o_ref, strip_scr, *, seq, tq, nq, n_split,
                 lambda_init):
    qi = pl.program_id(2)

    @pl.when((pl.program_id(1) == 0) & (qi == 0))
    def _():
        strip_scr[...] = pltpu.roll(jnp.broadcast_to(u_ref[...], strip_scr.shape), 0, 1, stride=1, stride_axis=0)

    bias = strip_scr[:, pl.ds(pl.multiple_of((nq - qi) * tq, tq), seq)]
    lam = lam_ref[...]
    lam_full = (jnp.exp(jnp.sum(lam[0:1] * lam[1:2], axis=-1, keepdims=True))
                - jnp.exp(jnp.sum(lam[2:3] * lam[3:4], axis=-1, keepdims=True)) + lambda_init)

    v = jnp.concatenate([v_ref[0], v_ref[1]], axis=-1)
    half = tq // n_split

    def scores(p, rows):
        s = lax.dot_general(q_ref[p, rows, :], k_ref[p], (((1,), (1,)), ((), ())), preferred_element_type=F32)
        return s + bias[rows]

    def attend(s1, s2):
        e1 = jnp.exp2(s1 - jnp.max(s1, axis=-1, keepdims=True))
        e2 = jnp.exp2(s2 - jnp.max(s2, axis=-1, keepdims=True))
        r1 = 1.0 / jnp.sum(e1, axis=-1, keepdims=True)
        r2 = lam_full / jnp.sum(e2, axis=-1, keepdims=True)
        o = jnp.dot((e1 * r1 - e2 * r2).astype(BF16), v, preferred_element_type=F32)
        ms = jnp.mean(o * o, axis=-1, keepdims=True)
        return o * lax.rsqrt(ms + RMS_EPS) * sg_ref[...] * (1.0 - lambda_init)

    blocks = [slice(j * half, (j + 1) * half) for j in range(n_split)]
    all_scores = [(scores(0, rows), scores(1, rows)) for rows in blocks]
    for rows, (s1, s2) in zip(blocks, all_scores):
        o = attend(s1, s2)
        o_ref[0, rows, :] = o[:, :HEAD_DIM].astype(BF16)
        o_ref[1, rows, :] = o[:, HEAD_DIM:].astype(BF16)


def _diff_attention(qkv, bias_by_offset, lam, sub_g, lambda_init, bsz, seq, ns):
    t = qkv.shape[1]
    hd = ns // 2
    tq = _tile(seq, 512)
    nq = seq // tq
    qb, kb, vb = 3 * ns // 2, 4 * ns // 2, 5 * ns // 2
    return pl.pallas_call(
        functools.partial(_diff_kernel, seq=seq, tq=tq, nq=nq, n_split=tq // 128, lambda_init=lambda_init),
        grid=(hd, bsz, nq),
        in_specs=[pl.BlockSpec((2, tq, HEAD_DIM), lambda h, b, i: (qb + h, b * nq + i, 0)),
                  pl.BlockSpec((2, seq, HEAD_DIM), lambda h, b, i: (kb + h, b, 0)),
                  pl.BlockSpec((2, seq, HEAD_DIM), lambda h, b, i: (vb + h, b, 0)),
                  pl.BlockSpec((None, 1, 2 * seq), lambda h, b, i: (h, 0, 0)),
                  pl.BlockSpec((4, HEAD_DIM), lambda h, b, i: (0, 0)),
                  pl.BlockSpec((1, 2 * HEAD_DIM), lambda h, b, i: (0, 0))],
        out_specs=pl.BlockSpec((2, tq, HEAD_DIM), lambda h, b, i: (h, b * nq + i, 0)),
        out_shape=jax.ShapeDtypeStruct((ns, t, HEAD_DIM), BF16),
        scratch_shapes=[pltpu.VMEM((tq, 2 * seq), F32)],
        compiler_params=_params(3),
        name="diff_attn",
    )(qkv, qkv, qkv, bias_by_offset, lam, sub_g.reshape(1, 2 * HEAD_DIM))


def _rows_shape(rows, d):
    return (rows * (d // PAIR), LANES)


def _pack_pair(lo, hi):
    lo_w = lax.bitcast_convert_type(lo.astype(BF16).astype(F32), U32) >> 16
    hi_w = lax.bitcast_convert_type(hi.astype(BF16).astype(F32), U32) & jnp.uint32(0xFFFF0000)
    return lo_w | hi_w


def _unpack_pair(w):
    return (lax.bitcast_convert_type(w << 16, F32),
            lax.bitcast_convert_type(w & jnp.uint32(0xFFFF0000), F32))


def _split_bf16(v):
    hi = v.astype(BF16)
    lo = (v - hi.astype(F32)).astype(BF16)
    return hi, lo


def _outproj_kernel(nat_ref, dif_ref, x_ref, mod_ref, w_ref, g_ref, rw_ref, rb_ref,
                    x1_ref, h2_ref, lt_ref):
    ns = nat_ref.shape[0]
    a = jnp.concatenate([nat_ref[hh] for hh in range(ns)] + [dif_ref[hh] for hh in range(ns)], axis=-1)
    mix = jnp.dot(a, w_ref[...].astype(BF16), preferred_element_type=F32)
    m = mod_ref[...]
    x1 = x_ref[...] + m[2:3] * mix
    x1_ref[...] = x1
    ms = jnp.mean(x1 * x1, axis=-1, keepdims=True)
    h2 = x1 * lax.rsqrt(ms + RMS_EPS) * g_ref[...] * (1.0 + m[4:5]) + m[3:4]
    npair = h2.shape[1] // PAIR
    for p in range(npair):
        h2_ref[pl.ds(p, h2.shape[0], stride=npair), :] = _pack_pair(
            h2[:, 2 * p * LANES:(2 * p + 1) * LANES], h2[:, (2 * p + 1) * LANES:(2 * p + 2) * LANES])
    h_hi, h_lo = _split_bf16(h2)
    w_hi, w_lo = _split_bf16(rw_ref[...])
    both = jnp.dot(h_hi, jnp.concatenate([w_hi, w_lo], axis=-1), preferred_element_type=F32)
    lg = both[:, :LANES] + both[:, LANES:] + jnp.dot(h_lo, w_hi, preferred_element_type=F32)
    n_exp = lt_ref.shape[0]
    lt_ref[...] = lg.T[:n_exp] + rb_ref[...]


def _outproj(nat, dif, xf, mod3, w, g2, rw_pad, rb, seq):
    t, d = xf.shape
    ns = nat.shape[0]
    n_exp = rb.shape[0]
    tm = _tile(seq, 512)
    per_b = seq // tm
    return pl.pallas_call(
        _outproj_kernel,
        grid=(t // tm,),
        in_specs=[pl.BlockSpec((ns, tm, HEAD_DIM), lambda i: (0, i, 0)),
                  pl.BlockSpec((ns, tm, HEAD_DIM), lambda i: (0, i, 0)),
                  pl.BlockSpec((tm, d), lambda i: (i, 0)),
                  pl.BlockSpec((None, 6, d), lambda i: (i // per_b, 0, 0)),
                  pl.BlockSpec((d, d), lambda i: (0, 0), pipeline_mode=pl.Buffered(1)),
                  pl.BlockSpec((1, d), lambda i: (0, 0)),
                  pl.BlockSpec((d, LANES), lambda i: (0, 0)),
                  pl.BlockSpec((n_exp, 1), lambda i: (0, 0))],
        out_specs=[pl.BlockSpec((tm, d), lambda i: (i, 0)),
                   pl.BlockSpec((tm * (d // PAIR), LANES), lambda i: (i, 0)),
                   pl.BlockSpec((n_exp, tm), lambda i: (0, i))],
        out_shape=[jax.ShapeDtypeStruct((t, d), F32),
                   jax.ShapeDtypeStruct(_rows_shape(t, d), U32),
                   jax.ShapeDtypeStruct((n_exp, t), F32)],
        compiler_params=_params(1),
        name="outproj",
    )(nat, dif, xf, mod3, w, g2.reshape(1, d), rw_pad, rb.reshape(n_exp, 1))


def _route_kernel(lt_ref, gate_ref, pos_ref, emeta_ref, idx_ref, rank_scr, *, tm, tb):
    n_exp, t = lt_ref.shape
    eidx = lax.broadcasted_iota(I32, (n_exp, tb), 0)
    tri = (lax.broadcasted_iota(I32, (tb, tb), 0) < lax.broadcasted_iota(I32, (tb, tb), 1)).astype(BF16)

    def pass1(jb, counts):
        off = pl.multiple_of(jb * tb, tb)
        l = lt_ref[:, pl.ds(off, tb)]
        vals, sels, hots = [], [], []
        for _ in range(TOP_K):
            m = jnp.max(l, axis=0, keepdims=True)
            sel = jnp.min(jnp.where(l == m, eidx, n_exp), axis=0, keepdims=True)
            hot = eidx == sel
            vals.append(m)
            sels.append(sel)
            hots.append(hot)
            l = jnp.where(hot, -jnp.inf, l)
        exps = [jnp.exp(v - vals[0]) for v in vals]
        den = exps[0]
        for e in exps[1:]:
            den = den + e
        member = hots[0].astype(F32)
        for hot in hots[1:]:
            member = member + hot.astype(F32)
        before = jnp.dot(member.astype(BF16), tri, preferred_element_type=F32) + counts
        for k in range(TOP_K):
            rank = jnp.sum(jnp.where(hots[k], before, 0.0), axis=0, keepdims=True)
            idx_ref[pl.ds(k, 1), pl.ds(off, tb)] = sels[k]
            gate_ref[pl.ds(k, 1), pl.ds(off, tb)] = exps[k] / den
            rank_scr[pl.ds(k, 1), pl.ds(off, tb)] = rank.astype(I32)
        return counts + jnp.sum(member, axis=1, keepdims=True)

    counts = lax.fori_loop(0, t // tb, pass1, jnp.zeros((n_exp, 1), F32))

    padded = jnp.ceil(counts * (1.0 / tm)) * tm
    er = lax.broadcasted_iota(I32, (n_exp, LANES), 0)
    ec = lax.broadcasted_iota(I32, (n_exp, LANES), 1)

    def to_lanes(col):
        return jnp.sum(jnp.where(er == ec, col, 0.0), axis=0, keepdims=True)

    start = jnp.sum(jnp.where(ec < er, to_lanes(padded), 0.0), axis=1, keepdims=True)

    def pass2(jb, carry):
        off = pl.multiple_of(jb * tb, tb)
        for k in range(TOP_K):
            sel = idx_ref[pl.ds(k, 1), pl.ds(off, tb)]
            st = jnp.sum(jnp.where(eidx == sel, start, 0.0), axis=0, keepdims=True)
            pos_ref[pl.ds(k, 1), pl.ds(off, tb)] = st.astype(I32) + rank_scr[pl.ds(k, 1), pl.ds(off, tb)]
        return carry

    lax.fori_loop(0, t // tb, pass2, 0)

    n_used = jnp.sum(padded, axis=0, keepdims=True) * (1.0 / tm)
    emeta_ref[...] = jnp.zeros(emeta_ref.shape, I32)
    emeta_ref[0:1, :] = to_lanes(start).astype(I32)
    emeta_ref[1:2, :] = to_lanes(counts).astype(I32)
    emeta_ref[2:3, :] = to_lanes(padded).astype(I32)
    emeta_ref[3:4, :] = jnp.broadcast_to(n_used, (1, LANES)).astype(I32)


def _route(logits_t, tm):
    n_exp, t = logits_t.shape
    tb = _tile(t, 512)
    full = lambda shape: pl.BlockSpec(shape, lambda: (0,) * len(shape))
    return pl.pallas_call(
        functools.partial(_route_kernel, tm=tm, tb=tb),
        in_specs=[full((n_exp, t))],
        out_specs=[full((TOP_K, t)), full((TOP_K, t)), full((8, LANES))],
        out_shape=[jax.ShapeDtypeStruct((TOP_K, t), F32),
                   jax.ShapeDtypeStruct((TOP_K, t), I32),
                   jax.ShapeDtypeStruct((8, LANES), I32)],
        scratch_shapes=[pltpu.VMEM((TOP_K, t), I32), pltpu.VMEM((TOP_K, t), I32)],
        compiler_params=pltpu.CompilerParams(vmem_limit_bytes=V7X_VMEM_LIMIT_BYTES),
        name="route",
    )(logits_t)


def _dispatch_kernel(estart_ref, ecnt_ref, nused_ref, pos_hbm, h_ref, xs_hbm, pos_smem, zero_scr,
                     sem_idx, sem_row, sem_zero, *, tm, nch):
    i = pl.program_id(0)
    tmd = h_ref.shape[0] // nch
    n_exp = estart_ref.shape[0]
    n_tiles = xs_hbm.shape[0] // (tm * nch)
    slot = i % 2

    def idx_copy(step, s):
        return pltpu.make_async_copy(pos_hbm.at[step], pos_smem.at[s], sem_idx.at[s])

    @pl.when(i == 0)
    def _():
        idx_copy(0, 0).start()
        zero_scr[...] = jnp.zeros(zero_scr.shape, zero_scr.dtype)

        def fill(e, carry):
            cnt = ecnt_ref[e]
            pad = (tm - cnt % tm) % tm
            base = estart_ref[e] + cnt
            size = tm // 2
            while size >= 1:
                off = pad & ~(2 * size - 1)

                @pl.when((pad & size) != 0)
                def _(size=size, off=off):
                    dst0 = pl.multiple_of((base + off) * nch, nch)
                    cp = pltpu.make_async_copy(zero_scr.at[pl.ds(0, size * nch)],
                                               xs_hbm.at[pl.ds(dst0, size * nch)], sem_zero)
                    cp.start()
                    cp.wait()
                size //= 2
            return carry

        lax.fori_loop(0, n_exp, fill, 0)

        def tail_copy(j, part):
            dst0 = pl.multiple_of((j * tm + part * (tm // 2)) * nch, nch)
            return pltpu.make_async_copy(zero_scr, xs_hbm.at[pl.ds(dst0, tm // 2 * nch)], sem_zero)

        def tail_start(j, carry):
            tail_copy(j, 0).start()
            tail_copy(j, 1).start()
            return carry

        def tail_wait(j, carry):
            tail_copy(j, 0).wait()
            tail_copy(j, 1).wait()
            return carry

        lax.fori_loop(nused_ref[0], n_tiles, tail_start, 0)
        lax.fori_loop(nused_ref[0], n_tiles, tail_wait, 0)

    @pl.when(i + 1 < pl.num_programs(0))
    def _():
        idx_copy(i + 1, 1 - slot).start()

    idx_copy(i, slot).wait()

    def start_rows(s):
        def issue(tok, carry):
            src0 = pl.multiple_of(tok * nch, nch)
            for k in range(TOP_K):
                dst0 = pl.multiple_of(pos_smem[s, tok * TOP_K + k], nch)
                pltpu.make_async_copy(h_ref.at[pl.ds(src0, nch)], xs_hbm.at[pl.ds(dst0, nch)],
                                      sem_row).start(priority=k % 2)
            return carry

        lax.fori_loop(0, tmd, issue, 0, unroll=4)

    for s in range(2):
        pl.when(slot == s)(functools.partial(start_rows, s))
    for _ in range(TOP_K):
        pltpu.make_async_copy(h_ref, xs_hbm.at[pl.ds(0, tmd * nch)], sem_row).wait()


def _dispatch(h2, pos_tiles, estart, ecnt, n_used, tm, n_tiles, nch):
    ntt, per_tile = pos_tiles.shape
    tmd = per_tile // TOP_K
    grid_spec = pltpu.PrefetchScalarGridSpec(
        num_scalar_prefetch=3,
        grid=(ntt,),
        in_specs=[pl.BlockSpec(memory_space=pl.ANY),
                  pl.BlockSpec((tmd * nch, LANES), lambda i, es, ec, nu: (i, 0))],
        out_specs=pl.BlockSpec(memory_space=pl.ANY),
        scratch_shapes=[pltpu.SMEM((2, per_tile), I32),
                        pltpu.VMEM((tm // 2 * nch, LANES), h2.dtype),
                        pltpu.SemaphoreType.DMA((2,)), pltpu.SemaphoreType.DMA, pltpu.SemaphoreType.DMA],
    )
    return pl.pallas_call(
        functools.partial(_dispatch_kernel, tm=tm, nch=nch),
        grid_spec=grid_spec,
        out_shape=jax.ShapeDtypeStruct((n_tiles * tm * nch, LANES), h2.dtype),
        compiler_params=_params(1),
        name="dispatch",
    )(estart, ecnt, n_used, pos_tiles, h2)


GATE_UP_SPLIT = 2


(M_TILE, M_HALF, M_EXP, M_PARTS, M_FIRST, M_SLOT, M_NEXT, M_HAS_NEXT) = range(8)
TILE_PARTS = 4


def _stream_group_weights(meta_ref, step, live, copies):
    @pl.when(live & (step == 0))
    def _():
        for cp in copies(meta_ref[M_EXP, 0], meta_ref[M_SLOT, 0]):
            cp.start()

    @pl.when(live & (meta_ref[M_FIRST, step] != 0))
    def _():
        slot = meta_ref[M_SLOT, step]
        for cp in copies(meta_ref[M_EXP, step], slot):
            cp.wait()

        @pl.when(meta_ref[M_HAS_NEXT, step] != 0)
        def _():
            for cp in copies(meta_ref[M_NEXT, step], 1 - slot):
                cp.start()


def _gate_up_kernel(meta_ref, nsteps_ref, x_ref, w_hbm, bg_ref, bu_ref, act_ref, wbuf, sem_w, *, chunk, npair):
    s = pl.program_id(0)
    tm, ffh = act_ref.shape
    ff = w_hbm.shape[2] // 2
    live = s < nsteps_ref[0]
    slot = meta_ref[M_SLOT, s]

    def weight_copies(e, half):
        c0 = pl.multiple_of(half * ffh, ffh)
        return (pltpu.make_async_copy(w_hbm.at[e, :, pl.ds(c0, ffh)], wbuf.at[half, 0], sem_w.at[half]),
                pltpu.make_async_copy(w_hbm.at[e, :, pl.ds(pl.multiple_of(ff + c0, ffh), ffh)], wbuf.at[half, 1],
                                      sem_w.at[half]))

    _stream_group_weights(meta_ref, s, live, weight_copies)

    def rows_pass(nrows):
        chunks = []
        for p in range(npair):
            chunks += _unpack_pair(x_ref[pl.ds(p, nrows, stride=npair), :])
        x = jnp.concatenate(chunks, axis=-1).astype(BF16)
        for c0 in range(0, ffh, chunk):
            cols = slice(c0, c0 + chunk)
            gate = jnp.dot(x, wbuf[slot, 0, :, cols].astype(BF16), preferred_element_type=F32) + bg_ref[:, cols]
            up = jnp.dot(x, wbuf[slot, 1, :, cols].astype(BF16), preferred_element_type=F32) + bu_ref[:, cols]
            gate = jnp.minimum(gate, SWIGLU_LIMIT)
            up = jnp.clip(up, -SWIGLU_LIMIT, SWIGLU_LIMIT)
            act = (up + 1.0) * gate * jax.nn.sigmoid(gate * SWIGLU_ALPHA)
            act_ref[0:nrows, cols] = act.astype(BF16)

    for part in range(1, TILE_PARTS + 1):
        @pl.when(live & (meta_ref[M_PARTS, s] == part))
        def _(nrows=part * tm // TILE_PARTS):
            rows_pass(nrows)
            if nrows < tm:
                act_ref[nrows:, :] = jnp.zeros((tm - nrows, ffh), BF16)

    @pl.when(jnp.logical_not(live))
    def _():
        act_ref[...] = jnp.zeros(act_ref.shape, BF16)


def _expert_of(index, ends):
    n_exp = ends.shape[0]
    e = jnp.minimum(jnp.sum((index[:, None] >= ends[None, :]).astype(I32), axis=1), n_exp - 1)
    hot = e[:, None] == jnp.arange(n_exp, dtype=I32)[None, :]
    return e, lambda table: jnp.sum(jnp.where(hot, table[None, :], 0), axis=1)


def _tile_parts(tile, first, rows_in_group, tm):
    real = jnp.clip(rows_in_group - (tile - first) * tm, 1, tm)
    return (real + (tm // TILE_PARTS - 1)) // (tm // TILE_PARTS)


def _live_expert_walk(group_tiles):
    n_exp = group_tiles.shape[0]
    idx = jnp.arange(n_exp, dtype=I32)
    owns = (group_tiles > 0)[None, :]
    rank = jnp.sum((idx[None, :] < idx[:, None]) & owns, axis=1).astype(I32)
    nxt = jnp.min(jnp.where((idx[None, :] > idx[:, None]) & owns, idx[None, :], n_exp), axis=1).astype(I32)
    return rank, nxt


def _gate_up_schedule(n_used, first_tile, group_tiles, group_rows, n_tiles, tm):
    assert GATE_UP_SPLIT == 2
    n_exp = first_tile.shape[0]
    n_steps = 2 * n_used
    s = jnp.minimum(jnp.arange(2 * n_tiles, dtype=I32), jnp.maximum(n_steps - 1, 0))
    e, lookup = _expert_of(s, 2 * (first_tile + group_tiles))
    first, count = lookup(first_tile), lookup(group_tiles)
    local = s - 2 * first
    half = (local >= count).astype(I32)
    inner = local - half * count
    tile = first + (count - 1) - inner
    _, next_live = _live_expert_walk(group_tiles)
    nxt = lookup(next_live)
    rows = [None] * 8
    rows[M_TILE], rows[M_HALF], rows[M_EXP] = tile, half, e
    rows[M_PARTS] = _tile_parts(tile, first, lookup(group_rows), tm)
    rows[M_FIRST] = (inner == 0).astype(I32)
    rows[M_SLOT] = half
    rows[M_NEXT] = jnp.where(half == 0, e, jnp.minimum(nxt, n_exp - 1))
    rows[M_HAS_NEXT] = ((half == 0) | (nxt < n_exp)).astype(I32)
    return jnp.stack(rows), n_steps


def _gate_up(xs, w, b, n_used, first_tile, group_tiles, group_rows, tm):
    n_exp, d, ff2 = w.shape
    nch = d // PAIR
    rows = xs.shape[0] // nch
    ff = ff2 // 2
    n_tiles = rows // tm
    ns = GATE_UP_SPLIT
    ffh = ff // ns
    meta, n_steps = _gate_up_schedule(n_used, first_tile, group_tiles, group_rows, n_tiles, tm)

    def out_index(s, m, n):
        live = s < n[0]
        return jnp.where(live, m[M_TILE, s], s // ns), jnp.where(live, m[M_HALF, s], s % ns)

    b3 = b.reshape(n_exp, 1, ff2)
    grid_spec = pltpu.PrefetchScalarGridSpec(
        num_scalar_prefetch=2,
        grid=(ns * n_tiles,),
        in_specs=[pl.BlockSpec((tm * nch, LANES), lambda s, m, n: (m[M_TILE, s], 0)),
                  pl.BlockSpec(memory_space=pl.ANY),
                  pl.BlockSpec((None, 1, ffh), lambda s, m, n: (m[M_EXP, s], 0, m[M_HALF, s])),
                  pl.BlockSpec((None, 1, ffh), lambda s, m, n: (m[M_EXP, s], 0, ns + m[M_HALF, s]))],
        out_specs=pl.BlockSpec((tm, ffh), out_index),
        scratch_shapes=[pltpu.VMEM((2, 2, d, ffh), F32), pltpu.SemaphoreType.DMA((2,))],
    )
    return pl.pallas_call(
        functools.partial(_gate_up_kernel, chunk=_tile(ffh, 512), npair=nch),
        grid_spec=grid_spec,
        out_shape=jax.ShapeDtypeStruct((rows, ff), BF16),
        compiler_params=_params(1),
        name="gate_up",
    )(meta, n_steps, xs, w, b3, b3)


def _down_kernel(meta_ref, nused_ref, a_ref, w_hbm, b_ref, y_ref, wbuf, sem_w):
    i = pl.program_id(0)
    tm = a_ref.shape[0]
    d = w_hbm.shape[2]
    npair = d // PAIR
    chunk = _tile(d, 512)
    live = i < nused_ref[0]
    slot = meta_ref[M_SLOT, i]

    def weight_copies(e, slot_):
        return (pltpu.make_async_copy(w_hbm.at[e], wbuf.at[slot_], sem_w.at[slot_]),)

    _stream_group_weights(meta_ref, i, live, weight_copies)

    def rows_pass(nrows):
        a = a_ref[0:nrows, :]
        for c0 in range(0, d, chunk):
            y = (jnp.dot(a, wbuf[slot, :, c0:c0 + chunk].astype(BF16), preferred_element_type=F32)
                 + b_ref[:, c0:c0 + chunk])
            for q in range(chunk // PAIR):
                y_ref[pl.ds(c0 // PAIR + q, nrows, stride=npair), :] = _pack_pair(
                    y[:, 2 * q * LANES:(2 * q + 1) * LANES], y[:, (2 * q + 1) * LANES:(2 * q + 2) * LANES])

    for part in range(1, TILE_PARTS + 1):
        @pl.when(live & (meta_ref[M_PARTS, i] == part))
        def _(nrows=part * tm // TILE_PARTS):
            rows_pass(nrows)
            if nrows < tm:
                y_ref[nrows * npair:, :] = jnp.zeros(((tm - nrows) * npair, LANES), U32)

    @pl.when(jnp.logical_not(live))
    def _():
        y_ref[...] = jnp.zeros(y_ref.shape, U32)


def _down_schedule(n_used, first_tile, group_tiles, group_rows, n_tiles, tm):
    n_exp = first_tile.shape[0]
    steps = jnp.arange(n_tiles, dtype=I32)
    e, lookup = _expert_of(steps, first_tile + group_tiles)
    first = lookup(first_tile)
    inner = steps - first
    tile = jnp.where(steps < n_used, first + (lookup(group_tiles) - 1) - inner, steps)
    rank, next_live = _live_expert_walk(group_tiles)
    nxt = lookup(next_live)
    rows = [None] * 8
    rows[M_TILE], rows[M_HALF], rows[M_EXP] = tile, jnp.zeros_like(steps), e
    rows[M_PARTS] = _tile_parts(tile, first, lookup(group_rows), tm)
    rows[M_FIRST] = (inner == 0).astype(I32)
    rows[M_SLOT] = lookup(rank) % 2
    rows[M_NEXT] = jnp.minimum(nxt, n_exp - 1)
    rows[M_HAS_NEXT] = (nxt < n_exp).astype(I32)
    return jnp.stack(rows)


def _down(act, w, b, n_used, first_tile, group_tiles, group_rows, tm):
    rows, ff = act.shape
    n_exp, _, d = w.shape
    nch = d // PAIR
    n_tiles = rows // tm
    meta = _down_schedule(n_used, first_tile, group_tiles, group_rows, n_tiles, tm)
    grid_spec = pltpu.PrefetchScalarGridSpec(
        num_scalar_prefetch=2,
        grid=(n_tiles,),
        in_specs=[pl.BlockSpec((tm, ff), lambda i, m, nu: (m[M_TILE, jnp.minimum(i, nu[0] - 1)], 0)),
                  pl.BlockSpec(memory_space=pl.ANY),
                  pl.BlockSpec((None, 1, d), lambda i, m, nu: (m[M_EXP, i], 0, 0))],
        out_specs=pl.BlockSpec((tm * nch, LANES), lambda i, m, nu: (m[M_TILE, i], 0)),
        scratch_shapes=[pltpu.VMEM((2, ff, d), F32), pltpu.SemaphoreType.DMA((2,))],
    )
    return pl.pallas_call(
        _down_kernel,
        grid_spec=grid_spec,
        out_shape=jax.ShapeDtypeStruct(_rows_shape(rows, d), U32),
        compiler_params=_params(1),
        name="down",
    )(meta, n_used, act, w, b.reshape(n_exp, 1, d))


def _combine_kernel(pos_hbm, y_hbm, x1_ref, mod_ref, g_ref, o_ref, pos_smem0, pos_smem1, ybuf, sem_idx, sem_row):
    i = pl.program_id(0)
    n = pl.num_programs(0)
    tmc, d = x1_ref.shape
    nch = d // PAIR
    slot = i % 2
    pos_smem = (pos_smem0, pos_smem1)

    def idx_copy(step, s):
        return pltpu.make_async_copy(pos_hbm.at[step], pos_smem[s], sem_idx.at[s])

    def per_slot(s_dyn, fn):
        for s in range(2):
            pl.when(s_dyn == s)(functools.partial(fn, s))

    def start_gathers(s):
        def issue(tok, carry):
            dst0 = pl.multiple_of(tok * nch, nch)
            for k in range(TOP_K):
                src0 = pl.multiple_of(pos_smem[s][tok * TOP_K + k], nch)
                pltpu.make_async_copy(y_hbm.at[pl.ds(src0, nch)], ybuf.at[s, k, pl.ds(dst0, nch)],
                                      sem_row.at[s]).start(priority=k % 2)
            return carry

        lax.fori_loop(0, tmc, issue, 0, unroll=4)

    @pl.when(i == 0)
    def _():
        idx_copy(0, 0).start()
        idx_copy(0, 0).wait()
        start_gathers(0)

        @pl.when(n > 1)
        def _():
            idx_copy(1, 1).start()

    def fetch_ahead(s):
        idx_copy(i + 1, s).wait()
        start_gathers(s)

    @pl.when(i + 1 < n)
    def _():
        per_slot(1 - slot, fetch_ahead)

    @pl.when(i + 2 < n)
    def _():
        per_slot(slot, lambda s: idx_copy(i + 2, s).start())

    for k in range(TOP_K):
        pltpu.make_async_copy(y_hbm.at[pl.ds(0, tmc * nch)], ybuf.at[slot, k], sem_row.at[slot]).wait()

    g = g_ref[...]
    gate2 = mod_ref[...][5:6]
    for p in range(nch):
        lo_sum = hi_sum = None
        for k in range(TOP_K):
            lo, hi = _unpack_pair(ybuf[slot, k, pl.ds(p, tmc, stride=nch), :])
            gk = g[:, k:k + 1]
            lo_sum = lo * gk if lo_sum is None else lo_sum + lo * gk
            hi_sum = hi * gk if hi_sum is None else hi_sum + hi * gk
        for half, moe in enumerate((lo_sum, hi_sum)):
            cols = slice((2 * p + half) * LANES, (2 * p + half + 1) * LANES)
            o_ref[:, cols] = x1_ref[:, cols] + gate2[:, cols] * moe


def _combine(y, pos_tiles, x1, mod3, gates_t, seq):
    t, d = x1.shape
    ntt, per_tile = pos_tiles.shape
    tmc = per_tile // TOP_K
    per_b = seq // tmc
    return pl.pallas_call(
        _combine_kernel,
        grid=(ntt,),
        in_specs=[pl.BlockSpec(memory_space=pl.ANY),
                  pl.BlockSpec(memory_space=pl.ANY),
                  pl.BlockSpec((tmc, d), lambda i: (i, 0)),
                  pl.BlockSpec((None, 6, d), lambda i: (i // per_b, 0, 0)),
                  pl.BlockSpec((tmc, TOP_K), lambda i: (i, 0))],
        out_specs=pl.BlockSpec((tmc, d), lambda i: (i, 0)),
        out_shape=jax.ShapeDtypeStruct((t, d), F32),
        scratch_shapes=[pltpu.SMEM((per_tile,), I32), pltpu.SMEM((per_tile,), I32),
                        pltpu.VMEM((2, TOP_K, tmc * (d // PAIR), LANES), U32),
                        pltpu.SemaphoreType.DMA((2,)), pltpu.SemaphoreType.DMA((2,))],
        compiler_params=_params(1),
        name="combine",
    )(pos_tiles, y, x1, mod3, gates_t)


def _moe(h2, logits_t, x1, mod3, w_gu, b_gu, w_dn, b_dn, seq):
    t, d = x1.shape
    n_exp = logits_t.shape[0]
    tm = _tile(t * TOP_K, 512)
    n_tiles = (t * TOP_K) // tm + n_exp
    gates, pos, emeta = _route(logits_t, tm)
    tok_tile = _tile(seq, 256)
    pos_tiles = (pos.T * (d // PAIR)).reshape(t // tok_tile, tok_tile * TOP_K)
    n_used = emeta[3, :1]
    group_start, group_rows = emeta[0, :n_exp], emeta[1, :n_exp]
    first_tile, group_tiles = group_start // tm, emeta[2, :n_exp] // tm
    xs = _dispatch(h2, pos_tiles, group_start, group_rows, n_used, tm, n_tiles, d // PAIR)
    act = _gate_up(xs, w_gu, b_gu, n_used, first_tile, group_tiles, group_rows, tm)
    y = _down(act, w_dn, b_dn, n_used, first_tile, group_tiles, group_rows, tm)
    return _combine(y, pos_tiles, x1, mod3, gates.T, seq)


def kernel(x, c, w_ada, b_ada, norm1_g, w_in, nat_q_g, nat_k_g, nat_rpb, diff_q_g, diff_k_g, diff_lambda,
           diff_sub_g, rel_bias_table, w_out, norm2_g, router_w, router_b, w_gate_up, b_gate_up, w_down, b_down):
    bsz, seq, d = x.shape
    t = bsz * seq
    ns = (d // 2) // HEAD_DIM
    n_exp = router_w.shape[-1]
    scale = HEAD_DIM ** -0.5
    t5_bias = _t5_bias_by_offset(rel_bias_table, seq)
    ones = jnp.ones((HEAD_DIM,), F32)
    xf = x.reshape(t, d)
    for l in range(w_ada.shape[0]):
        lambda_init = 0.8 - 0.6 * math.exp(-0.3 * l)
        mod3 = _adaln(c, w_ada[l], b_ada[l]).reshape(bsz, 6, d)
        gains = jnp.stack([nat_q_g[l] * scale, nat_k_g[l], ones, diff_q_g[l] * (scale * LOG2_E), diff_k_g[l],
                           ones, ones, ones])
        qkv = _inproj(xf, mod3, norm1_g[l], w_in[l], gains, seq)
        nat = _nat_attention(qkv, nat_rpb[l], bsz, seq, ns)
        dif = _diff_attention(qkv, t5_bias, diff_lambda[l], diff_sub_g[l], lambda_init, bsz, seq, ns)
        rw_pad = jnp.pad(router_w[l], ((0, 0), (0, LANES - n_exp)))
        x1, h2, logits_t = _outproj(nat, dif, xf, mod3, w_out[l], norm2_g[l], rw_pad, router_b[l], seq)
        xf = _moe(h2, logits_t, x1, mod3, w_gate_up[l], b_gate_up[l], w_down[l], b_down[l], seq)
    return xf.reshape(bsz, seq, d)
```

```python
import functools
import math

import jax
import jax.numpy as jnp
from jax import lax
from jax.experimental import pallas as pl
from jax.experimental.pallas import tpu as pltpu

HEAD_DIM = 128
GRID_W = 64
NAT_KR_MAX = 8
NAT_KC = 16
N_BUCKETS = 32
MAX_DISTANCE = 128
TOP_K = 4
SWIGLU_LIMIT = 7.0
SWIGLU_ALPHA = 1.702
RMS_EPS = 1e-6
NEG_INF = -1e30
LOG2_E = 1.4426950408889634

LANES = 128
PAIR = 2 * LANES
V7X_VMEM_LIMIT_BYTES = 56 * 1024 * 1024

F32 = jnp.float32
BF16 = jnp.bfloat16
I32 = jnp.int32
U32 = jnp.uint32


def _tile(n, pref):
    t = min(n, pref)
    assert n % t == 0, (n, pref)
    return t


def _params(n_axes):
    return pltpu.CompilerParams(dimension_semantics=("arbitrary",) * n_axes,
                                vmem_limit_bytes=V7X_VMEM_LIMIT_BYTES)


def _adaln_kernel(c_ref, w_ref, b_ref, o_ref):
    c = c_ref[...]
    a = (c * jax.nn.sigmoid(c)).astype(BF16)
    o_ref[...] = jnp.dot(a, w_ref[...].astype(BF16), preferred_element_type=F32) + b_ref[...]


def _adaln(c, w, b):
    bsz, d = c.shape
    n = w.shape[1]
    tn = _tile(n, 1024)
    return pl.pallas_call(
        _adaln_kernel,
        grid=(n // tn,),
        in_specs=[pl.BlockSpec((bsz, d), lambda j: (0, 0)),
                  pl.BlockSpec((d, tn), lambda j: (0, j)),
                  pl.BlockSpec((1, tn), lambda j: (0, j))],
        out_specs=pl.BlockSpec((bsz, tn), lambda j: (0, j)),
        out_shape=jax.ShapeDtypeStruct((bsz, n), F32),
        compiler_params=_params(1),
        name="adaln",
    )(c, w, b.reshape(1, n))


def _inproj_kernel(x_ref, mod_ref, g_ref, w_ref, gain_ref, o_ref, h_scr):
    j = pl.program_id(1)

    @pl.when(j == 0)
    def _():
        x = x_ref[...]
        ms = jnp.mean(x * x, axis=-1, keepdims=True)
        y = x * lax.rsqrt(ms + RMS_EPS) * g_ref[...]
        m = mod_ref[...]
        h_scr[...] = (y * (1.0 + m[1:2]) + m[0:1]).astype(BF16)

    heads = o_ref.shape[0]
    per = min(heads, PAIR // HEAD_DIM)
    is_value = (j == 2) | (j == 5)
    g = gain_ref[pl.ds(j, 1), :]
    h = h_scr[...]
    for c0 in range(0, heads, per):
        y = jnp.dot(h, w_ref[:, c0 * HEAD_DIM:(c0 + per) * HEAD_DIM].astype(BF16), preferred_element_type=F32)
        for hh in range(per):
            yh = y[:, hh * HEAD_DIM:(hh + 1) * HEAD_DIM]
            ms = jnp.mean(yh * yh, axis=-1, keepdims=True)
            factor = jnp.where(is_value, 1.0, lax.rsqrt(ms + RMS_EPS))
            o_ref[c0 + hh] = (yh * factor * g).astype(BF16)


def _inproj(xf, mod3, g1, w, gains, seq):
    t, d = xf.shape
    seg = d // 2
    ns = seg // HEAD_DIM
    tm = _tile(seq, 1024)
    per_b = seq // tm
    return pl.pallas_call(
        _inproj_kernel,
        grid=(t // tm, 6),
        in_specs=[pl.BlockSpec((tm, d), lambda i, j: (i, 0)),
                  pl.BlockSpec((None, 6, d), lambda i, j: (i // per_b, 0, 0)),
                  pl.BlockSpec((1, d), lambda i, j: (0, 0)),
                  pl.BlockSpec((d, seg), lambda i, j: (0, j)),
                  pl.BlockSpec((8, HEAD_DIM), lambda i, j: (0, 0))],
        out_specs=pl.BlockSpec((ns, tm, HEAD_DIM), lambda i, j: (j, i, 0)),
        out_shape=jax.ShapeDtypeStruct((6 * ns, t, HEAD_DIM), BF16),
        scratch_shapes=[pltpu.VMEM((tm, d), BF16)],
        compiler_params=_params(2),
        name="inproj",
    )(xf, mod3, g1.reshape(1, d), w, gains)


def _nat_kernel(q_ref, k_ref, v_ref, w_ref, o_ref, b_ref, *, rows, kr, group):
    band = kr * GRID_W

    @pl.when(pl.program_id(1) == 0)
    def _():
        qc = lax.broadcasted_iota(I32, (GRID_W, LANES), 0)
        lane = lax.broadcasted_iota(I32, (GRID_W, LANES), 1)
        kc = lane % GRID_W
        col_start = jnp.clip(qc - NAT_KC // 2, 0, GRID_W - NAT_KC)
        visible = (kc >= col_start) & (kc < col_start + NAT_KC)
        for d in range(kr):
            for j in range(0, kr, 2):
                def tile(jj, shift):
                    ro = jj - d + (NAT_KR_MAX - 1)
                    row = jnp.broadcast_to(w_ref[ro:ro + 1, :], (GRID_W, LANES))
                    return pltpu.roll(row, shift, 1, stride=1, stride_axis=0)
                both = jnp.where(lane < GRID_W, tile(j, 0), tile(j + 1, GRID_W))
                b_ref[d, :, j * GRID_W:(j + 2) * GRID_W] = jnp.where(visible, both, NEG_INF)

    def body(g, carry):
        k0s, scores = [], []
        for i in range(group):
            r = g * group + i
            rs = jnp.clip(r - kr // 2, 0, rows - kr)
            k0 = pl.multiple_of(rs * GRID_W, GRID_W)
            q = q_ref[pl.ds(pl.multiple_of(r * GRID_W, GRID_W), GRID_W), :]
            s = lax.dot_general(q, k_ref[pl.ds(k0, band), :], (((1,), (1,)), ((), ())),
                                preferred_element_type=F32)
            k0s.append(k0)
            scores.append(s + b_ref[r - rs])
        s = jnp.concatenate(scores, axis=0)
        p = jnp.exp(s - jnp.max(s, axis=-1, keepdims=True))
        den = jnp.sum(p, axis=-1, keepdims=True)
        pb = p.astype(BF16)
        outs = [jnp.dot(pb[i * GRID_W:(i + 1) * GRID_W], v_ref[pl.ds(k0s[i], band), :],
                        preferred_element_type=F32) for i in range(group)]
        o = jnp.concatenate(outs, axis=0) / den
        o_ref[pl.ds(pl.multiple_of(g * (group * GRID_W), group * GRID_W), group * GRID_W), :] = o.astype(BF16)
        return carry

    lax.fori_loop(0, rows // group, body, 0)


def _nat_bias_by_offset(rpb):
    lane = jnp.arange(LANES)
    off = jnp.where(lane < LANES // 2, lane, lane - LANES)
    return jnp.take(rpb.astype(F32), jnp.clip(off, -(NAT_KC - 1), NAT_KC - 1) + (NAT_KC - 1), axis=-1)


def _nat_attention(qkv, rpb, bsz, seq, ns):
    t = qkv.shape[1]
    rows = seq // GRID_W
    kr = min(NAT_KR_MAX, rows)
    assert kr % 2 == 0 and 2 * GRID_W == LANES
    blk = (None, seq, HEAD_DIM)
    n_off = 2 * NAT_KR_MAX - 1
    return pl.pallas_call(
        functools.partial(_nat_kernel, rows=rows, kr=kr, group=_tile(rows, 32)),
        grid=(ns, bsz),
        in_specs=[pl.BlockSpec(blk, lambda h, b: (h, b, 0)),
                  pl.BlockSpec(blk, lambda h, b: (ns + h, b, 0)),
                  pl.BlockSpec(blk, lambda h, b: (2 * ns + h, b, 0)),
                  pl.BlockSpec((None, n_off, LANES), lambda h, b: (h, 0, 0))],
        out_specs=pl.BlockSpec(blk, lambda h, b: (h, b, 0)),
        out_shape=jax.ShapeDtypeStruct((ns, t, HEAD_DIM), BF16),
        scratch_shapes=[pltpu.VMEM((kr, GRID_W, kr * GRID_W), F32)],
        compiler_params=_params(2),
        name="nat_attn",
    )(qkv, qkv, qkv, _nat_bias_by_offset(rpb))


def _t5_bucket(rel):
    nb = N_BUCKETS // 2
    max_exact = nb // 2
    ret = jnp.where(rel > 0, nb, 0)
    n = jnp.abs(rel)
    nf = jnp.maximum(n, 1).astype(F32)
    large = max_exact + (jnp.log(nf / max_exact) / math.log(MAX_DISTANCE / max_exact)
                         * (nb - max_exact)).astype(I32)
    large = jnp.minimum(large, nb - 1)
    return ret + jnp.where(n < max_exact, n, large)


def _t5_bias_by_offset(rel_table, seq):
    rel = jnp.arange(2 * seq, dtype=I32) - seq
    hot = (_t5_bucket(rel)[None, :] == jnp.arange(N_BUCKETS, dtype=I32)[:, None]).astype(F32)
    u = jnp.dot(rel_table.astype(F32).T, hot, precision=lax.Precision.HIGHEST) * LOG2_E
    return u.reshape(u.shape[0], 1, 2 * seq)


def _diff_kernel(q_ref, k_ref, v_ref, u_ref, lam_ref, sg_ref, o_ref, strip_scr, *, seq, tq, nq, n_split,
                 lambda_init):
    qi = pl.program_id(2)

    @pl.when((pl.program_id(1) == 0) & (qi == 0))
    def _():
        strip_scr[...] = pltpu.roll(jnp.broadcast_to(u_ref[...], strip_scr.shape), 0, 1, stride=1, stride_axis=0)

    bias = strip_scr[:, pl.ds(pl.multiple_of((nq - qi) * tq, tq), seq)]
    lam = lam_ref[...]
    lam_full = (jnp.exp(jnp.sum(lam[0:1] * lam[1:2], axis=-1, keepdims=True))
                - jnp.exp(jnp.sum(lam[2:3] * lam[3:4], axis=-1, keepdims=True)) + lambda_init)

    v = jnp.concatenate([v_ref[0], v_ref[1]], axis=-1)
    half = tq // n_split

    def scores(p, rows):
        s = lax.dot_general(q_ref[p, rows, :], k_ref[p], (((1,), (1,)), ((), ())), preferred_element_type=F32)
        return s + bias[rows]

    def attend(s1, s2):
        e1 = jnp.exp2(s1 - jnp.max(s1, axis=-1, keepdims=True))
        e2 = jnp.exp2(s2 - jnp.max(s2, axis=-1, keepdims=True))
        r1 = 1.0 / jnp.sum(e1, axis=-1, keepdims=True)
        r2 = lam_full / jnp.sum(e2, axis=-1, keepdims=True)
        o = jnp.dot((e1 * r1 - e2 * r2).astype(BF16), v, preferred_element_type=F32)
        ms = jnp.mean(o * o, axis=-1, keepdims=True)
        return o * lax.rsqrt(ms + RMS_EPS) * sg_ref[...] * (1.0 - lambda_init)

    blocks = [slice(j * half, (j + 1) * half) for j in range(n_split)]
    all_scores = [(scores(0, rows), scores(1, rows)) for rows in blocks]
    for rows, (s1, s2) in zip(blocks, all_scores):
        o = attend(s1, s2)
        o_ref[0, rows, :] = o[:, :HEAD_DIM].astype(BF16)
        o_ref[1, rows, :] = o[:, HEAD_DIM:].astype(BF16)


def _diff_attention(qkv, bias_by_offset, lam, sub_g, lambda_init, bsz, seq, ns):
    t = qkv.shape[1]
    hd = ns // 2
    tq = _tile(seq, 512)
    nq = seq // tq
    qb, kb, vb = 3 * ns // 2, 4 * ns // 2, 5 * ns // 2
    return pl.pallas_call(
        functools.partial(_diff_kernel, seq=seq, tq=tq, nq=nq, n_split=tq // 128, lambda_init=lambda_init),
        grid=(hd, bsz, nq),
        in_specs=[pl.BlockSpec((2, tq, HEAD_DIM), lambda h, b, i: (qb + h, b * nq + i, 0)),
                  pl.BlockSpec((2, seq, HEAD_DIM), lambda h, b, i: (kb + h, b, 0)),
                  pl.BlockSpec((2, seq, HEAD_DIM), lambda h, b, i: (vb + h, b, 0)),
                  pl.BlockSpec((None, 1, 2 * seq), lambda h, b, i: (h, 0, 0)),
                  pl.BlockSpec((4, HEAD_DIM), lambda h, b, i: (0, 0)),
                  pl.BlockSpec((1, 2 * HEAD_DIM), lambda h, b, i: (0, 0))],
        out_specs=pl.BlockSpec((2, tq, HEAD_DIM), lambda h, b, i: (h, b * nq + i, 0)),
        out_shape=jax.ShapeDtypeStruct((ns, t, HEAD_DIM), BF16),
        scratch_shapes=[pltpu.VMEM((tq, 2 * seq), F32)],
        compiler_params=_params(3),
        name="diff_attn",
    )(qkv, qkv, qkv, bias_by_offset, lam, sub_g.reshape(1, 2 * HEAD_DIM))


def _rows_shape(rows, d):
    return (rows * (d // PAIR), LANES)


def _pack_pair(lo, hi):
    lo_w = lax.bitcast_convert_type(lo.astype(BF16).astype(F32), U32) >> 16
    hi_w = lax.bitcast_convert_type(hi.astype(BF16).astype(F32), U32) & jnp.uint32(0xFFFF0000)
    return lo_w | hi_w


def _unpack_pair(w):
    return (lax.bitcast_convert_type(w << 16, F32),
            lax.bitcast_convert_type(w & jnp.uint32(0xFFFF0000), F32))


def _split_bf16(v):
    hi = v.astype(BF16)
    lo = (v - hi.astype(F32)).astype(BF16)
    return hi, lo


def _outproj_kernel(nat_ref, dif_ref, x_ref, mod_ref, w_ref, g_ref, rw_ref, rb_ref,
                    x1_ref, h2_ref, lt_ref):
    ns = nat_ref.shape[0]
    a = jnp.concatenate([nat_ref[hh] for hh in range(ns)] + [dif_ref[hh] for hh in range(ns)], axis=-1)
    mix = jnp.dot(a, w_ref[...].astype(BF16), preferred_element_type=F32)
    m = mod_ref[...]
    x1 = x_ref[...] + m[2:3] * mix
    x1_ref[...] = x1
    ms = jnp.mean(x1 * x1, axis=-1, keepdims=True)
    h2 = x1 * lax.rsqrt(ms + RMS_EPS) * g_ref[...] * (1.0 + m[4:5]) + m[3:4]
    npair = h2.shape[1] // PAIR
    for p in range(npair):
        h2_ref[pl.ds(p, h2.shape[0], stride=npair), :] = _pack_pair(
            h2[:, 2 * p * LANES:(2 * p + 1) * LANES], h2[:, (2 * p + 1) * LANES:(2 * p + 2) * LANES])
    h_hi, h_lo = _split_bf16(h2)
    w_hi, w_lo = _split_bf16(rw_ref[...])
    both = jnp.dot(h_hi, jnp.concatenate([w_hi, w_lo], axis=-1), preferred_element_type=F32)
    lg = both[:, :LANES] + both[:, LANES:] + jnp.dot(h_lo, w_hi, preferred_element_type=F32)
    n_exp = lt_ref.shape[0]
    lt_ref[...] = lg.T[:n_exp] + rb_ref[...]


def _outproj(nat, dif, xf, mod3, w, g2, rw_pad, rb, seq):
    t, d = xf.shape
    ns = nat.shape[0]
    n_exp = rb.shape[0]
    tm = _tile(seq, 512)
    per_b = seq // tm
    return pl.pallas_call(
        _outproj_kernel,
        grid=(t // tm,),
        in_specs=[pl.BlockSpec((ns, tm, HEAD_DIM), lambda i: (0, i, 0)),
                  pl.BlockSpec((ns, tm, HEAD_DIM), lambda i: (0, i, 0)),
                  pl.BlockSpec((tm, d), lambda i: (i, 0)),
                  pl.BlockSpec((None, 6, d), lambda i: (i // per_b, 0, 0)),
                  pl.BlockSpec((d, d), lambda i: (0, 0), pipeline_mode=pl.Buffered(1)),
                  pl.BlockSpec((1, d), lambda i: (0, 0)),
                  pl.BlockSpec((d, LANES), lambda i: (0, 0)),
                  pl.BlockSpec((n_exp, 1), lambda i: (0, 0))],
        out_specs=[pl.BlockSpec((tm, d), lambda i: (i, 0)),
                   pl.BlockSpec((tm * (d // PAIR), LANES), lambda i: (i, 0)),
                   pl.BlockSpec((n_exp, tm), lambda i: (0, i))],
        out_shape=[jax.ShapeDtypeStruct((t, d), F32),
                   jax.ShapeDtypeStruct(_rows_shape(t, d), U32),
                   jax.ShapeDtypeStruct((n_exp, t), F32)],
        compiler_params=_params(1),
        name="outproj",
    )(nat, dif, xf, mod3, w, g2.reshape(1, d), rw_pad, rb.reshape(n_exp, 1))


def _route_kernel(lt_ref, gate_ref, pos_ref, emeta_ref, idx_ref, rank_scr, *, tm, tb):
    n_exp, t = lt_ref.shape
    eidx = lax.broadcasted_iota(I32, (n_exp, tb), 0)
    tri = (lax.broadcasted_iota(I32, (tb, tb), 0) < lax.broadcasted_iota(I32, (tb, tb), 1)).astype(BF16)

    def pass1(jb, counts):
        off = pl.multiple_of(jb * tb, tb)
        l = lt_ref[:, pl.ds(off, tb)]
        vals, sels, hots = [], [], []
        for _ in range(TOP_K):
            m = jnp.max(l, axis=0, keepdims=True)
            sel = jnp.min(jnp.where(l == m, eidx, n_exp), axis=0, keepdims=True)
            hot = eidx == sel
            vals.append(m)
            sels.append(sel)
            hots.append(hot)
            l = jnp.where(hot, -jnp.inf, l)
        exps = [jnp.exp(v - vals[0]) for v in vals]
        den = exps[0]
        for e in exps[1:]:
            den = den + e
        member = hots[0].astype(F32)
        for hot in hots[1:]:
            member = member + hot.astype(F32)
        before = jnp.dot(member.astype(BF16), tri, preferred_element_type=F32) + counts
        for k in range(TOP_K):
            rank = jnp.sum(jnp.where(hots[k], before, 0.0), axis=0, keepdims=True)
            idx_ref[pl.ds(k, 1), pl.ds(off, tb)] = sels[k]
            gate_ref[pl.ds(k, 1), pl.ds(off, tb)] = exps[k] / den
            rank_scr[pl.ds(k, 1), pl.ds(off, tb)] = rank.astype(I32)
        return counts + jnp.sum(member, axis=1, keepdims=True)

    counts = lax.fori_loop(0, t // tb, pass1, jnp.zeros((n_exp, 1), F32))

    padded = jnp.ceil(counts * (1.0 / tm)) * tm
    er = lax.broadcasted_iota(I32, (n_exp, LANES), 0)
    ec = lax.broadcasted_iota(I32, (n_exp, LANES), 1)

    def to_lanes(col):
        return jnp.sum(jnp.where(er == ec, col, 0.0), axis=0, keepdims=True)

    start = jnp.sum(jnp.where(ec < er, to_lanes(padded), 0.0), axis=1, keepdims=True)

    def pass2(jb, carry):
        off = pl.multiple_of(jb * tb, tb)
        for k in range(TOP_K):
            sel = idx_ref[pl.ds(k, 1), pl.ds(off, tb)]
            st = jnp.sum(jnp.where(eidx == sel, start, 0.0), axis=0, keepdims=True)
            pos_ref[pl.ds(k, 1), pl.ds(off, tb)] = st.astype(I32) + rank_scr[pl.ds(k, 1), pl.ds(off, tb)]
        return carry

    lax.fori_loop(0, t // tb, pass2, 0)

    n_used = jnp.sum(padded, axis=0, keepdims=True) * (1.0 / tm)
    emeta_ref[...] = jnp.zeros(emeta_ref.shape, I32)
    emeta_ref[0:1, :] = to_lanes(start).astype(I32)
    emeta_ref[1:2, :] = to_lanes(counts).astype(I32)
    emeta_ref[2:3, :] = to_lanes(padded).astype(I32)
    emeta_ref[3:4, :] = jnp.broadcast_to(n_used, (1, LANES)).astype(I32)


def _route(logits_t, tm):
    n_exp, t = logits_t.shape
    tb = _tile(t, 512)
    full = lambda shape: pl.BlockSpec(shape, lambda: (0,) * len(shape))
    return pl.pallas_call(
        functools.partial(_route_kernel, tm=tm, tb=tb),
        in_specs=[full((n_exp, t))],
        out_specs=[full((TOP_K, t)), full((TOP_K, t)), full((8, LANES))],
        out_shape=[jax.ShapeDtypeStruct((TOP_K, t), F32),
                   jax.ShapeDtypeStruct((TOP_K, t), I32),
                   jax.ShapeDtypeStruct((8, LANES), I32)],
        scratch_shapes=[pltpu.VMEM((TOP_K, t), I32), pltpu.VMEM((TOP_K, t), I32)],
        compiler_params=pltpu.CompilerParams(vmem_limit_bytes=V7X_VMEM_LIMIT_BYTES),
        name="route",
    )(logits_t)


def _dispatch_kernel(estart_ref, ecnt_ref, nused_ref, pos_hbm, h_ref, xs_hbm, pos_smem, zero_scr,
                     sem_idx, sem_row, sem_zero, *, tm, nch):
    i = pl.program_id(0)
    tmd = h_ref.shape[0] // nch
    n_exp = estart_ref.shape[0]
    n_tiles = xs_hbm.shape[0] // (tm * nch)
    slot = i % 2

    def idx_copy(step, s):
        return pltpu.make_async_copy(pos_hbm.at[step], pos_smem.at[s], sem_idx.at[s])

    def zero_fills(do):
        def fill(e, carry):
            cnt = ecnt_ref[e]
            pad = (tm - cnt % tm) % tm
            base = estart_ref[e] + cnt
            size = tm // 2
            while size >= 1:
                off = pad & ~(2 * size - 1)

                @pl.when((pad & size) != 0)
                def _(size=size, off=off):
                    dst0 = pl.multiple_of((base + off) * nch, nch)
                    do(pltpu.make_async_copy(zero_scr.at[pl.ds(0, size * nch)],
                                             xs_hbm.at[pl.ds(dst0, size * nch)], sem_zero))
                size //= 2
            return carry

        lax.fori_loop(0, n_exp, fill, 0)

        def tail(j, carry):
            for part in range(2):
                dst0 = pl.multiple_of((j * tm + part * (tm // 2)) * nch, nch)
                do(pltpu.make_async_copy(zero_scr, xs_hbm.at[pl.ds(dst0, tm // 2 * nch)], sem_zero))
            return carry

        lax.fori_loop(nused_ref[0], n_tiles, tail, 0)

    @pl.when(i == 0)
    def _():
        idx_copy(0, 0).start()
        zero_scr[...] = jnp.zeros(zero_scr.shape, zero_scr.dtype)
        zero_fills(lambda cp: cp.start())

    @pl.when(i + 1 < pl.num_programs(0))
    def _():
        idx_copy(i + 1, 1 - slot).start()

    idx_copy(i, slot).wait()

    def start_rows(s):
        def issue(tok, carry):
            src0 = pl.multiple_of(tok * nch, nch)
            for k in range(TOP_K):
                dst0 = pl.multiple_of(pos_smem[s, tok * TOP_K + k], nch)
                pltpu.make_async_copy(h_ref.at[pl.ds(src0, nch)], xs_hbm.at[pl.ds(dst0, nch)],
                                      sem_row).start(priority=k % 2)
            return carry

        lax.fori_loop(0, tmd, issue, 0, unroll=4)

    for s in range(2):
        pl.when(slot == s)(functools.partial(start_rows, s))
    for _ in range(TOP_K):
        pltpu.make_async_copy(h_ref, xs_hbm.at[pl.ds(0, tmd * nch)], sem_row).wait()

    @pl.when(i == pl.num_programs(0) - 1)
    def _():
        zero_fills(lambda cp: cp.wait())


def _dispatch(h2, pos_tiles, estart, ecnt, n_used, tm, n_tiles, nch):
    ntt, per_tile = pos_tiles.shape
    tmd = per_tile // TOP_K
    grid_spec = pltpu.PrefetchScalarGridSpec(
        num_scalar_prefetch=3,
        grid=(ntt,),
        in_specs=[pl.BlockSpec(memory_space=pl.ANY),
                  pl.BlockSpec((tmd * nch, LANES), lambda i, es, ec, nu: (i, 0))],
        out_specs=pl.BlockSpec(memory_space=pl.ANY),
        scratch_shapes=[pltpu.SMEM((2, per_tile), I32),
                        pltpu.VMEM((tm // 2 * nch, LANES), h2.dtype),
                        pltpu.SemaphoreType.DMA((2,)), pltpu.SemaphoreType.DMA, pltpu.SemaphoreType.DMA],
    )
    return pl.pallas_call(
        functools.partial(_dispatch_kernel, tm=tm, nch=nch),
        grid_spec=grid_spec,
        out_shape=jax.ShapeDtypeStruct((n_tiles * tm * nch, LANES), h2.dtype),
        compiler_params=_params(1),
        name="dispatch",
    )(estart, ecnt, n_used, pos_tiles, h2)


GATE_UP_SPLIT = 2


(M_TILE, M_HALF, M_EXP, M_PARTS, M_FIRST, M_SLOT, M_NEXT, M_HAS_NEXT) = range(8)
TILE_PARTS = 4


def _stream_group_weights(meta_ref, step, live, copies):
    @pl.when(live & (step == 0))
    def _():
        for cp in copies(meta_ref[M_EXP, 0], meta_ref[M_SLOT, 0]):
            cp.start()

    @pl.when(live & (meta_ref[M_FIRST, step] != 0))
    def _():
        slot = meta_ref[M_SLOT, step]
        for cp in copies(meta_ref[M_EXP, step], slot):
            cp.wait()

        @pl.when(meta_ref[M_HAS_NEXT, step] != 0)
        def _():
            for cp in copies(meta_ref[M_NEXT, step], 1 - slot):
                cp.start()


def _gate_up_kernel(meta_ref, nsteps_ref, x_ref, w_hbm, bg_ref, bu_ref, act_ref, wbuf, sem_w, *, chunk, npair):
    s = pl.program_id(0)
    tm, ffh = act_ref.shape
    ff = w_hbm.shape[2] // 2
    live = s < nsteps_ref[0]
    slot = meta_ref[M_SLOT, s]

    def weight_copies(e, half):
        c0 = pl.multiple_of(half * ffh, ffh)
        return (pltpu.make_async_copy(w_hbm.at[e, :, pl.ds(c0, ffh)], wbuf.at[half, 0], sem_w.at[half]),
                pltpu.make_async_copy(w_hbm.at[e, :, pl.ds(pl.multiple_of(ff + c0, ffh), ffh)], wbuf.at[half, 1],
                                      sem_w.at[half]))

    _stream_group_weights(meta_ref, s, live, weight_copies)

    def rows_pass(nrows):
        chunks = []
        for p in range(npair):
            chunks += _unpack_pair(x_ref[pl.ds(p, nrows, stride=npair), :])
        x = jnp.concatenate(chunks, axis=-1).astype(BF16)
        for c0 in range(0, ffh, chunk):
            cols = slice(c0, c0 + chunk)
            gate = jnp.dot(x, wbuf[slot, 0, :, cols].astype(BF16), preferred_element_type=F32) + bg_ref[:, cols]
            up = jnp.dot(x, wbuf[slot, 1, :, cols].astype(BF16), preferred_element_type=F32) + bu_ref[:, cols]
            gate = jnp.minimum(gate, SWIGLU_LIMIT)
            up = jnp.clip(up, -SWIGLU_LIMIT, SWIGLU_LIMIT)
            act = (up + 1.0) * gate * jax.nn.sigmoid(gate * SWIGLU_ALPHA)
            act_ref[0:nrows, cols] = act.astype(BF16)

    for part in range(1, TILE_PARTS + 1):
        @pl.when(live & (meta_ref[M_PARTS, s] == part))
        def _(nrows=part * tm // TILE_PARTS):
            rows_pass(nrows)
            if nrows < tm:
                act_ref[nrows:, :] = jnp.zeros((tm - nrows, ffh), BF16)

    @pl.when(jnp.logical_not(live))
    def _():
        act_ref[...] = jnp.zeros(act_ref.shape, BF16)


def _expert_of(index, ends):
    n_exp = ends.shape[0]
    e = jnp.minimum(jnp.sum((index[:, None] >= ends[None, :]).astype(I32), axis=1), n_exp - 1)
    hot = e[:, None] == jnp.arange(n_exp, dtype=I32)[None, :]
    return e, lambda table: jnp.sum(jnp.where(hot, table[None, :], 0), axis=1)


def _tile_parts(tile, first, rows_in_group, tm):
    real = jnp.clip(rows_in_group - (tile - first) * tm, 1, tm)
    return (real + (tm // TILE_PARTS - 1)) // (tm // TILE_PARTS)


def _live_expert_walk(group_tiles):
    n_exp = group_tiles.shape[0]
    idx = jnp.arange(n_exp, dtype=I32)
    owns = (group_tiles > 0)[None, :]
    rank = jnp.sum((idx[None, :] < idx[:, None]) & owns, axis=1).astype(I32)
    nxt = jnp.min(jnp.where((idx[None, :] > idx[:, None]) & owns, idx[None, :], n_exp), axis=1).astype(I32)
    return rank, nxt


def _gate_up_schedule(n_used, first_tile, group_tiles, group_rows, n_tiles, tm):
    assert GATE_UP_SPLIT == 2
    n_exp = first_tile.shape[0]
    n_steps = 2 * n_used
    s = jnp.minimum(jnp.arange(2 * n_tiles, dtype=I32), jnp.maximum(n_steps - 1, 0))
    e, lookup = _expert_of(s, 2 * (first_tile + group_tiles))
    first, count = lookup(first_tile), lookup(group_tiles)
    local = s - 2 * first
    half = (local >= count).astype(I32)
    inner = local - half * count
    tile = first + (count - 1) - inner
    _, next_live = _live_expert_walk(group_tiles)
    nxt = lookup(next_live)
    rows = [None] * 8
    rows[M_TILE], rows[M_HALF], rows[M_EXP] = tile, half, e
    rows[M_PARTS] = _tile_parts(tile, first, lookup(group_rows), tm)
    rows[M_FIRST] = (inner == 0).astype(I32)
    rows[M_SLOT] = half
    rows[M_NEXT] = jnp.where(half == 0, e, jnp.minimum(nxt, n_exp - 1))
    rows[M_HAS_NEXT] = ((half == 0) | (nxt < n_exp)).astype(I32)
    return jnp.stack(rows), n_steps


def _gate_up(xs, w, b, n_used, first_tile, group_tiles, group_rows, tm):
    n_exp, d, ff2 = w.shape
    nch = d // PAIR
    rows = xs.shape[0] // nch
    ff = ff2 // 2
    n_tiles = rows // tm
    ns = GATE_UP_SPLIT
    ffh = ff // ns
    meta, n_steps = _gate_up_schedule(n_used, first_tile, group_tiles, group_rows, n_tiles, tm)

    def out_index(s, m, n):
        live = s < n[0]
        return jnp.where(live, m[M_TILE, s], s // ns), jnp.where(live, m[M_HALF, s], s % ns)

    b3 = b.reshape(n_exp, 1, ff2)
    grid_spec = pltpu.PrefetchScalarGridSpec(
        num_scalar_prefetch=2,
        grid=(ns * n_tiles,),
        in_specs=[pl.BlockSpec((tm * nch, LANES), lambda s, m, n: (m[M_TILE, s], 0)),
                  pl.BlockSpec(memory_space=pl.ANY),
                  pl.BlockSpec((None, 1, ffh), lambda s, m, n: (m[M_EXP, s], 0, m[M_HALF, s])),
                  pl.BlockSpec((None, 1, ffh), lambda s, m, n: (m[M_EXP, s], 0, ns + m[M_HALF, s]))],
        out_specs=pl.BlockSpec((tm, ffh), out_index),
        scratch_shapes=[pltpu.VMEM((2, 2, d, ffh), F32), pltpu.SemaphoreType.DMA((2,))],
    )
    return pl.pallas_call(
        functools.partial(_gate_up_kernel, chunk=_tile(ffh, 512), npair=nch),
        grid_spec=grid_spec,
        out_shape=jax.ShapeDtypeStruct((rows, ff), BF16),
        compiler_params=_params(1),
        name="gate_up",
    )(meta, n_steps, xs, w, b3, b3)


def _down_kernel(meta_ref, nused_ref, a_ref, w_hbm, b_ref, y_ref, wbuf, sem_w):
    i = pl.program_id(0)
    tm = a_ref.shape[0]
    d = w_hbm.shape[2]
    npair = d // PAIR
    chunk = _tile(d, 512)
    live = i < nused_ref[0]
    slot = meta_ref[M_SLOT, i]

    def weight_copies(e, slot_):
        return (pltpu.make_async_copy(w_hbm.at[e], wbuf.at[slot_], sem_w.at[slot_]),)

    _stream_group_weights(meta_ref, i, live, weight_copies)

    def rows_pass(nrows):
        a = a_ref[0:nrows, :]
        for c0 in range(0, d, chunk):
            y = (jnp.dot(a, wbuf[slot, :, c0:c0 + chunk].astype(BF16), preferred_element_type=F32)
                 + b_ref[:, c0:c0 + chunk])
            for q in range(chunk // PAIR):
                y_ref[pl.ds(c0 // PAIR + q, nrows, stride=npair), :] = _pack_pair(
                    y[:, 2 * q * LANES:(2 * q + 1) * LANES], y[:, (2 * q + 1) * LANES:(2 * q + 2) * LANES])

    for part in range(1, TILE_PARTS + 1):
        @pl.when(live & (meta_ref[M_PARTS, i] == part))
        def _(nrows=part * tm // TILE_PARTS):
            rows_pass(nrows)
            if nrows < tm:
                y_ref[nrows * npair:, :] = jnp.zeros(((tm - nrows) * npair, LANES), U32)

    @pl.when(jnp.logical_not(live))
    def _():
        y_ref[...] = jnp.zeros(y_ref.shape, U32)


def _down_schedule(n_used, first_tile, group_tiles, group_rows, n_tiles, tm):
    n_exp = first_tile.shape[0]
    steps = jnp.arange(n_tiles, dtype=I32)
    e, lookup = _expert_of(steps, first_tile + group_tiles)
    first = lookup(first_tile)
    inner = steps - first
    tile = jnp.where(steps < n_used, first + (lookup(group_tiles) - 1) - inner, steps)
    rank, next_live = _live_expert_walk(group_tiles)
    nxt = lookup(next_live)
    rows = [None] * 8
    rows[M_TILE], rows[M_HALF], rows[M_EXP] = tile, jnp.zeros_like(steps), e
    rows[M_PARTS] = _tile_parts(tile, first, lookup(group_rows), tm)
    rows[M_FIRST] = (inner == 0).astype(I32)
    rows[M_SLOT] = lookup(rank) % 2
    rows[M_NEXT] = jnp.minimum(nxt, n_exp - 1)
    rows[M_HAS_NEXT] = (nxt < n_exp).astype(I32)
    return jnp.stack(rows)


def _down(act, w, b, n_used, first_tile, group_tiles, group_rows, tm):
    rows, ff = act.shape
    n_exp, _, d = w.shape
    nch = d // PAIR
    n_tiles = rows // tm
    meta = _down_schedule(n_used, first_tile, group_tiles, group_rows, n_tiles, tm)
    grid_spec = pltpu.PrefetchScalarGridSpec(
        num_scalar_prefetch=2,
        grid=(n_tiles,),
        in_specs=[pl.BlockSpec((tm, ff), lambda i, m, nu: (m[M_TILE, jnp.minimum(i, nu[0] - 1)], 0)),
                  pl.BlockSpec(memory_space=pl.ANY),
                  pl.BlockSpec((None, 1, d), lambda i, m, nu: (m[M_EXP, i], 0, 0))],
        out_specs=pl.BlockSpec((tm * nch, LANES), lambda i, m, nu: (m[M_TILE, i], 0)),
        scratch_shapes=[pltpu.VMEM((2, ff, d), F32), pltpu.SemaphoreType.DMA((2,))],
    )
    return pl.pallas_call(
        _down_kernel,
        grid_spec=grid_spec,
        out_shape=jax.ShapeDtypeStruct(_rows_shape(rows, d), U32),
        compiler_params=_params(1),
        name="down",
    )(meta, n_used, act, w, b.reshape(n_exp, 1, d))


def _combine_kernel(pos_hbm, y_hbm, x1_ref, mod_ref, g_ref, o_ref, pos_smem0, pos_smem1, ybuf, sem_idx, sem_row):
    i = pl.program_id(0)
    n = pl.num_programs(0)
    tmc, d = x1_ref.shape
    nch = d // PAIR
    slot = i % 2
    pos_smem = (pos_smem0, pos_smem1)

    def idx_copy(step, s):
        return pltpu.make_async_copy(pos_hbm.at[step], pos_smem[s], sem_idx.at[s])

    def per_slot(s_dyn, fn):
        for s in range(2):
            pl.when(s_dyn == s)(functools.partial(fn, s))

    def start_gathers(s):
        def issue(tok, carry):
            dst0 = pl.multiple_of(tok * nch, nch)
            for k in range(TOP_K):
                src0 = pl.multiple_of(pos_smem[s][tok * TOP_K + k], nch)
                pltpu.make_async_copy(y_hbm.at[pl.ds(src0, nch)], ybuf.at[s, k, pl.ds(dst0, nch)],
                                      sem_row.at[s]).start(priority=k % 2)
            return carry

        lax.fori_loop(0, tmc, issue, 0, unroll=4)

    @pl.when(i == 0)
    def _():
        idx_copy(0, 0).start()
        idx_copy(0, 0).wait()
        start_gathers(0)

        @pl.when(n > 1)
        def _():
            idx_copy(1, 1).start()

    def fetch_ahead(s):
        idx_copy(i + 1, s).wait()
        start_gathers(s)

    @pl.when(i + 1 < n)
    def _():
        per_slot(1 - slot, fetch_ahead)

    @pl.when(i + 2 < n)
    def _():
        per_slot(slot, lambda s: idx_copy(i + 2, s).start())

    for k in range(TOP_K):
        pltpu.make_async_copy(y_hbm.at[pl.ds(0, tmc * nch)], ybuf.at[slot, k], sem_row.at[slot]).wait()

    g = g_ref[...]
    gate2 = mod_ref[...][5:6]
    for p in range(nch):
        lo_sum = hi_sum = None
        for k in range(TOP_K):
            lo, hi = _unpack_pair(ybuf[slot, k, pl.ds(p, tmc, stride=nch), :])
            gk = g[:, k:k + 1]
            lo_sum = lo * gk if lo_sum is None else lo_sum + lo * gk
            hi_sum = hi * gk if hi_sum is None else hi_sum + hi * gk
        for half, moe in enumerate((lo_sum, hi_sum)):
            cols = slice((2 * p + half) * LANES, (2 * p + half + 1) * LANES)
            o_ref[:, cols] = x1_ref[:, cols] + gate2[:, cols] * moe


def _combine(y, pos_tiles, x1, mod3, gates_t, seq):
    t, d = x1.shape
    ntt, per_tile = pos_tiles.shape
    tmc = per_tile // TOP_K
    per_b = seq // tmc
    return pl.pallas_call(
        _combine_kernel,
        grid=(ntt,),
        in_specs=[pl.BlockSpec(memory_space=pl.ANY),
                  pl.BlockSpec(memory_space=pl.ANY),
                  pl.BlockSpec((tmc, d), lambda i: (i, 0)),
                  pl.BlockSpec((None, 6, d), lambda i: (i // per_b, 0, 0)),
                  pl.BlockSpec((tmc, TOP_K), lambda i: (i, 0))],
        out_specs=pl.BlockSpec((tmc, d), lambda i: (i, 0)),
        out_shape=jax.ShapeDtypeStruct((t, d), F32),
        scratch_shapes=[pltpu.SMEM((per_tile,), I32), pltpu.SMEM((per_tile,), I32),
                        pltpu.VMEM((2, TOP_K, tmc * (d // PAIR), LANES), U32),
                        pltpu.SemaphoreType.DMA((2,)), pltpu.SemaphoreType.DMA((2,))],
        compiler_params=_params(1),
        name="combine",
    )(pos_tiles, y, x1, mod3, gates_t)


def _moe(h2, logits_t, x1, mod3, w_gu, b_gu, w_dn, b_dn, seq):
    t, d = x1.shape
    n_exp = logits_t.shape[0]
    tm = _tile(t * TOP_K, 512)
    n_tiles = (t * TOP_K) // tm + n_exp
    gates, pos, emeta = _route(logits_t, tm)
    tok_tile = _tile(seq, 256)
    pos_tiles = (pos.T * (d // PAIR)).reshape(t // tok_tile, tok_tile * TOP_K)
    n_used = emeta[3, :1]
    group_start, group_rows = emeta[0, :n_exp], emeta[1, :n_exp]
    first_tile, group_tiles = group_start // tm, emeta[2, :n_exp] // tm
    xs = _dispatch(h2, pos_tiles, group_start, group_rows, n_used, tm, n_tiles, d // PAIR)
    act = _gate_up(xs, w_gu, b_gu, n_used, first_tile, group_tiles, group_rows, tm)
    y = _down(act, w_dn, b_dn, n_used, first_tile, group_tiles, group_rows, tm)
    return _combine(y, pos_tiles, x1, mod3, gates.T, seq)


def kernel(x, c, w_ada, b_ada, norm1_g, w_in, nat_q_g, nat_k_g, nat_rpb, diff_q_g, diff_k_g, diff_lambda,
           diff_sub_g, rel_bias_table, w_out, norm2_g, router_w, router_b, w_gate_up, b_gate_up, w_down, b_down):
    bsz, seq, d = x.shape
    t = bsz * seq
    ns = (d // 2) // HEAD_DIM
    n_exp = router_w.shape[-1]
    scale = HEAD_DIM ** -0.5
    t5_bias = _t5_bias_by_offset(rel_bias_table, seq)
    ones = jnp.ones((HEAD_DIM,), F32)
    xf = x.reshape(t, d)
    for l in range(w_ada.shape[0]):
        lambda_init = 0.8 - 0.6 * math.exp(-0.3 * l)
        mod3 = _adaln(c, w_ada[l], b_ada[l]).reshape(bsz, 6, d)
        gains = jnp.stack([nat_q_g[l] * scale, nat_k_g[l], ones, diff_q_g[l] * (scale * LOG2_E), diff_k_g[l],
                           ones, ones, ones])
        qkv = _inproj(xf, mod3, norm1_g[l], w_in[l], gains, seq)
        nat = _nat_attention(qkv, nat_rpb[l], bsz, seq, ns)
        dif = _diff_attention(qkv, t5_bias, diff_lambda[l], diff_sub_g[l], lambda_init, bsz, seq, ns)
        rw_pad = jnp.pad(router_w[l], ((0, 0), (0, LANES - n_exp)))
        x1, h2, logits_t = _outproj(nat, dif, xf, mod3, w_out[l], norm2_g[l], rw_pad, router_b[l], seq)
        xf = _moe(h2, logits_t, x1, mod3, w_gate_up[l], b_gate_up[l], w_down[l], b_down[l], seq)
    return xf.reshape(bsz, seq, d)
```

```python
import functools
import math

import jax
import jax.numpy as jnp
from jax import lax
from jax.experimental import pallas as pl
from jax.experimental.pallas import tpu as pltpu

HEAD_DIM = 128
GRID_W = 64
NAT_KR_MAX = 8
NAT_KC = 16
N_BUCKETS = 32
MAX_DISTANCE = 128
TOP_K = 4
SWIGLU_LIMIT = 7.0
SWIGLU_ALPHA = 1.702
RMS_EPS = 1e-6
NEG_INF = -1e30
LOG2_E = 1.4426950408889634

LANES = 128
PAIR = 2 * LANES
V7X_VMEM_LIMIT_BYTES = 56 * 1024 * 1024

F32 = jnp.float32
BF16 = jnp.bfloat16
I32 = jnp.int32
U32 = jnp.uint32


def _tile(n, pref):
    t = min(n, pref)
    assert n % t == 0, (n, pref)
    return t


def _params(n_axes):
    return pltpu.CompilerParams(dimension_semantics=("arbitrary",) * n_axes,
                                vmem_limit_bytes=V7X_VMEM_LIMIT_BYTES)


def _adaln_kernel(c_ref, w_ref, b_ref, o_ref):
    c = c_ref[...]
    a = (c * jax.nn.sigmoid(c)).astype(BF16)
    o_ref[...] = jnp.dot(a, w_ref[...].astype(BF16), preferred_element_type=F32) + b_ref[...]


def _adaln(c, w, b):
    bsz, d = c.shape
    n = w.shape[1]
    tn = _tile(n, 1024)
    return pl.pallas_call(
        _adaln_kernel,
        grid=(n // tn,),
        in_specs=[pl.BlockSpec((bsz, d), lambda j: (0, 0)),
                  pl.BlockSpec((d, tn), lambda j: (0, j)),
                  pl.BlockSpec((1, tn), lambda j: (0, j))],
        out_specs=pl.BlockSpec((bsz, tn), lambda j: (0, j)),
        out_shape=jax.ShapeDtypeStruct((bsz, n), F32),
        compiler_params=_params(1),
        name="adaln",
    )(c, w, b.reshape(1, n))


def _inproj_kernel(x_ref, mod_ref, g_ref, w_ref, gain_ref, o_ref, h_scr):
    j = pl.program_id(1)

    @pl.when(j == 0)
    def _():
        x = x_ref[...]
        ms = jnp.mean(x * x, axis=-1, keepdims=True)
        y = x * lax.rsqrt(ms + RMS_EPS) * g_ref[...]
        m = mod_ref[...]
        h_scr[...] = (y * (1.0 + m[1:2]) + m[0:1]).astype(BF16)

    heads = o_ref.shape[0]
    per = min(heads, PAIR // HEAD_DIM)
    is_value = (j == 2) | (j == 5)
    g = gain_ref[pl.ds(j, 1), :]
    h = h_scr[...]
    for c0 in range(0, heads, per):
        y = jnp.dot(h, w_ref[:, c0 * HEAD_DIM:(c0 + per) * HEAD_DIM].astype(BF16), preferred_element_type=F32)
        for hh in range(per):
            yh = y[:, hh * HEAD_DIM:(hh + 1) * HEAD_DIM]
            ms = jnp.mean(yh * yh, axis=-1, keepdims=True)
            factor = jnp.where(is_value, 1.0, lax.rsqrt(ms + RMS_EPS))
            o_ref[c0 + hh] = (yh * factor * g).astype(BF16)


def _inproj(xf, mod3, g1, w, gains, seq):
    t, d = xf.shape
    seg = d // 2
    ns = seg // HEAD_DIM
    tm = _tile(seq, 1024)
    per_b = seq // tm
    return pl.pallas_call(
        _inproj_kernel,
        grid=(t // tm, 6),
        in_specs=[pl.BlockSpec((tm, d), lambda i, j: (i, 0)),
                  pl.BlockSpec((None, 6, d), lambda i, j: (i // per_b, 0, 0)),
                  pl.BlockSpec((1, d), lambda i, j: (0, 0)),
                  pl.BlockSpec((d, seg), lambda i, j: (0, j)),
                  pl.BlockSpec((8, HEAD_DIM), lambda i, j: (0, 0))],
        out_specs=pl.BlockSpec((ns, tm, HEAD_DIM), lambda i, j: (j, i, 0)),
        out_shape=jax.ShapeDtypeStruct((6 * ns, t, HEAD_DIM), BF16),
        scratch_shapes=[pltpu.VMEM((tm, d), BF16)],
        compiler_params=_params(2),
        name="inproj",
    )(xf, mod3, g1.reshape(1, d), w, gains)


def _nat_kernel(q_ref, k_ref, v_ref, w_ref, o_ref, b_ref, *, rows, kr, group):
    band = kr * GRID_W

    @pl.when(pl.program_id(1) == 0)
    def _():
        qc = lax.broadcasted_iota(I32, (GRID_W, LANES), 0)
        lane = lax.broadcasted_iota(I32, (GRID_W, LANES), 1)
        kc = lane % GRID_W
        col_start = jnp.clip(qc - NAT_KC // 2, 0, GRID_W - NAT_KC)
        visible = (kc >= col_start) & (kc < col_start + NAT_KC)
        for d in range(kr):
            for j in range(0, kr, 2):
                def tile(jj, shift):
                    ro = jj - d + (NAT_KR_MAX - 1)
                    row = jnp.broadcast_to(w_ref[ro:ro + 1, :], (GRID_W, LANES))
                    return pltpu.roll(row, shift, 1, stride=1, stride_axis=0)
                both = jnp.where(lane < GRID_W, tile(j, 0), tile(j + 1, GRID_W))
                b_ref[d, :, j * GRID_W:(j + 2) * GRID_W] = jnp.where(visible, both, NEG_INF)

    def body(g, carry):
        k0s, scores = [], []
        for i in range(group):
            r = g * group + i
            rs = jnp.clip(r - kr // 2, 0, rows - kr)
            k0 = pl.multiple_of(rs * GRID_W, GRID_W)
            q = q_ref[pl.ds(pl.multiple_of(r * GRID_W, GRID_W), GRID_W), :]
            s = lax.dot_general(q, k_ref[pl.ds(k0, band), :], (((1,), (1,)), ((), ())),
                                preferred_element_type=F32)
            k0s.append(k0)
            scores.append(s + b_ref[r - rs])
        s = jnp.concatenate(scores, axis=0)
        p = jnp.exp(s - jnp.max(s, axis=-1, keepdims=True))
        den = jnp.sum(p, axis=-1, keepdims=True)
        pb = p.astype(BF16)
        outs = [jnp.dot(pb[i * GRID_W:(i + 1) * GRID_W], v_ref[pl.ds(k0s[i], band), :],
                        preferred_element_type=F32) for i in range(group)]
        o = jnp.concatenate(outs, axis=0) / den
        o_ref[pl.ds(pl.multiple_of(g * (group * GRID_W), group * GRID_W), group * GRID_W), :] = o.astype(BF16)
        return carry

    lax.fori_loop(0, rows // group, body, 0)


def _nat_bias_by_offset(rpb):
    lane = jnp.arange(LANES)
    off = jnp.where(lane < LANES // 2, lane, lane - LANES)
    return jnp.take(rpb.astype(F32), jnp.clip(off, -(NAT_KC - 1), NAT_KC - 1) + (NAT_KC - 1), axis=-1)


def _nat_attention(qkv, rpb, bsz, seq, ns):
    t = qkv.shape[1]
    rows = seq // GRID_W
    kr = min(NAT_KR_MAX, rows)
    assert kr % 2 == 0 and 2 * GRID_W == LANES
    blk = (None, seq, HEAD_DIM)
    n_off = 2 * NAT_KR_MAX - 1
    return pl.pallas_call(
        functools.partial(_nat_kernel, rows=rows, kr=kr, group=_tile(rows, 32)),
        grid=(ns, bsz),
        in_specs=[pl.BlockSpec(blk, lambda h, b: (h, b, 0)),
                  pl.BlockSpec(blk, lambda h, b: (ns + h, b, 0)),
                  pl.BlockSpec(blk, lambda h, b: (2 * ns + h, b, 0)),
                  pl.BlockSpec((None, n_off, LANES), lambda h, b: (h, 0, 0))],
        out_specs=pl.BlockSpec(blk, lambda h, b: (h, b, 0)),
        out_shape=jax.ShapeDtypeStruct((ns, t, HEAD_DIM), BF16),
        scratch_shapes=[pltpu.VMEM((kr, GRID_W, kr * GRID_W), F32)],
        compiler_params=_params(2),
        name="nat_attn",
    )(qkv, qkv, qkv, _nat_bias_by_offset(rpb))


def _t5_bucket(rel):
    nb = N_BUCKETS // 2
    max_exact = nb // 2
    ret = jnp.where(rel > 0, nb, 0)
    n = jnp.abs(rel)
    nf = jnp.maximum(n, 1).astype(F32)
    large = max_exact + (jnp.log(nf / max_exact) / math.log(MAX_DISTANCE / max_exact)
                         * (nb - max_exact)).astype(I32)
    large = jnp.minimum(large, nb - 1)
    return ret + jnp.where(n < max_exact, n, large)


def _t5_bias_by_offset(rel_table, seq):
    rel = jnp.arange(2 * seq, dtype=I32) - seq
    hot = (_t5_bucket(rel)[None, :] == jnp.arange(N_BUCKETS, dtype=I32)[:, None]).astype(F32)
    u = jnp.dot(rel_table.astype(F32).T, hot, precision=lax.Precision.HIGHEST) * LOG2_E
    return u.reshape(u.shape[0], 1, 2 * seq)


def _diff_kernel(q_ref, k_ref, v_ref, u_ref, lam_ref, sg_ref, o_ref, strip_scr, *, seq, tq, nq, n_split,
                 lambda_init):
    qi = pl.program_id(2)

    @pl.when((pl.program_id(1) == 0) & (qi == 0))
    def _():
        strip_scr[...] = pltpu.roll(jnp.broadcast_to(u_ref[...], strip_scr.shape), 0, 1, stride=1, stride_axis=0)

    bias = strip_scr[:, pl.ds(pl.multiple_of((nq - qi) * tq, tq), seq)]
    lam = lam_ref[...]
    lam_full = (jnp.exp(jnp.sum(lam[0:1] * lam[1:2], axis=-1, keepdims=True))
                - jnp.exp(jnp.sum(lam[2:3] * lam[3:4], axis=-1, keepdims=True)) + lambda_init)

    v = jnp.concatenate([v_ref[0], v_ref[1]], axis=-1)
    half = tq // n_split

    def scores(p, rows):
        s = lax.dot_general(q_ref[p, rows, :], k_ref[p], (((1,), (1,)), ((), ())), preferred_element_type=F32)
        return s + bias[rows]

    def attend(s1, s2):
        e1 = jnp.exp2(s1 - jnp.max(s1, axis=-1, keepdims=True))
        e2 = jnp.exp2(s2 - jnp.max(s2, axis=-1, keepdims=True))
        r1 = 1.0 / jnp.sum(e1, axis=-1, keepdims=True)
        r2 = lam_full / jnp.sum(e2, axis=-1, keepdims=True)
        o = jnp.dot((e1 * r1 - e2 * r2).astype(BF16), v, preferred_element_type=F32)
        ms = jnp.mean(o * o, axis=-1, keepdims=True)
        return o * lax.rsqrt(ms + RMS_EPS) * sg_ref[...] * (1.0 - lambda_init)

    blocks = [slice(j * half, (j + 1) * half) for j in range(n_split)]
    all_scores = [(scores(0, rows), scores(1, rows)) for rows in blocks]
    for rows, (s1, s2) in zip(blocks, all_scores):
        o = attend(s1, s2)
        o_ref[0, rows, :] = o[:, :HEAD_DIM].astype(BF16)
        o_ref[1, rows, :] = o[:, HEAD_DIM:].astype(BF16)


def _diff_attention(qkv, bias_by_offset, lam, sub_g, lambda_init, bsz, seq, ns):
    t = qkv.shape[1]
    hd = ns // 2
    tq = _tile(seq, 512)
    nq = seq // tq
    qb, kb, vb = 3 * ns // 2, 4 * ns // 2, 5 * ns // 2
    return pl.pallas_call(
        functools.partial(_diff_kernel, seq=seq, tq=tq, nq=nq, n_split=tq // 128, lambda_init=lambda_init),
        grid=(hd, bsz, nq),
        in_specs=[pl.BlockSpec((2, tq, HEAD_DIM), lambda h, b, i: (qb + h, b * nq + i, 0)),
                  pl.BlockSpec((2, seq, HEAD_DIM), lambda h, b, i: (kb + h, b, 0)),
                  pl.BlockSpec((2, seq, HEAD_DIM), lambda h, b, i: (vb + h, b, 0)),
                  pl.BlockSpec((None, 1, 2 * seq), lambda h, b, i: (h, 0, 0)),
                  pl.BlockSpec((4, HEAD_DIM), lambda h, b, i: (0, 0)),
                  pl.BlockSpec((1, 2 * HEAD_DIM), lambda h, b, i: (0, 0))],
        out_specs=pl.BlockSpec((2, tq, HEAD_DIM), lambda h, b, i: (h, b * nq + i, 0)),
        out_shape=jax.ShapeDtypeStruct((ns, t, HEAD_DIM), BF16),
        scratch_shapes=[pltpu.VMEM((tq, 2 * seq), F32)],
        compiler_params=_params(3),
        name="diff_attn",
    )(qkv, qkv, qkv, bias_by_offset, lam, sub_g.reshape(1, 2 * HEAD_DIM))


def _rows_shape(rows, d):
    return (rows * (d // PAIR), LANES)


def _pack_pair(lo, hi):
    lo_w = lax.bitcast_convert_type(lo.astype(BF16).astype(F32), U32) >> 16
    hi_w = lax.bitcast_convert_type(hi.astype(BF16).astype(F32), U32) & jnp.uint32(0xFFFF0000)
    return lo_w | hi_w


def _unpack_pair(w):
    return (lax.bitcast_convert_type(w << 16, F32),
            lax.bitcast_convert_type(w & jnp.uint32(0xFFFF0000), F32))


def _split_bf16(v):
    hi = v.astype(BF16)
    lo = (v - hi.astype(F32)).astype(BF16)
    return hi, lo


def _outproj_kernel(nat_ref, dif_ref, x_ref, mod_ref, w_ref, g_ref, rw_ref, rb_ref,
                    x1_ref, h2_ref, lt_ref):
    ns = nat_ref.shape[0]
    a = jnp.concatenate([nat_ref[hh] for hh in range(ns)] + [dif_ref[hh] for hh in range(ns)], axis=-1)
    mix = jnp.dot(a, w_ref[...].astype(BF16), preferred_element_type=F32)
    m = mod_ref[...]
    x1 = x_ref[...] + m[2:3] * mix
    x1_ref[...] = x1
    ms = jnp.mean(x1 * x1, axis=-1, keepdims=True)
    h2 = x1 * lax.rsqrt(ms + RMS_EPS) * g_ref[...] * (1.0 + m[4:5]) + m[3:4]
    npair = h2.shape[1] // PAIR
    for p in range(npair):
        h2_ref[pl.ds(p, h2.shape[0], stride=npair), :] = _pack_pair(
            h2[:, 2 * p * LANES:(2 * p + 1) * LANES], h2[:, (2 * p + 1) * LANES:(2 * p + 2) * LANES])
    h_hi, h_lo = _split_bf16(h2)
    w_hi, w_lo = _split_bf16(rw_ref[...])
    both = jnp.dot(h_hi, jnp.concatenate([w_hi, w_lo], axis=-1), preferred_element_type=F32)
    lg = both[:, :LANES] + both[:, LANES:] + jnp.dot(h_lo, w_hi, preferred_element_type=F32)
    n_exp = lt_ref.shape[0]
    lt_ref[...] = lg.T[:n_exp] + rb_ref[...]


def _outproj(nat, dif, xf, mod3, w, g2, rw_pad, rb, seq):
    t, d = xf.shape
    ns = nat.shape[0]
    n_exp = rb.shape[0]
    tm = _tile(seq, 512)
    per_b = seq // tm
    return pl.pallas_call(
        _outproj_kernel,
        grid=(t // tm,),
        in_specs=[pl.BlockSpec((ns, tm, HEAD_DIM), lambda i: (0, i, 0)),
                  pl.BlockSpec((ns, tm, HEAD_DIM), lambda i: (0, i, 0)),
                  pl.BlockSpec((tm, d), lambda i: (i, 0)),
                  pl.BlockSpec((None, 6, d), lambda i: (i // per_b, 0, 0)),
                  pl.BlockSpec((d, d), lambda i: (0, 0), pipeline_mode=pl.Buffered(1)),
                  pl.BlockSpec((1, d), lambda i: (0, 0)),
                  pl.BlockSpec((d, LANES), lambda i: (0, 0)),
                  pl.BlockSpec((n_exp, 1), lambda i: (0, 0))],
        out_specs=[pl.BlockSpec((tm, d), lambda i: (i, 0)),
                   pl.BlockSpec((tm * (d // PAIR), LANES), lambda i: (i, 0)),
                   pl.BlockSpec((n_exp, tm), lambda i: (0, i))],
        out_shape=[jax.ShapeDtypeStruct((t, d), F32),
                   jax.ShapeDtypeStruct(_rows_shape(t, d), U32),
                   jax.ShapeDtypeStruct((n_exp, t), F32)],
        compiler_params=_params(1),
        name="outproj",
    )(nat, dif, xf, mod3, w, g2.reshape(1, d), rw_pad, rb.reshape(n_exp, 1))


def _route_kernel(lt_ref, gate_ref, pos_ref, emeta_ref, idx_ref, rank_scr, *, tm, tb):
    n_exp, t = lt_ref.shape
    eidx = lax.broadcasted_iota(I32, (n_exp, tb), 0)
    tri = (lax.broadcasted_iota(I32, (tb, tb), 0) < lax.broadcasted_iota(I32, (tb, tb), 1)).astype(BF16)

    def pass1(jb, counts):
        off = pl.multiple_of(jb * tb, tb)
        l = lt_ref[:, pl.ds(off, tb)]
        vals, sels, hots = [], [], []
        for _ in range(TOP_K):
            m = jnp.max(l, axis=0, keepdims=True)
            sel = jnp.min(jnp.where(l == m, eidx, n_exp), axis=0, keepdims=True)
            hot = eidx == sel
            vals.append(m)
            sels.append(sel)
            hots.append(hot)
            l = jnp.where(hot, -jnp.inf, l)
        exps = [jnp.exp(v - vals[0]) for v in vals]
        den = exps[0]
        for e in exps[1:]:
            den = den + e
        member = hots[0].astype(F32)
        for hot in hots[1:]:
            member = member + hot.astype(F32)
        before = jnp.dot(member.astype(BF16), tri, preferred_element_type=F32) + counts
        for k in range(TOP_K):
            rank = jnp.sum(jnp.where(hots[k], before, 0.0), axis=0, keepdims=True)
            idx_ref[pl.ds(k, 1), pl.ds(off, tb)] = sels[k]
            gate_ref[pl.ds(k, 1), pl.ds(off, tb)] = exps[k] / den
            rank_scr[pl.ds(k, 1), pl.ds(off, tb)] = rank.astype(I32)
        return counts + jnp.sum(member, axis=1, keepdims=True)

    counts = lax.fori_loop(0, t // tb, pass1, jnp.zeros((n_exp, 1), F32))

    padded = jnp.ceil(counts * (1.0 / tm)) * tm
    er = lax.broadcasted_iota(I32, (n_exp, LANES), 0)
    ec = lax.broadcasted_iota(I32, (n_exp, LANES), 1)

    def to_lanes(col):
        return jnp.sum(jnp.where(er == ec, col, 0.0), axis=0, keepdims=True)

    start = jnp.sum(jnp.where(ec < er, to_lanes(padded), 0.0), axis=1, keepdims=True)

    def pass2(jb, carry):
        off = pl.multiple_of(jb * tb, tb)
        for k in range(TOP_K):
            sel = idx_ref[pl.ds(k, 1), pl.ds(off, tb)]
            st = jnp.sum(jnp.where(eidx == sel, start, 0.0), axis=0, keepdims=True)
            pos_ref[pl.ds(k, 1), pl.ds(off, tb)] = st.astype(I32) + rank_scr[pl.ds(k, 1), pl.ds(off, tb)]
        return carry

    lax.fori_loop(0, t // tb, pass2, 0)

    n_used = jnp.sum(padded, axis=0, keepdims=True) * (1.0 / tm)
    emeta_ref[...] = jnp.zeros(emeta_ref.shape, I32)
    emeta_ref[0:1, :] = to_lanes(start).astype(I32)
    emeta_ref[1:2, :] = to_lanes(counts).astype(I32)
    emeta_ref[2:3, :] = to_lanes(padded).astype(I32)
    emeta_ref[3:4, :] = jnp.broadcast_to(n_used, (1, LANES)).astype(I32)


def _route(logits_t, tm):
    n_exp, t = logits_t.shape
    tb = _tile(t, 512)
    full = lambda shape: pl.BlockSpec(shape, lambda: (0,) * len(shape))
    return pl.pallas_call(
        functools.partial(_route_kernel, tm=tm, tb=tb),
        in_specs=[full((n_exp, t))],
        out_specs=[full((TOP_K, t)), full((TOP_K, t)), full((8, LANES))],
        out_shape=[jax.ShapeDtypeStruct((TOP_K, t), F32),
                   jax.ShapeDtypeStruct((TOP_K, t), I32),
                   jax.ShapeDtypeStruct((8, LANES), I32)],
        scratch_shapes=[pltpu.VMEM((TOP_K, t), I32), pltpu.VMEM((TOP_K, t), I32)],
        compiler_params=pltpu.CompilerParams(vmem_limit_bytes=V7X_VMEM_LIMIT_BYTES),
        name="route",
    )(logits_t)


def _dispatch_kernel(estart_ref, ecnt_ref, nused_ref, pos_hbm, h_ref, xs_hbm, pos_smem, zero_scr,
                     sem_idx, sem_row, sem_zero, *, tm, nch):
    i = pl.program_id(0)
    tmd = h_ref.shape[0] // nch
    n_exp = estart_ref.shape[0]
    n_tiles = xs_hbm.shape[0] // (tm * nch)
    slot = i % 2

    def idx_copy(step, s):
        return pltpu.make_async_copy(pos_hbm.at[step], pos_smem.at[s], sem_idx.at[s])

    def zero_fills(do):
        def fill(e, carry):
            cnt = ecnt_ref[e]
            pad = (tm - cnt % tm) % tm
            base = estart_ref[e] + cnt
            size = tm // 2
            while size >= 1:
                off = pad & ~(2 * size - 1)

                @pl.when((pad & size) != 0)
                def _(size=size, off=off):
                    dst0 = pl.multiple_of((base + off) * nch, nch)
                    do(pltpu.make_async_copy(zero_scr.at[pl.ds(0, size * nch)],
                                             xs_hbm.at[pl.ds(dst0, size * nch)], sem_zero))
                size //= 2
            return carry

        lax.fori_loop(0, n_exp, fill, 0)

        def tail(j, carry):
            for part in range(2):
                dst0 = pl.multiple_of((j * tm + part * (tm // 2)) * nch, nch)
                do(pltpu.make_async_copy(zero_scr, xs_hbm.at[pl.ds(dst0, tm // 2 * nch)], sem_zero))
            return carry

        lax.fori_loop(nused_ref[0], n_tiles, tail, 0)

    @pl.when(i == 0)
    def _():
        idx_copy(0, 0).start()
        zero_scr[...] = jnp.zeros(zero_scr.shape, zero_scr.dtype)
        zero_fills(lambda cp: cp.start())

    @pl.when(i + 1 < pl.num_programs(0))
    def _():
        idx_copy(i + 1, 1 - slot).start()

    idx_copy(i, slot).wait()

    def start_rows(s):
        def issue(tok, carry):
            src0 = pl.multiple_of(tok * nch, nch)
            for k in range(TOP_K):
                dst0 = pl.multiple_of(pos_smem[s, k * tmd + tok] * nch, nch)
                pltpu.make_async_copy(h_ref.at[pl.ds(src0, nch)], xs_hbm.at[pl.ds(dst0, nch)],
                                      sem_row).start(priority=k % 2)
            return carry

        lax.fori_loop(0, tmd, issue, 0, unroll=4)

    for s in range(2):
        pl.when(slot == s)(functools.partial(start_rows, s))
    for _ in range(TOP_K):
        pltpu.make_async_copy(h_ref, xs_hbm.at[pl.ds(0, tmd * nch)], sem_row).wait()

    @pl.when(i == pl.num_programs(0) - 1)
    def _():
        zero_fills(lambda cp: cp.wait())


def _dispatch(h2, pos_tiles, estart, ecnt, n_used, tm, n_tiles, nch):
    ntt, per_tile = pos_tiles.shape
    tmd = per_tile // TOP_K
    grid_spec = pltpu.PrefetchScalarGridSpec(
        num_scalar_prefetch=3,
        grid=(ntt,),
        in_specs=[pl.BlockSpec(memory_space=pl.ANY),
                  pl.BlockSpec((tmd * nch, LANES), lambda i, es, ec, nu: (i, 0))],
        out_specs=pl.BlockSpec(memory_space=pl.ANY),
        scratch_shapes=[pltpu.SMEM((2, per_tile), I32),
                        pltpu.VMEM((tm // 2 * nch, LANES), h2.dtype),
                        pltpu.SemaphoreType.DMA((2,)), pltpu.SemaphoreType.DMA, pltpu.SemaphoreType.DMA],
    )
    return pl.pallas_call(
        functools.partial(_dispatch_kernel, tm=tm, nch=nch),
        grid_spec=grid_spec,
        out_shape=jax.ShapeDtypeStruct((n_tiles * tm * nch, LANES), h2.dtype),
        compiler_params=_params(1),
        name="dispatch",
    )(estart, ecnt, n_used, pos_tiles, h2)


GATE_UP_SPLIT = 2


(M_TILE, M_HALF, M_EXP, M_PARTS, M_FIRST, M_SLOT, M_NEXT, M_HAS_NEXT) = range(8)
TILE_PARTS = 4


def _stream_group_weights(meta_ref, step, live, copies):
    @pl.when(live & (step == 0))
    def _():
        for cp in copies(meta_ref[M_EXP, 0], meta_ref[M_SLOT, 0]):
            cp.start()

    @pl.when(live & (meta_ref[M_FIRST, step] != 0))
    def _():
        slot = meta_ref[M_SLOT, step]
        for cp in copies(meta_ref[M_EXP, step], slot):
            cp.wait()

        @pl.when(meta_ref[M_HAS_NEXT, step] != 0)
        def _():
            for cp in copies(meta_ref[M_NEXT, step], 1 - slot):
                cp.start()


def _gate_up_kernel(meta_ref, nsteps_ref, x_ref, w_hbm, bg_ref, bu_ref, act_ref, wbuf, sem_w, *, chunk, npair):
    s = pl.program_id(0)
    tm, ffh = act_ref.shape
    ff = w_hbm.shape[2] // 2
    live = s < nsteps_ref[0]
    slot = meta_ref[M_SLOT, s]

    def weight_copies(e, half):
        c0 = pl.multiple_of(half * ffh, ffh)
        return (pltpu.make_async_copy(w_hbm.at[e, :, pl.ds(c0, ffh)], wbuf.at[half, 0], sem_w.at[half]),
                pltpu.make_async_copy(w_hbm.at[e, :, pl.ds(pl.multiple_of(ff + c0, ffh), ffh)], wbuf.at[half, 1],
                                      sem_w.at[half]))

    _stream_group_weights(meta_ref, s, live, weight_copies)

    def rows_pass(nrows):
        chunks = []
        for p in range(npair):
            chunks += _unpack_pair(x_ref[pl.ds(p, nrows, stride=npair), :])
        x = jnp.concatenate(chunks, axis=-1).astype(BF16)
        for c0 in range(0, ffh, chunk):
            cols = slice(c0, c0 + chunk)
            gate = jnp.dot(x, wbuf[slot, 0, :, cols].astype(BF16), preferred_element_type=F32) + bg_ref[:, cols]
            up = jnp.dot(x, wbuf[slot, 1, :, cols].astype(BF16), preferred_element_type=F32) + bu_ref[:, cols]
            gate = jnp.minimum(gate, SWIGLU_LIMIT)
            up = jnp.clip(up, -SWIGLU_LIMIT, SWIGLU_LIMIT)
            act = (up + 1.0) * gate * jax.nn.sigmoid(gate * SWIGLU_ALPHA)
            act_ref[0:nrows, cols] = act.astype(BF16)

    for part in range(1, TILE_PARTS + 1):
        @pl.when(live & (meta_ref[M_PARTS, s] == part))
        def _(nrows=part * tm // TILE_PARTS):
            rows_pass(nrows)
            if nrows < tm:
                act_ref[nrows:, :] = jnp.zeros((tm - nrows, ffh), BF16)

    @pl.when(jnp.logical_not(live))
    def _():
        act_ref[...] = jnp.zeros(act_ref.shape, BF16)


def _expert_of(index, ends):
    n_exp = ends.shape[0]
    e = jnp.minimum(jnp.sum((index[:, None] >= ends[None, :]).astype(I32), axis=1), n_exp - 1)
    hot = e[:, None] == jnp.arange(n_exp, dtype=I32)[None, :]
    return e, lambda table: jnp.sum(jnp.where(hot, table[None, :], 0), axis=1)


def _tile_parts(tile, first, rows_in_group, tm):
    real = jnp.clip(rows_in_group - (tile - first) * tm, 1, tm)
    return (real + (tm // TILE_PARTS - 1)) // (tm // TILE_PARTS)


def _live_expert_walk(group_tiles):
    n_exp = group_tiles.shape[0]
    idx = jnp.arange(n_exp, dtype=I32)
    owns = (group_tiles > 0)[None, :]
    rank = jnp.sum((idx[None, :] < idx[:, None]) & owns, axis=1).astype(I32)
    nxt = jnp.min(jnp.where((idx[None, :] > idx[:, None]) & owns, idx[None, :], n_exp), axis=1).astype(I32)
    return rank, nxt


def _gate_up_schedule(n_used, first_tile, group_tiles, group_rows, n_tiles, tm):
    assert GATE_UP_SPLIT == 2
    n_exp = first_tile.shape[0]
    n_steps = 2 * n_used
    s = jnp.minimum(jnp.arange(2 * n_tiles, dtype=I32), jnp.maximum(n_steps - 1, 0))
    e, lookup = _expert_of(s, 2 * (first_tile + group_tiles))
    first, count = lookup(first_tile), lookup(group_tiles)
    local = s - 2 * first
    half = (local >= count).astype(I32)
    inner = local - half * count
    tile = first + (count - 1) - inner
    _, next_live = _live_expert_walk(group_tiles)
    nxt = lookup(next_live)
    rows = [None] * 8
    rows[M_TILE], rows[M_HALF], rows[M_EXP] = tile, half, e
    rows[M_PARTS] = _tile_parts(tile, first, lookup(group_rows), tm)
    rows[M_FIRST] = (inner == 0).astype(I32)
    rows[M_SLOT] = half
    rows[M_NEXT] = jnp.where(half == 0, e, jnp.minimum(nxt, n_exp - 1))
    rows[M_HAS_NEXT] = ((half == 0) | (nxt < n_exp)).astype(I32)
    return jnp.stack(rows), n_steps


def _gate_up(xs, w, b, n_used, first_tile, group_tiles, group_rows, tm):
    n_exp, d, ff2 = w.shape
    nch = d // PAIR
    rows = xs.shape[0] // nch
    ff = ff2 // 2
    n_tiles = rows // tm
    ns = GATE_UP_SPLIT
    ffh = ff // ns
    meta, n_steps = _gate_up_schedule(n_used, first_tile, group_tiles, group_rows, n_tiles, tm)

    def out_index(s, m, n):
        live = s < n[0]
        return jnp.where(live, m[M_TILE, s], s // ns), jnp.where(live, m[M_HALF, s], s % ns)

    b3 = b.reshape(n_exp, 1, ff2)
    grid_spec = pltpu.PrefetchScalarGridSpec(
        num_scalar_prefetch=2,
        grid=(ns * n_tiles,),
        in_specs=[pl.BlockSpec((tm * nch, LANES), lambda s, m, n: (m[M_TILE, s], 0)),
                  pl.BlockSpec(memory_space=pl.ANY),
                  pl.BlockSpec((None, 1, ffh), lambda s, m, n: (m[M_EXP, s], 0, m[M_HALF, s])),
                  pl.BlockSpec((None, 1, ffh), lambda s, m, n: (m[M_EXP, s], 0, ns + m[M_HALF, s]))],
        out_specs=pl.BlockSpec((tm, ffh), out_index),
        scratch_shapes=[pltpu.VMEM((2, 2, d, ffh), F32), pltpu.SemaphoreType.DMA((2,))],
    )
    return pl.pallas_call(
        functools.partial(_gate_up_kernel, chunk=_tile(ffh, 512), npair=nch),
        grid_spec=grid_spec,
        out_shape=jax.ShapeDtypeStruct((rows, ff), BF16),
        compiler_params=_params(1),
        name="gate_up",
    )(meta, n_steps, xs, w, b3, b3)


def _down_kernel(meta_ref, nused_ref, a_ref, w_hbm, b_ref, y_ref, wbuf, sem_w):
    i = pl.program_id(0)
    tm = a_ref.shape[0]
    d = w_hbm.shape[2]
    npair = d // PAIR
    chunk = _tile(d, 512)
    live = i < nused_ref[0]
    slot = meta_ref[M_SLOT, i]

    def weight_copies(e, slot_):
        return (pltpu.make_async_copy(w_hbm.at[e], wbuf.at[slot_], sem_w.at[slot_]),)

    _stream_group_weights(meta_ref, i, live, weight_copies)

    def rows_pass(nrows):
        a = a_ref[0:nrows, :]
        for c0 in range(0, d, chunk):
            y = (jnp.dot(a, wbuf[slot, :, c0:c0 + chunk].astype(BF16), preferred_element_type=F32)
                 + b_ref[:, c0:c0 + chunk])
            for q in range(chunk // PAIR):
                y_ref[pl.ds(c0 // PAIR + q, nrows, stride=npair), :] = _pack_pair(
                    y[:, 2 * q * LANES:(2 * q + 1) * LANES], y[:, (2 * q + 1) * LANES:(2 * q + 2) * LANES])

    for part in range(1, TILE_PARTS + 1):
        @pl.when(live & (meta_ref[M_PARTS, i] == part))
        def _(nrows=part * tm // TILE_PARTS):
            rows_pass(nrows)
            if nrows < tm:
                y_ref[nrows * npair:, :] = jnp.zeros(((tm - nrows) * npair, LANES), U32)

    @pl.when(jnp.logical_not(live))
    def _():
        y_ref[...] = jnp.zeros(y_ref.shape, U32)


def _down_schedule(n_used, first_tile, group_tiles, group_rows, n_tiles, tm):
    n_exp = first_tile.shape[0]
    steps = jnp.arange(n_tiles, dtype=I32)
    e, lookup = _expert_of(steps, first_tile + group_tiles)
    first = lookup(first_tile)
    inner = steps - first
    tile = jnp.where(steps < n_used, first + (lookup(group_tiles) - 1) - inner, steps)
    rank, next_live = _live_expert_walk(group_tiles)
    nxt = lookup(next_live)
    rows = [None] * 8
    rows[M_TILE], rows[M_HALF], rows[M_EXP] = tile, jnp.zeros_like(steps), e
    rows[M_PARTS] = _tile_parts(tile, first, lookup(group_rows), tm)
    rows[M_FIRST] = (inner == 0).astype(I32)
    rows[M_SLOT] = lookup(rank) % 2
    rows[M_NEXT] = jnp.minimum(nxt, n_exp - 1)
    rows[M_HAS_NEXT] = (nxt < n_exp).astype(I32)
    return jnp.stack(rows)


def _down(act, w, b, n_used, first_tile, group_tiles, group_rows, tm):
    rows, ff = act.shape
    n_exp, _, d = w.shape
    nch = d // PAIR
    n_tiles = rows // tm
    meta = _down_schedule(n_used, first_tile, group_tiles, group_rows, n_tiles, tm)
    grid_spec = pltpu.PrefetchScalarGridSpec(
        num_scalar_prefetch=2,
        grid=(n_tiles,),
        in_specs=[pl.BlockSpec((tm, ff), lambda i, m, nu: (m[M_TILE, jnp.minimum(i, nu[0] - 1)], 0)),
                  pl.BlockSpec(memory_space=pl.ANY),
                  pl.BlockSpec((None, 1, d), lambda i, m, nu: (m[M_EXP, i], 0, 0))],
        out_specs=pl.BlockSpec((tm * nch, LANES), lambda i, m, nu: (m[M_TILE, i], 0)),
        scratch_shapes=[pltpu.VMEM((2, ff, d), F32), pltpu.SemaphoreType.DMA((2,))],
    )
    return pl.pallas_call(
        _down_kernel,
        grid_spec=grid_spec,
        out_shape=jax.ShapeDtypeStruct(_rows_shape(rows, d), U32),
        compiler_params=_params(1),
        name="down",
    )(meta, n_used, act, w, b.reshape(n_exp, 1, d))


def _combine_kernel(pos_hbm, y_hbm, x1_ref, mod_ref, g_ref, o_ref, pos_smem0, pos_smem1, ybuf, sem_idx, sem_row):
    i = pl.program_id(0)
    n = pl.num_programs(0)
    tmc, d = x1_ref.shape
    nch = d // PAIR
    slot = i % 2
    pos_smem = (pos_smem0, pos_smem1)

    def idx_copy(step, s):
        return pltpu.make_async_copy(pos_hbm.at[step], pos_smem[s], sem_idx.at[s])

    def per_slot(s_dyn, fn):
        for s in range(2):
            pl.when(s_dyn == s)(functools.partial(fn, s))

    def start_gathers(s):
        def issue(tok, carry):
            dst0 = pl.multiple_of(tok * nch, nch)
            for k in range(TOP_K):
                src0 = pl.multiple_of(pos_smem[s][k * tmc + tok] * nch, nch)
                pltpu.make_async_copy(y_hbm.at[pl.ds(src0, nch)], ybuf.at[s, k, pl.ds(dst0, nch)],
                                      sem_row.at[s]).start(priority=k % 2)
            return carry

        lax.fori_loop(0, tmc, issue, 0, unroll=4)

    @pl.when(i == 0)
    def _():
        idx_copy(0, 0).start()
        idx_copy(0, 0).wait()
        start_gathers(0)

        @pl.when(n > 1)
        def _():
            idx_copy(1, 1).start()

    def fetch_ahead(s):
        idx_copy(i + 1, s).wait()
        start_gathers(s)

    @pl.when(i + 1 < n)
    def _():
        per_slot(1 - slot, fetch_ahead)

    @pl.when(i + 2 < n)
    def _():
        per_slot(slot, lambda s: idx_copy(i + 2, s).start())

    for k in range(TOP_K):
        pltpu.make_async_copy(y_hbm.at[pl.ds(0, tmc * nch)], ybuf.at[slot, k], sem_row.at[slot]).wait()

    g = g_ref[...]
    gate2 = mod_ref[...][5:6]
    for p in range(nch):
        lo_sum = hi_sum = None
        for k in range(TOP_K):
            lo, hi = _unpack_pair(ybuf[slot, k, pl.ds(p, tmc, stride=nch), :])
            gk = g[:, k:k + 1]
            lo_sum = lo * gk if lo_sum is None else lo_sum + lo * gk
            hi_sum = hi * gk if hi_sum is None else hi_sum + hi * gk
        for half, moe in enumerate((lo_sum, hi_sum)):
            cols = slice((2 * p + half) * LANES, (2 * p + half + 1) * LANES)
            o_ref[:, cols] = x1_ref[:, cols] + gate2[:, cols] * moe


def _combine(y, pos_tiles, x1, mod3, gates_t, seq):
    t, d = x1.shape
    ntt, per_tile = pos_tiles.shape
    tmc = per_tile // TOP_K
    per_b = seq // tmc
    return pl.pallas_call(
        _combine_kernel,
        grid=(ntt,),
        in_specs=[pl.BlockSpec(memory_space=pl.ANY),
                  pl.BlockSpec(memory_space=pl.ANY),
                  pl.BlockSpec((tmc, d), lambda i: (i, 0)),
                  pl.BlockSpec((None, 6, d), lambda i: (i // per_b, 0, 0)),
                  pl.BlockSpec((tmc, TOP_K), lambda i: (i, 0))],
        out_specs=pl.BlockSpec((tmc, d), lambda i: (i, 0)),
        out_shape=jax.ShapeDtypeStruct((t, d), F32),
        scratch_shapes=[pltpu.SMEM((per_tile,), I32), pltpu.SMEM((per_tile,), I32),
                        pltpu.VMEM((2, TOP_K, tmc * (d // PAIR), LANES), U32),
                        pltpu.SemaphoreType.DMA((2,)), pltpu.SemaphoreType.DMA((2,))],
        compiler_params=_params(1),
        name="combine",
    )(pos_tiles, y, x1, mod3, gates_t)


def _moe(h2, logits_t, x1, mod3, w_gu, b_gu, w_dn, b_dn, seq):
    t, d = x1.shape
    n_exp = logits_t.shape[0]
    tm = _tile(t * TOP_K, 512)
    n_tiles = (t * TOP_K) // tm + n_exp
    gates, pos, emeta = _route(logits_t, tm)
    tok_tile = _tile(seq, 256)
    pos_tiles = (pos.reshape(TOP_K, t // tok_tile, tok_tile).transpose(1, 0, 2)
                 .reshape(t // tok_tile, TOP_K * tok_tile))
    n_used = emeta[3, :1]
    group_start, group_rows = emeta[0, :n_exp], emeta[1, :n_exp]
    first_tile, group_tiles = group_start // tm, emeta[2, :n_exp] // tm
    xs = _dispatch(h2, pos_tiles, group_start, group_rows, n_used, tm, n_tiles, d // PAIR)
    act = _gate_up(xs, w_gu, b_gu, n_used, first_tile, group_tiles, group_rows, tm)
    y = _down(act, w_dn, b_dn, n_used, first_tile, group_tiles, group_rows, tm)
    return _combine(y, pos_tiles, x1, mod3, gates.T, seq)


def kernel(x, c, w_ada, b_ada, norm1_g, w_in, nat_q_g, nat_k_g, nat_rpb, diff_q_g, diff_k_g, diff_lambda,
           diff_sub_g, rel_bias_table, w_out, norm2_g, router_w, router_b, w_gate_up, b_gate_up, w_down, b_down):
    bsz, seq, d = x.shape
    t = bsz * seq
    ns = (d // 2) // HEAD_DIM
    n_exp = router_w.shape[-1]
    scale = HEAD_DIM ** -0.5
    t5_bias = _t5_bias_by_offset(rel_bias_table, seq)
    ones = jnp.ones((HEAD_DIM,), F32)
    xf = x.reshape(t, d)
    for l in range(w_ada.shape[0]):
        lambda_init = 0.8 - 0.6 * math.exp(-0.3 * l)
        mod3 = _adaln(c, w_ada[l], b_ada[l]).reshape(bsz, 6, d)
        gains = jnp.stack([nat_q_g[l] * scale, nat_k_g[l], ones, diff_q_g[l] * (scale * LOG2_E), diff_k_g[l],
                           ones, ones, ones])
        qkv = _inproj(xf, mod3, norm1_g[l], w_in[l], gains, seq)
        nat = _nat_attention(qkv, nat_rpb[l], bsz, seq, ns)
        dif = _diff_attention(qkv, t5_bias, diff_lambda[l], diff_sub_g[l], lambda_init, bsz, seq, ns)
        rw_pad = jnp.pad(router_w[l], ((0, 0), (0, LANES - n_exp)))
        x1, h2, logits_t = _outproj(nat, dif, xf, mod3, w_out[l], norm2_g[l], rw_pad, router_b[l], seq)
        xf = _moe(h2, logits_t, x1, mod3, w_gate_up[l], b_gate_up[l], w_down[l], b_down[l], seq)
    return xf.reshape(bsz, seq, d)
```
